```python
import jax, jax.numpy as jnp
from jax import lax
import numpy as np

D_MODEL = 1024
BATCH = 8
SEQ = 2048
DEPTH = 1
DEC_BATCH = 128
DEC_SEQ = 8
PAST_LEN = 16384
PAGE_SIZE = 128

D_MIX = D_MODEL
LRU_WIDTH = D_MIX // 2
LRU_HEADS = 8
LRU_HEAD_DIM = LRU_WIDTH // LRU_HEADS
LRU_C = 8.0
CONV_W = 4
POOL_WIDTH = D_MIX - LRU_WIDTH
POOL_WINDOWS = (2, 4, 8, 16)
POOL_GROUPS = len(POOL_WINDOWS)
POOL_GROUP_DIM = POOL_WIDTH // POOL_GROUPS
POOL_MAX = max(POOL_WINDOWS)
D_FF = 2816
MEM_LEN = 256
XA_HEADS = 4
XA_HEAD_DIM = D_MODEL // XA_HEADS
EPS = 1e-6

kernel_name = "hymba_rglru_pool_macaron_memxattn_step"


def rmsnorm(x, g):
    xf = x.astype(jnp.float32)
    var = jnp.mean(xf * xf, axis=-1, keepdims=True)
    return (xf * lax.rsqrt(var + EPS) * g.astype(jnp.float32)).astype(x.dtype)


def swiglu(x, w_gate, w_up, w_down):
    return (jax.nn.silu(x @ w_gate) * (x @ w_up)) @ w_down


def block_diag(u, w):
    b, t, _ = u.shape
    g, dg, _ = w.shape
    uh = u.reshape(b, t, g, dg)
    return jnp.einsum('btgi,gij->btgj', uh, w).reshape(b, t, g * dg)


def causal_conv(u, buf, w, bias):
    t = u.shape[1]
    ext = jnp.concatenate([buf, u], axis=1)
    out = bias + sum(ext[:, k:k + t] * w[k] for k in range(CONV_W))
    return out, ext[:, -(CONV_W - 1):]


def rglru(u, h0, wa, ba, wx, bx, lam):
    r = jax.nn.sigmoid(block_diag(u, wa) + ba).astype(jnp.float32)
    i = jax.nn.sigmoid(block_diag(u, wx) + bx).astype(jnp.float32)
    log_a = -LRU_C * r * jax.nn.softplus(-lam.astype(jnp.float32))
    a = jnp.exp(log_a)
    mult = jnp.sqrt(jnp.maximum(-jnp.expm1(2.0 * log_a), 0.0))
    bterm = mult * i * u.astype(jnp.float32)

    def step(h, ab):
        a_t, b_t = ab
        h = a_t * h + b_t
        return h, h

    h_last, hs = lax.scan(step, h0.astype(jnp.float32),
                          (jnp.swapaxes(a, 0, 1), jnp.swapaxes(bterm, 0, 1)))
    return jnp.swapaxes(hs, 0, 1).astype(u.dtype), h_last.astype(h0.dtype)


def pool_mixer(u, buf, pos0, w, scale):
    b, t, c = u.shape
    ext = jnp.concatenate([buf, u], axis=1)
    cs = jnp.cumsum(ext.astype(jnp.float32), axis=1)
    cs = jnp.pad(cs, ((0, 0), (1, 0), (0, 0)))
    csg = cs.reshape(b, t + POOL_MAX, POOL_GROUPS, POOL_GROUP_DIM)
    pos = pos0 + jnp.arange(t)
    pooled = []
    for g, win in enumerate(POOL_WINDOWS):
        s = csg[:, POOL_MAX:POOL_MAX + t, g] - csg[:, POOL_MAX - win:POOL_MAX - win + t, g]
        cnt = jnp.minimum(pos + 1, win).astype(jnp.float32)
        pooled.append(s / cnt[None, :, None])
    pooled = jnp.stack(pooled, axis=2).reshape(b, t, c)
    delta = (pooled - u.astype(jnp.float32)).astype(u.dtype)
    out = scale * block_diag(delta, w)
    return out, ext[:, -(POOL_MAX - 1):]


def mem_kv(mem, g, wk, wv):
    m = rmsnorm(mem, g)
    b = mem.shape[0]
    k = (m @ wk).reshape(b, MEM_LEN, XA_HEADS, XA_HEAD_DIM)
    v = (m @ wv).reshape(b, MEM_LEN, XA_HEADS, XA_HEAD_DIM)
    return k, v


def cross_attend(h, k, v, wq, wo):
    b, t, _ = h.shape
    q = (h @ wq).reshape(b, t, XA_HEADS, XA_HEAD_DIM)
    s = jnp.einsum('bthd,bmhd->bhtm', q, k).astype(jnp.float32) * (XA_HEAD_DIM ** -0.5)
    p = jax.nn.softmax(s, axis=-1).astype(v.dtype)
    o = jnp.einsum('bhtm,bmhd->bthd', p, v).reshape(b, t, D_MODEL)
    return o @ wo


def layer_forward(x, conv_buf, lru_h, pool_buf, mem_k, mem_v, pos0, p):
    x = x + 0.5 * swiglu(rmsnorm(x, p['ffn1_norm']), p['ffn1_w_gate'], p['ffn1_w_up'], p['ffn1_w_down'])
    h = rmsnorm(x, p['mix_norm'])
    proj = h @ p['w_in']
    u_lru, gate, u_pool = jnp.split(proj, [LRU_WIDTH, 2 * LRU_WIDTH], axis=-1)
    u_conv, conv_buf = causal_conv(u_lru, conv_buf, p['conv_w'], p['conv_b'])
    hs, lru_h = rglru(u_conv, lru_h, p['lru_wa'], p['lru_ba'], p['lru_wx'], p['lru_bx'], p['lru_lambda'])
    y_lru = jax.nn.gelu(gate) * hs
    y_pool, pool_buf = pool_mixer(u_pool, pool_buf, pos0, p['pool_w'], p['pool_scale'])
    x = x + jnp.concatenate([y_lru, y_pool], axis=-1) @ p['w_out']
    x = x + cross_attend(rmsnorm(x, p['xattn_norm']), mem_k, mem_v, p['xattn_wq'], p['xattn_wo'])
    x = x + 0.5 * swiglu(rmsnorm(x, p['ffn2_norm']), p['ffn2_w_gate'], p['ffn2_w_up'], p['ffn2_w_down'])
    return x, conv_buf, lru_h, pool_buf


def setup_inputs(seed: int = 0) -> dict:
    key = jax.random.key(seed)
    ks = iter(jax.random.split(key, 48))

    def nrm(shape, scale=1.0):
        return jax.random.normal(next(ks), shape, jnp.float32) * scale

    def gain(n=D_MODEL):
        return 1.0 + nrm((DEPTH, n), 0.02)

    d = D_MODEL
    a8 = jax.random.uniform(next(ks), (DEPTH, LRU_WIDTH), jnp.float32, 0.9, 0.999)
    pa = a8 ** (1.0 / LRU_C)
    lru_lambda = jnp.log(pa) - jnp.log1p(-pa)
    return {
        'x_prompt': nrm((BATCH, SEQ, d)),
        'x_sample': nrm((DEC_BATCH, DEC_SEQ, d)),
        'mem_prompt': nrm((BATCH, MEM_LEN, d)),
        'state_conv': nrm((DEPTH, DEC_BATCH, CONV_W - 1, LRU_WIDTH)),
        'state_lru': nrm((DEPTH, DEC_BATCH, LRU_WIDTH), 0.5),
        'state_pool': nrm((DEPTH, DEC_BATCH, POOL_MAX - 1, POOL_WIDTH)),
        'cache_mem_k': nrm((DEPTH, DEC_BATCH, MEM_LEN, XA_HEADS, XA_HEAD_DIM)),
        'cache_mem_v': nrm((DEPTH, DEC_BATCH, MEM_LEN, XA_HEADS, XA_HEAD_DIM)),
        'ffn1_norm': gain(),
        'ffn1_w_gate': nrm((DEPTH, d, D_FF), d ** -0.5),
        'ffn1_w_up': nrm((DEPTH, d, D_FF), d ** -0.5),
        'ffn1_w_down': nrm((DEPTH, D_FF, d), D_FF ** -0.5),
        'mix_norm': gain(),
        'w_in': nrm((DEPTH, d, 2 * LRU_WIDTH + POOL_WIDTH), d ** -0.5),
        'conv_w': nrm((DEPTH, CONV_W, LRU_WIDTH), CONV_W ** -0.5),
        'conv_b': nrm((DEPTH, LRU_WIDTH), 0.01),
        'lru_wa': nrm((DEPTH, LRU_HEADS, LRU_HEAD_DIM, LRU_HEAD_DIM), LRU_HEAD_DIM ** -0.5),
        'lru_ba': nrm((DEPTH, LRU_WIDTH), 0.01),
        'lru_wx': nrm((DEPTH, LRU_HEADS, LRU_HEAD_DIM, LRU_HEAD_DIM), LRU_HEAD_DIM ** -0.5),
        'lru_bx': nrm((DEPTH, LRU_WIDTH), 0.01),
        'lru_lambda': lru_lambda,
        'pool_w': nrm((DEPTH, POOL_GROUPS, POOL_GROUP_DIM, POOL_GROUP_DIM), POOL_GROUP_DIM ** -0.5),
        'pool_scale': gain(POOL_WIDTH),
        'w_out': nrm((DEPTH, D_MIX, d), D_MIX ** -0.5),
        'xattn_norm': gain(),
        'mem_norm': gain(),
        'xattn_wq': nrm((DEPTH, d, d), d ** -0.5),
        'xattn_wk': nrm((DEPTH, d, d), d ** -0.5),
        'xattn_wv': nrm((DEPTH, d, d), d ** -0.5),
        'xattn_wo': nrm((DEPTH, d, d), d ** -0.5),
        'ffn2_norm': gain(),
        'ffn2_w_gate': nrm((DEPTH, d, D_FF), d ** -0.5),
        'ffn2_w_up': nrm((DEPTH, d, D_FF), d ** -0.5),
        'ffn2_w_down': nrm((DEPTH, D_FF, d), D_FF ** -0.5),
        'final_norm': 1.0 + nrm((D_MODEL,), 0.02),
    }


def reference(x_prompt, x_sample, mem_prompt, state_conv, state_lru, state_pool, cache_mem_k, cache_mem_v,
              ffn1_norm, ffn1_w_gate, ffn1_w_up, ffn1_w_down,
              mix_norm, w_in, conv_w, conv_b, lru_wa, lru_ba, lru_wx, lru_bx, lru_lambda, pool_w, pool_scale, w_out,
              xattn_norm, mem_norm, xattn_wq, xattn_wk, xattn_wv, xattn_wo,
              ffn2_norm, ffn2_w_gate, ffn2_w_up, ffn2_w_down,
              final_norm):
    yp, ys = x_prompt, x_sample
    b = x_prompt.shape[0]
    zero_conv = jnp.zeros((b, CONV_W - 1, LRU_WIDTH), x_prompt.dtype)
    zero_h = jnp.zeros((b, LRU_WIDTH), state_lru.dtype)
    zero_pool = jnp.zeros((b, POOL_MAX - 1, POOL_WIDTH), x_prompt.dtype)
    p_conv, p_lru, p_pool, p_mk, p_mv = [], [], [], [], []
    s_conv, s_lru, s_pool = [], [], []
    for l in range(DEPTH):
        prm = {
            'ffn1_norm': ffn1_norm[l], 'ffn1_w_gate': ffn1_w_gate[l], 'ffn1_w_up': ffn1_w_up[l],
            'ffn1_w_down': ffn1_w_down[l], 'mix_norm': mix_norm[l], 'w_in': w_in[l],
            'conv_w': conv_w[l], 'conv_b': conv_b[l], 'lru_wa': lru_wa[l], 'lru_ba': lru_ba[l],
            'lru_wx': lru_wx[l], 'lru_bx': lru_bx[l], 'lru_lambda': lru_lambda[l],
            'pool_w': pool_w[l], 'pool_scale': pool_scale[l], 'w_out': w_out[l],
            'xattn_norm': xattn_norm[l], 'xattn_wq': xattn_wq[l], 'xattn_wo': xattn_wo[l],
            'ffn2_norm': ffn2_norm[l], 'ffn2_w_gate': ffn2_w_gate[l], 'ffn2_w_up': ffn2_w_up[l],
            'ffn2_w_down': ffn2_w_down[l],
        }
        mk, mv = mem_kv(mem_prompt, mem_norm[l], xattn_wk[l], xattn_wv[l])
        yp, pc, ph, pb = layer_forward(yp, zero_conv, zero_h, zero_pool, mk, mv, 0, prm)
        p_conv.append(pc); p_lru.append(ph); p_pool.append(pb); p_mk.append(mk); p_mv.append(mv)
        ys, sc, sh, sb = layer_forward(ys, state_conv[l], state_lru[l], state_pool[l],
                                       cache_mem_k[l], cache_mem_v[l], PAST_LEN, prm)
        s_conv.append(sc); s_lru.append(sh); s_pool.append(sb)
    y_prompt = rmsnorm(yp, final_norm)
    y_sample = rmsnorm(ys, final_norm)
    return (y_prompt, y_sample,
            jnp.stack(p_conv), jnp.stack(p_lru), jnp.stack(p_pool), jnp.stack(p_mk), jnp.stack(p_mv),
            jnp.stack(s_conv), jnp.stack(s_lru), jnp.stack(s_pool))
```

```python
import functools

import jax
import jax.numpy as jnp
from jax import lax
from jax.experimental import pallas as pl
from jax.experimental.pallas import tpu as pltpu

F32 = jnp.float32
BF16 = jnp.bfloat16

D_MODEL = 1024
D_FF = 2816
LRU_WIDTH = 512
LRU_HEADS = 8
LRU_C = 8.0
CONV_W = 4
POOL_WIDTH = 512
POOL_WINDOWS = (2, 4, 8, 16)
POOL_GROUP_DIM = POOL_WIDTH // len(POOL_WINDOWS)
POOL_HIST = max(POOL_WINDOWS) - 1
CONV_HIST = CONV_W - 1
MEM_LEN = 256
XA_HEADS = 4
XA_HEAD_DIM = D_MODEL // XA_HEADS
EPS = 1e-6
PAST_LEN = 16384

MXU_TILE = 256
FF_CHUNK = MXU_TILE
N_FF_CHUNKS = D_FF // FF_CHUNK
ROW_TILE = 512
VMEM_LIMIT_BYTES = 56 * 1024 * 1024


def _rmsnorm(x, g):
    var = jnp.mean(x * x, axis=-1, keepdims=True)
    return x * lax.rsqrt(var + EPS) * g


def _dot(a, b):
    return jnp.dot(a, b, preferred_element_type=F32)


def _const_spec(shape):
    zeros = (0,) * len(shape)
    return pl.BlockSpec(shape, lambda *_: zeros, pipeline_mode=pl.Buffered(1))


def _params(semantics):
    return pltpu.CompilerParams(dimension_semantics=semantics, vmem_limit_bytes=VMEM_LIMIT_BYTES)


def _ffn_kernel(x_ref, g_ref, wg_ref, wu_ref, wd_ref, fn_ref, o_ref, h_scr, acc_scr, *, final_norm):
    x = x_ref[...]
    h_scr[...] = _rmsnorm(x, g_ref[...]).astype(BF16)
    acc_scr[...] = jnp.zeros_like(acc_scr)

    def chunk(c, carry):
        h = h_scr[...]
        g = _dot(h, wg_ref[c])
        u = _dot(h, wu_ref[c])
        a = (g * jax.nn.sigmoid(g) * u).astype(BF16)
        acc_scr[...] += _dot(a, wd_ref[c])
        return carry

    lax.fori_loop(0, N_FF_CHUNKS, chunk, 0)
    y = x + 0.5 * acc_scr[...]
    if final_norm:
        y = _rmsnorm(y, fn_ref[...])
    o_ref[...] = y


def _ffn(x, g, wg, wu, wd, fn, *, final_norm):
    rows = x.shape[0]
    row_spec = pl.BlockSpec((ROW_TILE, D_MODEL), lambda i: (i, 0))
    return pl.pallas_call(
        functools.partial(_ffn_kernel, final_norm=final_norm),
        grid=(rows // ROW_TILE,),
        in_specs=[row_spec, _const_spec((1, D_MODEL)),
                  _const_spec((N_FF_CHUNKS, D_MODEL, FF_CHUNK)),
                  _const_spec((N_FF_CHUNKS, D_MODEL, FF_CHUNK)),
                  _const_spec((N_FF_CHUNKS, FF_CHUNK, D_MODEL)),
                  _const_spec((1, D_MODEL))],
        out_specs=row_spec,
        out_shape=jax.ShapeDtypeStruct((rows, D_MODEL), F32),
        scratch_shapes=[pltpu.VMEM((ROW_TILE, D_MODEL), BF16), pltpu.VMEM((ROW_TILE, D_MODEL), F32)],
        compiler_params=_params(("parallel",)),
        name="ffn_final" if final_norm else "ffn",
    )(x, g, wg, wu, wd, fn)


def _block_diag_dot(u, w_ref):
    return jnp.concatenate([_dot(u[:, :MXU_TILE], w_ref[0]), _dot(u[:, MXU_TILE:], w_ref[1])], axis=1)


def _mix_kernel(x_ref, gm_ref, win_ref, cw_ref, cb_ref, wa_ref, ba_ref, wx_ref, bx_ref, lam_ref,
                pw_ref, ps_ref, wout_ref, sconv_ref, slru_ref, spool_ref,
                o_ref, oconv_ref, olru_ref, opool_ref,
                lru_ext, pool_ext, h_scr, a_scr, b_scr, *, tt, bb, pos0):
    rows = tt * bb
    i = pl.program_id(1)

    @pl.when(i == 0)
    def _():
        lru_ext[0:CONV_HIST * bb, :] = sconv_ref[...].reshape(CONV_HIST * bb, LRU_WIDTH)
        pool_ext[0:POOL_HIST * bb, :] = spool_ref[...].reshape(POOL_HIST * bb, POOL_WIDTH)
        h_scr[...] = slru_ref[...]

    x = x_ref[...].reshape(rows, D_MODEL)
    h = _rmsnorm(x, gm_ref[...]).astype(BF16)
    proj = _dot(h, win_ref[...])
    lru_ext[CONV_HIST * bb:, :] = proj[:, :LRU_WIDTH]
    gate = proj[:, LRU_WIDTH:2 * LRU_WIDTH]
    u_pool = proj[:, 2 * LRU_WIDTH:]
    pool_ext[POOL_HIST * bb:, :] = u_pool

    cw = cw_ref[...]
    u = cb_ref[...] + sum(lru_ext[k * bb:k * bb + rows, :] * cw[k:k + 1, :] for k in range(CONV_W))
    new_conv = lru_ext[tt * bb:(tt + CONV_HIST) * bb, :]
    oconv_ref[...] = new_conv.reshape(CONV_HIST, bb, LRU_WIDTH)
    lru_ext[0:CONV_HIST * bb, :] = new_conv

    ub = u.astype(BF16)
    r = jax.nn.sigmoid(_block_diag_dot(ub, wa_ref) + ba_ref[...])
    ig = jax.nn.sigmoid(_block_diag_dot(ub, wx_ref) + bx_ref[...])
    neg_lam = -lam_ref[...]
    softplus = jnp.maximum(neg_lam, 0.0) + jnp.log1p(jnp.exp(-jnp.abs(neg_lam)))
    log_a = (-LRU_C * softplus) * r
    a = jnp.exp(log_a)
    a_scr[...] = a
    b_scr[...] = jnp.sqrt(jnp.maximum(1.0 - a * a, 0.0)) * ig * u

    hcur = h_scr[...]
    for t in range(tt):
        sl = slice(t * bb, (t + 1) * bb)
        hcur = a_scr[sl, :] * hcur + b_scr[sl, :]
        b_scr[sl, :] = hcur
    h_scr[...] = hcur
    olru_ref[...] = hcur
    y_lru = (jax.nn.gelu(gate) * b_scr[...]).astype(BF16)

    t_idx = lax.broadcasted_iota(jnp.int32, (rows, POOL_GROUP_DIM), 0) // bb
    pos = pos0 + i * tt + t_idx
    deltas = []
    for g, win in enumerate(POOL_WINDOWS):
        lanes = slice(g * POOL_GROUP_DIM, (g + 1) * POOL_GROUP_DIM)
        s = pool_ext[:, lanes]
        span = 1
        while span < win:
            s = s[span * bb:, :] + s[:-span * bb, :]
            span *= 2
        s = s[s.shape[0] - rows:, :]
        cnt = jnp.minimum(pos + 1, win).astype(F32)
        deltas.append(s / cnt - u_pool[:, lanes])
    delta = jnp.concatenate(deltas, axis=1).astype(BF16)
    y_pool = (ps_ref[...] * _block_diag_dot(delta, pw_ref)).astype(BF16)
    new_pool = pool_ext[tt * bb:(tt + POOL_HIST) * bb, :]
    opool_ref[...] = new_pool.reshape(POOL_HIST, bb, POOL_WIDTH)
    pool_ext[0:POOL_HIST * bb, :] = new_pool

    out = x + _dot(y_lru, wout_ref[0:LRU_WIDTH, :]) + _dot(y_pool, wout_ref[LRU_WIDTH:, :])
    o_ref[...] = out.reshape(tt, bb, D_MODEL)


def _mix(x_tm, w, sconv, slru, spool, *, tt, bb, pos0):
    n_t, n_b = x_tm.shape[0], x_tm.shape[1]
    rows = tt * bb
    x_spec = pl.BlockSpec((tt, bb, D_MODEL), lambda j, i: (i, j, 0))
    conv_spec = pl.BlockSpec((CONV_HIST, bb, LRU_WIDTH), lambda j, i: (0, j, 0))
    lru_spec = pl.BlockSpec((bb, LRU_WIDTH), lambda j, i: (j, 0))
    pool_spec = pl.BlockSpec((POOL_HIST, bb, POOL_WIDTH), lambda j, i: (0, j, 0))
    return pl.pallas_call(
        functools.partial(_mix_kernel, tt=tt, bb=bb, pos0=pos0),
        grid=(n_b // bb, n_t // tt),
        in_specs=[x_spec, _const_spec((1, D_MODEL)), _const_spec((D_MODEL, 3 * LRU_WIDTH)),
                  _const_spec((CONV_W, LRU_WIDTH)), _const_spec((1, LRU_WIDTH)),
                  _const_spec((2, MXU_TILE, MXU_TILE)), _const_spec((1, LRU_WIDTH)),
                  _const_spec((2, MXU_TILE, MXU_TILE)), _const_spec((1, LRU_WIDTH)),
                  _const_spec((1, LRU_WIDTH)),
                  _const_spec((2, MXU_TILE, MXU_TILE)), _const_spec((1, POOL_WIDTH)),
                  _const_spec((D_MODEL, D_MODEL)),
                  conv_spec, lru_spec, pool_spec],
        out_specs=[x_spec, conv_spec, lru_spec, pool_spec],
        out_shape=[jax.ShapeDtypeStruct(x_tm.shape, F32),
                   jax.ShapeDtypeStruct((CONV_HIST, n_b, LRU_WIDTH), F32),
                   jax.ShapeDtypeStruct((n_b, LRU_WIDTH), F32),
                   jax.ShapeDtypeStruct((POOL_HIST, n_b, POOL_WIDTH), F32)],
        scratch_shapes=[pltpu.VMEM(((CONV_HIST + tt) * bb, LRU_WIDTH), F32),
                        pltpu.VMEM(((POOL_HIST + tt) * bb, POOL_WIDTH), F32),
                        pltpu.VMEM((bb, LRU_WIDTH), F32),
                        pltpu.VMEM((rows, LRU_WIDTH), F32),
                        pltpu.VMEM((rows, LRU_WIDTH), F32)],
        compiler_params=_params(("parallel", "arbitrary")),
        name="mix",
    )(x_tm, w["mix_norm"], w["w_in"], w["conv_w"], w["conv_b"], w["lru_wa"], w["lru_ba"],
      w["lru_wx"], w["lru_bx"], w["lru_lambda"], w["pool_w"], w["pool_scale"], w["w_out"],
      sconv, slru, spool)


def _memkv_kernel(m_ref, g_ref, wk_ref, wv_ref, k_ref, v_ref):
    m = _rmsnorm(m_ref[...], g_ref[...]).astype(BF16)
    k_ref[...] = _dot(m, wk_ref[...])
    v_ref[...] = _dot(m, wv_ref[...])


def _memkv(mem, g, wk, wv):
    rows = mem.shape[0]
    row_spec = pl.BlockSpec((ROW_TILE, D_MODEL), lambda i: (i, 0))
    return pl.pallas_call(
        _memkv_kernel,
        grid=(rows // ROW_TILE,),
        in_specs=[row_spec, _const_spec((1, D_MODEL)), _const_spec((D_MODEL, D_MODEL)),
                  _const_spec((D_MODEL, D_MODEL))],
        out_specs=[row_spec, row_spec],
        out_shape=[jax.ShapeDtypeStruct((rows, D_MODEL), F32)] * 2,
        compiler_params=_params(("parallel",)),
        name="memkv",
    )(mem, g, wk, wv)


def _attend(q, k, v):
    outs = []
    for hd in range(XA_HEADS):
        cols = slice(hd * XA_HEAD_DIM, (hd + 1) * XA_HEAD_DIM)
        s = lax.dot_general(q[:, cols].astype(BF16), k[:, cols], (((1,), (1,)), ((), ())),
                            preferred_element_type=F32) * (XA_HEAD_DIM ** -0.5)
        e = jnp.exp(s - jnp.max(s, axis=-1, keepdims=True))
        p = (e / jnp.sum(e, axis=-1, keepdims=True)).astype(BF16)
        outs.append(_dot(p, v[:, cols]).astype(BF16))
    return jnp.concatenate(outs, axis=1)


def _xattn_prompt_kernel(x_ref, g_ref, wq_ref, k_ref, v_ref, wo_ref, o_ref):
    x = x_ref[...]
    q = _dot(_rmsnorm(x, g_ref[...]).astype(BF16), wq_ref[...])
    o = _attend(q, k_ref[...].astype(BF16), v_ref[...].astype(BF16))
    o_ref[...] = x + _dot(o, wo_ref[...])


def _xattn_prompt(x, g, wq, k, v, wo):
    n_b, n_t = x.shape[0], x.shape[1]
    x_spec = pl.BlockSpec((None, ROW_TILE, D_MODEL), lambda b, i: (b, i, 0))
    kv_spec = pl.BlockSpec((None, MEM_LEN, D_MODEL), lambda b, i: (b, 0, 0))
    return pl.pallas_call(
        _xattn_prompt_kernel,
        grid=(n_b, n_t // ROW_TILE),
        in_specs=[x_spec, _const_spec((1, D_MODEL)), _const_spec((D_MODEL, D_MODEL)),
                  kv_spec, kv_spec, _const_spec((D_MODEL, D_MODEL))],
        out_specs=x_spec,
        out_shape=jax.ShapeDtypeStruct(x.shape, F32),
        compiler_params=_params(("parallel", "parallel")),
        name="xattn_prompt",
    )(x, g, wq, k, v, wo)


def _xattn_sample_kernel(x_ref, g_ref, wq_ref, k_ref, v_ref, wo_ref, o_ref, att_scr, *, bc, tq):
    x = x_ref[...]
    q = _dot(_rmsnorm(x, g_ref[...]).astype(BF16), wq_ref[...])
    for b in range(bc):
        att_scr[b * tq:(b + 1) * tq, :] = _attend(
            q[b * tq:(b + 1) * tq, :], k_ref[b].astype(BF16), v_ref[b].astype(BF16)).astype(F32)
    o_ref[...] = x + _dot(att_scr[...].astype(BF16), wo_ref[...])


def _xattn_sample(x, g, wq, k, v, wo, *, bc):
    n_b = k.shape[0]
    tq = x.shape[0] // n_b
    x_spec = pl.BlockSpec((bc * tq, D_MODEL), lambda j: (j, 0))
    kv_spec = pl.BlockSpec((bc, MEM_LEN, D_MODEL), lambda j: (j, 0, 0))
    return pl.pallas_call(
        functools.partial(_xattn_sample_kernel, bc=bc, tq=tq),
        grid=(n_b // bc,),
        in_specs=[x_spec, _const_spec((1, D_MODEL)), _const_spec((D_MODEL, D_MODEL)),
                  kv_spec, kv_spec, _const_spec((D_MODEL, D_MODEL))],
        out_specs=x_spec,
        out_shape=jax.ShapeDtypeStruct(x.shape, F32),
        scratch_shapes=[pltpu.VMEM((bc * tq, D_MODEL), F32)],
        compiler_params=_params(("parallel",)),
        name="xattn_sample",
    )(x, g, wq, k, v, wo)


def _ff_cols(w):
    return w.reshape(D_MODEL, N_FF_CHUNKS, FF_CHUNK).transpose(1, 0, 2).astype(BF16)


def _block_diag_tiles(w):
    groups, dg, _ = w.shape
    per_tile = MXU_TILE // dg
    w = w.reshape(groups // per_tile, per_tile, dg, dg)
    eye = jnp.eye(per_tile, dtype=w.dtype)
    tiles = jnp.einsum("npij,pq->npiqj", w, eye)
    return tiles.reshape(groups // per_tile, MXU_TILE, MXU_TILE).astype(BF16)


def _row(v):
    return v.reshape(1, -1).astype(F32)


def _layer(x_prompt, x_sample, mem_prompt, sconv, slru, spool, cache_k, cache_v, p, final_norm):
    n_b, n_t, _ = x_prompt.shape
    s_b, s_t, _ = x_sample.shape
    ffn1 = (_row(p["ffn1_norm"]), _ff_cols(p["ffn1_w_gate"]), _ff_cols(p["ffn1_w_up"]),
            p["ffn1_w_down"].reshape(N_FF_CHUNKS, FF_CHUNK, D_MODEL).astype(BF16), _row(final_norm))
    ffn2 = (_row(p["ffn2_norm"]), _ff_cols(p["ffn2_w_gate"]), _ff_cols(p["ffn2_w_up"]),
            p["ffn2_w_down"].reshape(N_FF_CHUNKS, FF_CHUNK, D_MODEL).astype(BF16), _row(final_norm))
    mixw = {
        "mix_norm": _row(p["mix_norm"]), "w_in": p["w_in"].astype(BF16),
        "conv_w": p["conv_w"].astype(F32), "conv_b": _row(p["conv_b"]),
        "lru_wa": _block_diag_tiles(p["lru_wa"]), "lru_ba": _row(p["lru_ba"]),
        "lru_wx": _block_diag_tiles(p["lru_wx"]), "lru_bx": _row(p["lru_bx"]),
        "lru_lambda": _row(p["lru_lambda"]),
        "pool_w": _block_diag_tiles(p["pool_w"]), "pool_scale": _row(p["pool_scale"]),
        "w_out": p["w_out"].astype(BF16),
    }
    gx, wq, wo = _row(p["xattn_norm"]), p["xattn_wq"].astype(BF16), p["xattn_wo"].astype(BF16)

    mk, mv = _memkv(mem_prompt.reshape(n_b * MEM_LEN, D_MODEL), _row(p["mem_norm"]),
                    p["xattn_wk"].astype(BF16), p["xattn_wv"].astype(BF16))
    mk = mk.reshape(n_b, MEM_LEN, D_MODEL)
    mv = mv.reshape(n_b, MEM_LEN, D_MODEL)

    xp = _ffn(x_prompt.reshape(n_b * n_t, D_MODEL), *ffn1, final_norm=False)
    xp_tm = jnp.swapaxes(xp.reshape(n_b, n_t, D_MODEL), 0, 1)
    xp_tm, p_conv, p_lru, p_pool = _mix(
        xp_tm, mixw, jnp.zeros((CONV_HIST, n_b, LRU_WIDTH), F32), jnp.zeros((n_b, LRU_WIDTH), F32),
        jnp.zeros((POOL_HIST, n_b, POOL_WIDTH), F32), tt=ROW_TILE // n_b, bb=n_b, pos0=0)
    xp = _xattn_prompt(jnp.swapaxes(xp_tm, 0, 1), gx, wq, mk, mv, wo)
    yp = _ffn(xp.reshape(n_b * n_t, D_MODEL), *ffn2, final_norm=True).reshape(n_b, n_t, D_MODEL)

    xs = _ffn(x_sample.reshape(s_b * s_t, D_MODEL), *ffn1, final_norm=False)
    xs_tm = jnp.swapaxes(xs.reshape(s_b, s_t, D_MODEL), 0, 1)
    bb = ROW_TILE // s_t
    xs_tm, s_conv, s_lru, s_pool = _mix(
        xs_tm, mixw, jnp.swapaxes(sconv, 0, 1), slru, jnp.swapaxes(spool, 0, 1),
        tt=s_t, bb=bb, pos0=PAST_LEN)
    xs = jnp.swapaxes(xs_tm, 0, 1).reshape(s_b * s_t, D_MODEL)
    xs = _xattn_sample(xs, gx, wq, cache_k.reshape(s_b, MEM_LEN, D_MODEL),
                       cache_v.reshape(s_b, MEM_LEN, D_MODEL), wo, bc=8)
    ys = _ffn(xs, *ffn2, final_norm=True).reshape(s_b, s_t, D_MODEL)

    states = (jnp.swapaxes(p_conv, 0, 1), p_lru, jnp.swapaxes(p_pool, 0, 1),
              mk.reshape(n_b, MEM_LEN, XA_HEADS, XA_HEAD_DIM), mv.reshape(n_b, MEM_LEN, XA_HEADS, XA_HEAD_DIM),
              jnp.swapaxes(s_conv, 0, 1), s_lru, jnp.swapaxes(s_pool, 0, 1))
    return yp, ys, states


def kernel(x_prompt, x_sample, mem_prompt, state_conv, state_lru, state_pool, cache_mem_k, cache_mem_v, ffn1_norm, ffn1_w_gate, ffn1_w_up, ffn1_w_down, mix_norm, w_in, conv_w, conv_b, lru_wa, lru_ba, lru_wx, lru_bx, lru_lambda, pool_w, pool_scale, w_out, xattn_norm, mem_norm, xattn_wq, xattn_wk, xattn_wv, xattn_wo, ffn2_norm, ffn2_w_gate, ffn2_w_up, ffn2_w_down, final_norm):
    depth = ffn1_norm.shape[0]
    assert depth == 1, "the final RMSNorm is fused into the last layer's second FFN"
    names = ("ffn1_norm", "ffn1_w_gate", "ffn1_w_up", "ffn1_w_down", "mix_norm", "w_in", "conv_w", "conv_b",
             "lru_wa", "lru_ba", "lru_wx", "lru_bx", "lru_lambda", "pool_w", "pool_scale", "w_out",
             "xattn_norm", "mem_norm", "xattn_wq", "xattn_wk", "xattn_wv", "xattn_wo",
             "ffn2_norm", "ffn2_w_gate", "ffn2_w_up", "ffn2_w_down")
    stacked = (ffn1_norm, ffn1_w_gate, ffn1_w_up, ffn1_w_down, mix_norm, w_in, conv_w, conv_b,
               lru_wa, lru_ba, lru_wx, lru_bx, lru_lambda, pool_w, pool_scale, w_out,
               xattn_norm, mem_norm, xattn_wq, xattn_wk, xattn_wv, xattn_wo,
               ffn2_norm, ffn2_w_gate, ffn2_w_up, ffn2_w_down)
    p = {n: a[0] for n, a in zip(names, stacked)}
    yp, ys, st = _layer(x_prompt, x_sample, mem_prompt, state_conv[0], state_lru[0], state_pool[0],
                        cache_mem_k[0], cache_mem_v[0], p, final_norm)
    p_conv, p_lru, p_pool, p_mk, p_mv, s_conv, s_lru, s_pool = (s[None] for s in st)
    return (yp, ys, p_conv, p_lru, p_pool, p_mk, p_mv, s_conv, s_lru, s_pool)
```

```python
import functools

import jax
import jax.numpy as jnp
from jax import lax
from jax.experimental import pallas as pl
from jax.experimental.pallas import tpu as pltpu

F32 = jnp.float32
BF16 = jnp.bfloat16

D_MODEL = 1024
D_FF = 2816
LRU_WIDTH = 512
LRU_HEADS = 8
LRU_C = 8.0
CONV_W = 4
POOL_WIDTH = 512
POOL_WINDOWS = (2, 4, 8, 16)
POOL_GROUP_DIM = POOL_WIDTH // len(POOL_WINDOWS)
POOL_HIST = max(POOL_WINDOWS) - 1
CONV_HIST = CONV_W - 1
MEM_LEN = 256
XA_HEADS = 4
XA_HEAD_DIM = D_MODEL // XA_HEADS
EPS = 1e-6
PAST_LEN = 16384

MXU_TILE = 256
LANES = 128
assert XA_HEAD_DIM == 2 * LANES
FF_CHUNK = MXU_TILE
N_FF_CHUNKS = D_FF // FF_CHUNK
ROW_TILE = 512
VMEM_LIMIT_BYTES = 56 * 1024 * 1024


def _rmsnorm(x, g):
    var = jnp.mean(x * x, axis=-1, keepdims=True)
    return x * lax.rsqrt(var + EPS) * g


def _dot(a, b):
    return jnp.dot(a, b, preferred_element_type=F32)


def _const_spec(shape):
    zeros = (0,) * len(shape)
    return pl.BlockSpec(shape, lambda *_: zeros, pipeline_mode=pl.Buffered(1))


def _params(semantics):
    return pltpu.CompilerParams(dimension_semantics=semantics, vmem_limit_bytes=VMEM_LIMIT_BYTES)


def _ffn_kernel(x_ref, g_ref, wg_ref, wu_ref, wd_ref, fn_ref, o_ref, h_scr, acc_scr, *, final_norm):
    x = x_ref[...]
    h_scr[...] = _rmsnorm(x, g_ref[...]).astype(BF16)
    acc_scr[...] = jnp.zeros_like(acc_scr)

    def chunk(c, carry):
        h = h_scr[...]
        g = _dot(h, wg_ref[c])
        u = _dot(h, wu_ref[c])
        a = (g * jax.nn.sigmoid(g) * u).astype(BF16)
        acc_scr[...] += _dot(a, wd_ref[c])
        return carry

    lax.fori_loop(0, N_FF_CHUNKS, chunk, 0)
    y = x + 0.5 * acc_scr[...]
    if final_norm:
        y = _rmsnorm(y, fn_ref[...])
    o_ref[...] = y


def _ffn(x, g, wg, wu, wd, fn, *, final_norm):
    rows = x.shape[0]
    row_spec = pl.BlockSpec((ROW_TILE, D_MODEL), lambda i: (i, 0))
    return pl.pallas_call(
        functools.partial(_ffn_kernel, final_norm=final_norm),
        grid=(rows // ROW_TILE,),
        in_specs=[row_spec, _const_spec((1, D_MODEL)),
                  _const_spec((N_FF_CHUNKS, D_MODEL, FF_CHUNK)),
                  _const_spec((N_FF_CHUNKS, D_MODEL, FF_CHUNK)),
                  _const_spec((N_FF_CHUNKS, FF_CHUNK, D_MODEL)),
                  _const_spec((1, D_MODEL))],
        out_specs=row_spec,
        out_shape=jax.ShapeDtypeStruct((rows, D_MODEL), F32),
        scratch_shapes=[pltpu.VMEM((ROW_TILE, D_MODEL), BF16), pltpu.VMEM((ROW_TILE, D_MODEL), F32)],
        compiler_params=_params(("parallel",)),
        name="ffn_final" if final_norm else "ffn",
    )(x, g, wg, wu, wd, fn)


def _block_diag_dot(u, w_ref):
    return jnp.concatenate([_dot(u[:, :MXU_TILE], w_ref[0]), _dot(u[:, MXU_TILE:], w_ref[1])], axis=1)


def _mix_kernel(x_ref, gm_ref, win_ref, cw_ref, cb_ref, wa_ref, ba_ref, wx_ref, bx_ref, lam_ref,
                pw_ref, ps_ref, wout_ref, sconv_ref, slru_ref, spool_ref,
                o_ref, oconv_ref, olru_ref, opool_ref,
                lru_ext, pool_ext, h_scr, a_scr, b_scr, *, tt, bb, pos0):
    rows = tt * bb
    i = pl.program_id(1)

    @pl.when(i == 0)
    def _():
        lru_ext[0:CONV_HIST * bb, :] = sconv_ref[...].reshape(CONV_HIST * bb, LRU_WIDTH)
        pool_ext[0:POOL_HIST * bb, :] = spool_ref[...].reshape(POOL_HIST * bb, POOL_WIDTH)
        h_scr[...] = slru_ref[...]

    x = jnp.swapaxes(x_ref[...], 0, 1).reshape(rows, D_MODEL)
    h = _rmsnorm(x, gm_ref[...]).astype(BF16)
    proj = _dot(h, win_ref[...])
    lru_ext[CONV_HIST * bb:, :] = proj[:, :LRU_WIDTH]
    gate = proj[:, LRU_WIDTH:2 * LRU_WIDTH]
    u_pool = proj[:, 2 * LRU_WIDTH:]
    pool_ext[POOL_HIST * bb:, :] = u_pool

    cw = cw_ref[...]
    u = cb_ref[...] + sum(lru_ext[k * bb:k * bb + rows, :] * cw[k:k + 1, :] for k in range(CONV_W))
    new_conv = lru_ext[tt * bb:(tt + CONV_HIST) * bb, :]
    oconv_ref[...] = new_conv.reshape(CONV_HIST, bb, LRU_WIDTH)
    lru_ext[0:CONV_HIST * bb, :] = new_conv

    ub = u.astype(BF16)
    r = jax.nn.sigmoid(_block_diag_dot(ub, wa_ref) + ba_ref[...])
    ig = jax.nn.sigmoid(_block_diag_dot(ub, wx_ref) + bx_ref[...])
    neg_lam = -lam_ref[...]
    softplus = jnp.maximum(neg_lam, 0.0) + jnp.log1p(jnp.exp(-jnp.abs(neg_lam)))
    log_a = (-LRU_C * softplus) * r
    a = jnp.exp(log_a)
    a_scr[...] = a
    b_scr[...] = jnp.sqrt(jnp.maximum(1.0 - a * a, 0.0)) * ig * u

    hcur = h_scr[...]
    for t in range(tt):
        sl = slice(t * bb, (t + 1) * bb)
        hcur = a_scr[sl, :] * hcur + b_scr[sl, :]
        b_scr[sl, :] = hcur
    h_scr[...] = hcur
    olru_ref[...] = hcur
    y_lru = (jax.nn.gelu(gate) * b_scr[...]).astype(BF16)

    t_idx = lax.broadcasted_iota(jnp.int32, (rows, POOL_GROUP_DIM), 0) // bb
    pos = pos0 + i * tt + t_idx
    deltas = []
    for g, win in enumerate(POOL_WINDOWS):
        lanes = slice(g * POOL_GROUP_DIM, (g + 1) * POOL_GROUP_DIM)
        s = pool_ext[:, lanes]
        span = 1
        while span < win:
            s = s[span * bb:, :] + s[:-span * bb, :]
            span *= 2
        s = s[s.shape[0] - rows:, :]
        cnt = jnp.minimum(pos + 1, win).astype(F32)
        deltas.append(s / cnt - u_pool[:, lanes])
    delta = jnp.concatenate(deltas, axis=1).astype(BF16)
    y_pool = (ps_ref[...] * _block_diag_dot(delta, pw_ref)).astype(BF16)
    new_pool = pool_ext[tt * bb:(tt + POOL_HIST) * bb, :]
    opool_ref[...] = new_pool.reshape(POOL_HIST, bb, POOL_WIDTH)
    pool_ext[0:POOL_HIST * bb, :] = new_pool

    out = x + _dot(y_lru, wout_ref[0:LRU_WIDTH, :]) + _dot(y_pool, wout_ref[LRU_WIDTH:, :])
    o_ref[...] = jnp.swapaxes(out.reshape(tt, bb, D_MODEL), 0, 1)


def _mix(x, w, sconv, slru, spool, *, tt, bb, pos0):
    n_b, n_t = x.shape[0], x.shape[1]
    rows = tt * bb
    x_spec = pl.BlockSpec((bb, tt, D_MODEL), lambda j, i: (j, i, 0))
    conv_spec = pl.BlockSpec((CONV_HIST, bb, LRU_WIDTH), lambda j, i: (0, j, 0))
    lru_spec = pl.BlockSpec((bb, LRU_WIDTH), lambda j, i: (j, 0))
    pool_spec = pl.BlockSpec((POOL_HIST, bb, POOL_WIDTH), lambda j, i: (0, j, 0))
    return pl.pallas_call(
        functools.partial(_mix_kernel, tt=tt, bb=bb, pos0=pos0),
        grid=(n_b // bb, n_t // tt),
        in_specs=[x_spec, _const_spec((1, D_MODEL)), _const_spec((D_MODEL, 3 * LRU_WIDTH)),
                  _const_spec((CONV_W, LRU_WIDTH)), _const_spec((1, LRU_WIDTH)),
                  _const_spec((2, MXU_TILE, MXU_TILE)), _const_spec((1, LRU_WIDTH)),
                  _const_spec((2, MXU_TILE, MXU_TILE)), _const_spec((1, LRU_WIDTH)),
                  _const_spec((1, LRU_WIDTH)),
                  _const_spec((2, MXU_TILE, MXU_TILE)), _const_spec((1, POOL_WIDTH)),
                  _const_spec((D_MODEL, D_MODEL)),
                  conv_spec, lru_spec, pool_spec],
        out_specs=[x_spec, conv_spec, lru_spec, pool_spec],
        out_shape=[jax.ShapeDtypeStruct(x.shape, F32),
                   jax.ShapeDtypeStruct((CONV_HIST, n_b, LRU_WIDTH), F32),
                   jax.ShapeDtypeStruct((n_b, LRU_WIDTH), F32),
                   jax.ShapeDtypeStruct((POOL_HIST, n_b, POOL_WIDTH), F32)],
        scratch_shapes=[pltpu.VMEM(((CONV_HIST + tt) * bb, LRU_WIDTH), F32),
                        pltpu.VMEM(((POOL_HIST + tt) * bb, POOL_WIDTH), F32),
                        pltpu.VMEM((bb, LRU_WIDTH), F32),
                        pltpu.VMEM((rows, LRU_WIDTH), F32),
                        pltpu.VMEM((rows, LRU_WIDTH), F32)],
        compiler_params=_params(("parallel", "arbitrary")),
        name="mix",
    )(x, w["mix_norm"], w["w_in"], w["conv_w"], w["conv_b"], w["lru_wa"], w["lru_ba"],
      w["lru_wx"], w["lru_bx"], w["lru_lambda"], w["pool_w"], w["pool_scale"], w["w_out"],
      sconv, slru, spool)


def _memkv_kernel(m_ref, g_ref, wk_ref, wv_ref, k_ref, v_ref, kb_ref, vb_ref):
    m = _rmsnorm(m_ref[0], g_ref[...]).astype(BF16)
    for w_ref, o_ref, ob_ref in ((wk_ref, k_ref, kb_ref), (wv_ref, v_ref, vb_ref)):
        kv = _dot(m, w_ref[...])
        ob_ref[0] = kv.astype(BF16)
        for hd in range(XA_HEADS):
            o_ref[0, :, hd, :] = kv[:, hd * XA_HEAD_DIM:(hd + 1) * XA_HEAD_DIM]


def _memkv(mem, g, wk, wv):
    n_b = mem.shape[0]
    row_spec = pl.BlockSpec((1, MEM_LEN, D_MODEL), lambda i: (i, 0, 0))
    head_spec = pl.BlockSpec((1, MEM_LEN, XA_HEADS, XA_HEAD_DIM), lambda i: (i, 0, 0, 0))
    return pl.pallas_call(
        _memkv_kernel,
        grid=(n_b,),
        in_specs=[row_spec, _const_spec((1, D_MODEL)), _const_spec((D_MODEL, D_MODEL)),
                  _const_spec((D_MODEL, D_MODEL))],
        out_specs=[head_spec, head_spec, row_spec, row_spec],
        out_shape=[jax.ShapeDtypeStruct((n_b, MEM_LEN, XA_HEADS, XA_HEAD_DIM), F32)] * 2
        + [jax.ShapeDtypeStruct((n_b, MEM_LEN, D_MODEL), BF16)] * 2,
        compiler_params=_params(("parallel",)),
        name="memkv",
    )(mem, g, wk, wv)


def _attend(q, head_k, head_v):
    outs = []
    for hd in range(XA_HEADS):
        cols = slice(hd * XA_HEAD_DIM, (hd + 1) * XA_HEAD_DIM)
        s = lax.dot_general(q[:, cols].astype(BF16), head_k(hd), (((1,), (1,)), ((), ())),
                            preferred_element_type=F32) * (XA_HEAD_DIM ** -0.5)
        e = jnp.exp(s - jnp.max(s, axis=-1, keepdims=True))
        p = (e / jnp.sum(e, axis=-1, keepdims=True)).astype(BF16)
        outs.append(_dot(p, head_v(hd)).astype(BF16))
    return jnp.concatenate(outs, axis=1)


def _xattn_prompt_kernel(x_ref, g_ref, wq_ref, k_ref, v_ref, wo_ref, o_ref):
    x = x_ref[...]
    q = _dot(_rmsnorm(x, g_ref[...]).astype(BF16), wq_ref[...])
    o = _attend(q, lambda hd: k_ref[:, hd * XA_HEAD_DIM:(hd + 1) * XA_HEAD_DIM],
                lambda hd: v_ref[:, hd * XA_HEAD_DIM:(hd + 1) * XA_HEAD_DIM])
    o_ref[...] = x + _dot(o, wo_ref[...])


def _xattn_prompt(x, g, wq, k, v, wo):
    n_b, n_t = x.shape[0], x.shape[1]
    x_spec = pl.BlockSpec((None, ROW_TILE, D_MODEL), lambda b, i: (b, i, 0))
    kv_spec = pl.BlockSpec((None, MEM_LEN, D_MODEL), lambda b, i: (b, 0, 0))
    return pl.pallas_call(
        _xattn_prompt_kernel,
        grid=(n_b, n_t // ROW_TILE),
        in_specs=[x_spec, _const_spec((1, D_MODEL)), _const_spec((D_MODEL, D_MODEL)),
                  kv_spec, kv_spec, _const_spec((D_MODEL, D_MODEL))],
        out_specs=x_spec,
        out_shape=jax.ShapeDtypeStruct(x.shape, F32),
        compiler_params=_params(("parallel", "parallel")),
        name="xattn_prompt",
    )(x, g, wq, k, v, wo)


def _xattn_sample_kernel(x_ref, g_ref, wq_ref, k0_ref, k1_ref, v0_ref, v1_ref, wo_ref, o_ref, att_scr,
                         *, bc, tq):
    x = x_ref[...]
    q = _dot(_rmsnorm(x, g_ref[...]).astype(BF16), wq_ref[...])
    halves = [[r.reshape(bc * MEM_LEN * XA_HEADS, LANES) for r in pair]
              for pair in ((k0_ref, k1_ref), (v0_ref, v1_ref))]

    def head(pair, b, hd):
        rows = pl.ds(b * MEM_LEN * XA_HEADS + hd, MEM_LEN, stride=XA_HEADS)
        return jnp.concatenate([r[rows, :] for r in pair], axis=1).astype(BF16)

    for b in range(bc):
        att_scr[b * tq:(b + 1) * tq, :] = _attend(
            q[b * tq:(b + 1) * tq, :], functools.partial(head, halves[0], b),
            functools.partial(head, halves[1], b)).astype(F32)
    o_ref[...] = x + _dot(att_scr[...].astype(BF16), wo_ref[...])


def _xattn_sample(x, g, wq, k, v, wo, *, bc):
    n_b = k.shape[0]
    tq = x.shape[0] // n_b
    x_spec = pl.BlockSpec((bc * tq, D_MODEL), lambda j: (j, 0))
    kv_specs = [pl.BlockSpec((bc, MEM_LEN, XA_HEADS, LANES), functools.partial(lambda half, j: (j, 0, 0, half), half))
                for half in range(XA_HEAD_DIM // LANES)]
    return pl.pallas_call(
        functools.partial(_xattn_sample_kernel, bc=bc, tq=tq),
        grid=(n_b // bc,),
        in_specs=[x_spec, _const_spec((1, D_MODEL)), _const_spec((D_MODEL, D_MODEL)),
                  *kv_specs, *kv_specs, _const_spec((D_MODEL, D_MODEL))],
        out_specs=x_spec,
        out_shape=jax.ShapeDtypeStruct(x.shape, F32),
        scratch_shapes=[pltpu.VMEM((bc * tq, D_MODEL), F32)],
        compiler_params=_params(("parallel",)),
        name="xattn_sample",
    )(x, g, wq, k, k, v, v, wo)


def _ff_cols(w):
    return w.reshape(D_MODEL, N_FF_CHUNKS, FF_CHUNK).transpose(1, 0, 2).astype(BF16)


def _block_diag_tiles(w):
    groups, dg, _ = w.shape
    per_tile = MXU_TILE // dg
    w = w.reshape(groups // per_tile, per_tile, dg, dg)
    eye = jnp.eye(per_tile, dtype=w.dtype)
    tiles = jnp.einsum("npij,pq->npiqj", w, eye)
    return tiles.reshape(groups // per_tile, MXU_TILE, MXU_TILE).astype(BF16)


def _row(v):
    return v.reshape(1, -1).astype(F32)


def _layer(x_prompt, x_sample, mem_prompt, sconv, slru, spool, cache_k, cache_v, p, final_norm):
    n_b, n_t, _ = x_prompt.shape
    s_b, s_t, _ = x_sample.shape
    ffn1 = (_row(p["ffn1_norm"]), _ff_cols(p["ffn1_w_gate"]), _ff_cols(p["ffn1_w_up"]),
            p["ffn1_w_down"].reshape(N_FF_CHUNKS, FF_CHUNK, D_MODEL).astype(BF16), _row(final_norm))
    ffn2 = (_row(p["ffn2_norm"]), _ff_cols(p["ffn2_w_gate"]), _ff_cols(p["ffn2_w_up"]),
            p["ffn2_w_down"].reshape(N_FF_CHUNKS, FF_CHUNK, D_MODEL).astype(BF16), _row(final_norm))
    mixw = {
        "mix_norm": _row(p["mix_norm"]), "w_in": p["w_in"].astype(BF16),
        "conv_w": p["conv_w"].astype(F32), "conv_b": _row(p["conv_b"]),
        "lru_wa": _block_diag_tiles(p["lru_wa"]), "lru_ba": _row(p["lru_ba"]),
        "lru_wx": _block_diag_tiles(p["lru_wx"]), "lru_bx": _row(p["lru_bx"]),
        "lru_lambda": _row(p["lru_lambda"]),
        "pool_w": _block_diag_tiles(p["pool_w"]), "pool_scale": _row(p["pool_scale"]),
        "w_out": p["w_out"].astype(BF16),
    }
    gx, wq, wo = _row(p["xattn_norm"]), p["xattn_wq"].astype(BF16), p["xattn_wo"].astype(BF16)

    mk, mv, mk_bf, mv_bf = _memkv(mem_prompt, _row(p["mem_norm"]),
                                  p["xattn_wk"].astype(BF16), p["xattn_wv"].astype(BF16))

    xp = _ffn(x_prompt.reshape(n_b * n_t, D_MODEL), *ffn1, final_norm=False)
    xp, p_conv, p_lru, p_pool = _mix(
        xp.reshape(n_b, n_t, D_MODEL), mixw, jnp.zeros((CONV_HIST, n_b, LRU_WIDTH), F32),
        jnp.zeros((n_b, LRU_WIDTH), F32), jnp.zeros((POOL_HIST, n_b, POOL_WIDTH), F32),
        tt=ROW_TILE // n_b, bb=n_b, pos0=0)
    xp = _xattn_prompt(xp, gx, wq, mk_bf, mv_bf, wo)
    yp = _ffn(xp.reshape(n_b * n_t, D_MODEL), *ffn2, final_norm=True).reshape(n_b, n_t, D_MODEL)

    xs = _ffn(x_sample.reshape(s_b * s_t, D_MODEL), *ffn1, final_norm=False)
    xs, s_conv, s_lru, s_pool = _mix(
        xs.reshape(s_b, s_t, D_MODEL), mixw, jnp.swapaxes(sconv, 0, 1), slru, jnp.swapaxes(spool, 0, 1),
        tt=s_t, bb=ROW_TILE // s_t, pos0=PAST_LEN)
    xs = _xattn_sample(xs.reshape(s_b * s_t, D_MODEL), gx, wq, cache_k, cache_v, wo, bc=8)
    ys = _ffn(xs, *ffn2, final_norm=True).reshape(s_b, s_t, D_MODEL)

    states = (jnp.swapaxes(p_conv, 0, 1), p_lru, jnp.swapaxes(p_pool, 0, 1), mk, mv,
              jnp.swapaxes(s_conv, 0, 1), s_lru, jnp.swapaxes(s_pool, 0, 1))
    return yp, ys, states


def kernel(x_prompt, x_sample, mem_prompt, state_conv, state_lru, state_pool, cache_mem_k, cache_mem_v, ffn1_norm, ffn1_w_gate, ffn1_w_up, ffn1_w_down, mix_norm, w_in, conv_w, conv_b, lru_wa, lru_ba, lru_wx, lru_bx, lru_lambda, pool_w, pool_scale, w_out, xattn_norm, mem_norm, xattn_wq, xattn_wk, xattn_wv, xattn_wo, ffn2_norm, ffn2_w_gate, ffn2_w_up, ffn2_w_down, final_norm):
    depth = ffn1_norm.shape[0]
    assert depth == 1, "the final RMSNorm is fused into the last layer's second FFN"
    names = ("ffn1_norm", "ffn1_w_gate", "ffn1_w_up", "ffn1_w_down", "mix_norm", "w_in", "conv_w", "conv_b",
             "lru_wa", "lru_ba", "lru_wx", "lru_bx", "lru_lambda", "pool_w", "pool_scale", "w_out",
             "xattn_norm", "mem_norm", "xattn_wq", "xattn_wk", "xattn_wv", "xattn_wo",
             "ffn2_norm", "ffn2_w_gate", "ffn2_w_up", "ffn2_w_down")
    stacked = (ffn1_norm, ffn1_w_gate, ffn1_w_up, ffn1_w_down, mix_norm, w_in, conv_w, conv_b,
               lru_wa, lru_ba, lru_wx, lru_bx, lru_lambda, pool_w, pool_scale, w_out,
               xattn_norm, mem_norm, xattn_wq, xattn_wk, xattn_wv, xattn_wo,
               ffn2_norm, ffn2_w_gate, ffn2_w_up, ffn2_w_down)
    p = {n: a[0] for n, a in zip(names, stacked)}
    yp, ys, st = _layer(x_prompt, x_sample, mem_prompt, state_conv[0], state_lru[0], state_pool[0],
                        cache_mem_k[0], cache_mem_v[0], p, final_norm)
    p_conv, p_lru, p_pool, p_mk, p_mv, s_conv, s_lru, s_pool = (s[None] for s in st)
    return (yp, ys, p_conv, p_lru, p_pool, p_mk, p_mv, s_conv, s_lru, s_pool)
```

```python
import functools

import jax
import jax.numpy as jnp
from jax import lax
from jax.experimental import pallas as pl
from jax.experimental.pallas import tpu as pltpu

F32 = jnp.float32
BF16 = jnp.bfloat16

D_MODEL = 1024
D_FF = 2816
LRU_WIDTH = 512
LRU_HEADS = 8
LRU_C = 8.0
CONV_W = 4
POOL_WIDTH = 512
POOL_WINDOWS = (2, 4, 8, 16)
POOL_GROUP_DIM = POOL_WIDTH // len(POOL_WINDOWS)
POOL_HIST = max(POOL_WINDOWS) - 1
CONV_HIST = CONV_W - 1
MEM_LEN = 256
XA_HEADS = 4
XA_HEAD_DIM = D_MODEL // XA_HEADS
EPS = 1e-6
PAST_LEN = 16384

MXU_TILE = 256
LANES = 128
assert XA_HEAD_DIM == 2 * LANES
FF_CHUNK = MXU_TILE
N_FF_CHUNKS = D_FF // FF_CHUNK
ROW_TILE = 512
VMEM_LIMIT_BYTES = 56 * 1024 * 1024


def _rmsnorm(x, g):
    var = jnp.mean(x * x, axis=-1, keepdims=True)
    return x * lax.rsqrt(var + EPS) * g


def _dot(a, b):
    return jnp.dot(a, b, preferred_element_type=F32)


def _const_spec(shape):
    zeros = (0,) * len(shape)
    return pl.BlockSpec(shape, lambda *_: zeros, pipeline_mode=pl.Buffered(1))


def _params(semantics):
    return pltpu.CompilerParams(dimension_semantics=semantics, vmem_limit_bytes=VMEM_LIMIT_BYTES)


def _ffn_kernel(x_ref, g_ref, wg_ref, wu_ref, wd_ref, fn_ref, o_ref, h_scr, a_scr, acc_scr, *, final_norm):
    x = x_ref[...]
    h_scr[...] = _rmsnorm(x, g_ref[...]).astype(BF16)

    def gate_up(c):
        h = h_scr[...]
        g = _dot(h, wg_ref[c])
        u = _dot(h, wu_ref[c])
        a_scr[c % 2] = (g * jax.nn.sigmoid(g) * u).astype(BF16)

    def down(c):
        d = _dot(a_scr[c % 2], wd_ref[c])
        if c == 0:
            acc_scr[...] = d
        else:
            acc_scr[...] += d

    gate_up(0)
    for c in range(1, N_FF_CHUNKS):
        gate_up(c)
        down(c - 1)
    down(N_FF_CHUNKS - 1)
    y = x + 0.5 * acc_scr[...]
    if final_norm:
        y = _rmsnorm(y, fn_ref[...])
    o_ref[...] = y


def _ffn(x, g, wg, wu, wd, fn, *, final_norm):
    rows = x.shape[0]
    row_spec = pl.BlockSpec((ROW_TILE, D_MODEL), lambda i: (i, 0))
    return pl.pallas_call(
        functools.partial(_ffn_kernel, final_norm=final_norm),
        grid=(rows // ROW_TILE,),
        in_specs=[row_spec, _const_spec((1, D_MODEL)),
                  _const_spec((N_FF_CHUNKS, D_MODEL, FF_CHUNK)),
                  _const_spec((N_FF_CHUNKS, D_MODEL, FF_CHUNK)),
                  _const_spec((N_FF_CHUNKS, FF_CHUNK, D_MODEL)),
                  _const_spec((1, D_MODEL))],
        out_specs=row_spec,
        out_shape=jax.ShapeDtypeStruct((rows, D_MODEL), F32),
        scratch_shapes=[pltpu.VMEM((ROW_TILE, D_MODEL), BF16), pltpu.VMEM((2, ROW_TILE, FF_CHUNK), BF16),
                        pltpu.VMEM((ROW_TILE, D_MODEL), F32)],
        compiler_params=_params(("parallel",)),
        name="ffn_final" if final_norm else "ffn",
    )(x, g, wg, wu, wd, fn)


def _block_diag_dot(u, w_ref):
    return jnp.concatenate([_dot(u[:, :MXU_TILE], w_ref[0]), _dot(u[:, MXU_TILE:], w_ref[1])], axis=1)


def _mix_kernel(x_ref, gm_ref, win_ref, cw_ref, cb_ref, wa_ref, ba_ref, wx_ref, bx_ref, lam_ref,
                pw_ref, ps_ref, wout_ref, sconv_ref, slru_ref, spool_ref,
                o_ref, oconv_ref, olru_ref, opool_ref,
                lru_ext, pool_ext, h_scr, a_scr, b_scr, *, tt, bb, pos0):
    rows = tt * bb
    i = pl.program_id(1)

    @pl.when(i == 0)
    def _():
        lru_ext[0:CONV_HIST * bb, :] = sconv_ref[...].reshape(CONV_HIST * bb, LRU_WIDTH)
        pool_ext[0:POOL_HIST * bb, :] = spool_ref[...].reshape(POOL_HIST * bb, POOL_WIDTH)
        h_scr[...] = slru_ref[...]

    x = jnp.swapaxes(x_ref[...], 0, 1).reshape(rows, D_MODEL)
    h = _rmsnorm(x, gm_ref[...]).astype(BF16)
    proj = _dot(h, win_ref[...])
    lru_ext[CONV_HIST * bb:, :] = proj[:, :LRU_WIDTH]
    gate = proj[:, LRU_WIDTH:2 * LRU_WIDTH]
    u_pool = proj[:, 2 * LRU_WIDTH:]
    pool_ext[POOL_HIST * bb:, :] = u_pool

    cw = cw_ref[...]
    u = cb_ref[...] + sum(lru_ext[k * bb:k * bb + rows, :] * cw[k:k + 1, :] for k in range(CONV_W))
    new_conv = lru_ext[tt * bb:(tt + CONV_HIST) * bb, :]
    oconv_ref[...] = new_conv.reshape(CONV_HIST, bb, LRU_WIDTH)
    lru_ext[0:CONV_HIST * bb, :] = new_conv

    ub = u.astype(BF16)
    r = jax.nn.sigmoid(_block_diag_dot(ub, wa_ref) + ba_ref[...])
    ig = jax.nn.sigmoid(_block_diag_dot(ub, wx_ref) + bx_ref[...])
    neg_lam = -lam_ref[...]
    softplus = jnp.maximum(neg_lam, 0.0) + jnp.log1p(jnp.exp(-jnp.abs(neg_lam)))
    log_a = (-LRU_C * softplus) * r
    a = jnp.exp(log_a)
    a_scr[...] = a
    b_scr[...] = jnp.sqrt(jnp.maximum(1.0 - a * a, 0.0)) * ig * u

    hcur = h_scr[...]
    for t in range(tt):
        sl = slice(t * bb, (t + 1) * bb)
        hcur = a_scr[sl, :] * hcur + b_scr[sl, :]
        b_scr[sl, :] = hcur
    h_scr[...] = hcur
    olru_ref[...] = hcur
    y_lru = (jax.nn.gelu(gate) * b_scr[...]).astype(BF16)

    t_idx = lax.broadcasted_iota(jnp.int32, (rows, POOL_GROUP_DIM), 0) // bb
    pos = pos0 + i * tt + t_idx
    deltas = []
    for g, win in enumerate(POOL_WINDOWS):
        lanes = slice(g * POOL_GROUP_DIM, (g + 1) * POOL_GROUP_DIM)
        s = pool_ext[:, lanes]
        span = 1
        while span < win:
            s = s[span * bb:, :] + s[:-span * bb, :]
            span *= 2
        s = s[s.shape[0] - rows:, :]
        cnt = jnp.minimum(pos + 1, win).astype(F32)
        deltas.append(s / cnt - u_pool[:, lanes])
    delta = jnp.concatenate(deltas, axis=1).astype(BF16)
    y_pool = (ps_ref[...] * _block_diag_dot(delta, pw_ref)).astype(BF16)
    new_pool = pool_ext[tt * bb:(tt + POOL_HIST) * bb, :]
    opool_ref[...] = new_pool.reshape(POOL_HIST, bb, POOL_WIDTH)
    pool_ext[0:POOL_HIST * bb, :] = new_pool

    out = x + _dot(y_lru, wout_ref[0:LRU_WIDTH, :]) + _dot(y_pool, wout_ref[LRU_WIDTH:, :])
    o_ref[...] = jnp.swapaxes(out.reshape(tt, bb, D_MODEL), 0, 1)


def _mix(x, w, sconv, slru, spool, *, tt, bb, pos0):
    n_b, n_t = x.shape[0], x.shape[1]
    rows = tt * bb
    x_spec = pl.BlockSpec((bb, tt, D_MODEL), lambda j, i: (j, i, 0))
    conv_spec = pl.BlockSpec((CONV_HIST, bb, LRU_WIDTH), lambda j, i: (0, j, 0))
    lru_spec = pl.BlockSpec((bb, LRU_WIDTH), lambda j, i: (j, 0))
    pool_spec = pl.BlockSpec((POOL_HIST, bb, POOL_WIDTH), lambda j, i: (0, j, 0))
    return pl.pallas_call(
        functools.partial(_mix_kernel, tt=tt, bb=bb, pos0=pos0),
        grid=(n_b // bb, n_t // tt),
        in_specs=[x_spec, _const_spec((1, D_MODEL)), _const_spec((D_MODEL, 3 * LRU_WIDTH)),
                  _const_spec((CONV_W, LRU_WIDTH)), _const_spec((1, LRU_WIDTH)),
                  _const_spec((2, MXU_TILE, MXU_TILE)), _const_spec((1, LRU_WIDTH)),
                  _const_spec((2, MXU_TILE, MXU_TILE)), _const_spec((1, LRU_WIDTH)),
                  _const_spec((1, LRU_WIDTH)),
                  _const_spec((2, MXU_TILE, MXU_TILE)), _const_spec((1, POOL_WIDTH)),
                  _const_spec((D_MODEL, D_MODEL)),
                  conv_spec, lru_spec, pool_spec],
        out_specs=[x_spec, conv_spec, lru_spec, pool_spec],
        out_shape=[jax.ShapeDtypeStruct(x.shape, F32),
                   jax.ShapeDtypeStruct((CONV_HIST, n_b, LRU_WIDTH), F32),
                   jax.ShapeDtypeStruct((n_b, LRU_WIDTH), F32),
                   jax.ShapeDtypeStruct((POOL_HIST, n_b, POOL_WIDTH), F32)],
        scratch_shapes=[pltpu.VMEM(((CONV_HIST + tt) * bb, LRU_WIDTH), F32),
                        pltpu.VMEM(((POOL_HIST + tt) * bb, POOL_WIDTH), F32),
                        pltpu.VMEM((bb, LRU_WIDTH), F32),
                        pltpu.VMEM((rows, LRU_WIDTH), F32),
                        pltpu.VMEM((rows, LRU_WIDTH), F32)],
        compiler_params=_params(("parallel", "arbitrary")),
        name="mix",
    )(x, w["mix_norm"], w["w_in"], w["conv_w"], w["conv_b"], w["lru_wa"], w["lru_ba"],
      w["lru_wx"], w["lru_bx"], w["lru_lambda"], w["pool_w"], w["pool_scale"], w["w_out"],
      sconv, slru, spool)


def _memkv_kernel(m_ref, g_ref, wk_ref, wv_ref, k_ref, v_ref, kb_ref, vb_ref):
    m = _rmsnorm(m_ref[0], g_ref[...]).astype(BF16)
    for w_ref, o_ref, ob_ref in ((wk_ref, k_ref, kb_ref), (wv_ref, v_ref, vb_ref)):
        kv = _dot(m, w_ref[...])
        ob_ref[0] = kv.astype(BF16)
        for hd in range(XA_HEADS):
            o_ref[0, :, hd, :] = kv[:, hd * XA_HEAD_DIM:(hd + 1) * XA_HEAD_DIM]


def _memkv(mem, g, wk, wv):
    n_b = mem.shape[0]
    row_spec = pl.BlockSpec((1, MEM_LEN, D_MODEL), lambda i: (i, 0, 0))
    head_spec = pl.BlockSpec((1, MEM_LEN, XA_HEADS, XA_HEAD_DIM), lambda i: (i, 0, 0, 0))
    return pl.pallas_call(
        _memkv_kernel,
        grid=(n_b,),
        in_specs=[row_spec, _const_spec((1, D_MODEL)), _const_spec((D_MODEL, D_MODEL)),
                  _const_spec((D_MODEL, D_MODEL))],
        out_specs=[head_spec, head_spec, row_spec, row_spec],
        out_shape=[jax.ShapeDtypeStruct((n_b, MEM_LEN, XA_HEADS, XA_HEAD_DIM), F32)] * 2
        + [jax.ShapeDtypeStruct((n_b, MEM_LEN, D_MODEL), BF16)] * 2,
        compiler_params=_params(("parallel",)),
        name="memkv",
    )(mem, g, wk, wv)


def _attend(q, head_k, head_v):
    outs = []
    for hd in range(XA_HEADS):
        cols = slice(hd * XA_HEAD_DIM, (hd + 1) * XA_HEAD_DIM)
        s = lax.dot_general(q[:, cols].astype(BF16), head_k(hd), (((1,), (1,)), ((), ())),
                            preferred_element_type=F32) * (XA_HEAD_DIM ** -0.5)
        e = jnp.exp(s - jnp.max(s, axis=-1, keepdims=True))
        p = (e / jnp.sum(e, axis=-1, keepdims=True)).astype(BF16)
        outs.append(_dot(p, head_v(hd)).astype(BF16))
    return jnp.concatenate(outs, axis=1)


def _xattn_prompt_kernel(x_ref, g_ref, wq_ref, k_ref, v_ref, wo_ref, o_ref):
    x = x_ref[...]
    q = _dot(_rmsnorm(x, g_ref[...]).astype(BF16), wq_ref[...])
    o = _attend(q, lambda hd: k_ref[:, hd * XA_HEAD_DIM:(hd + 1) * XA_HEAD_DIM],
                lambda hd: v_ref[:, hd * XA_HEAD_DIM:(hd + 1) * XA_HEAD_DIM])
    o_ref[...] = x + _dot(o, wo_ref[...])


def _xattn_prompt(x, g, wq, k, v, wo):
    n_b, n_t = x.shape[0], x.shape[1]
    x_spec = pl.BlockSpec((None, ROW_TILE, D_MODEL), lambda b, i: (b, i, 0))
    kv_spec = pl.BlockSpec((None, MEM_LEN, D_MODEL), lambda b, i: (b, 0, 0))
    return pl.pallas_call(
        _xattn_prompt_kernel,
        grid=(n_b, n_t // ROW_TILE),
        in_specs=[x_spec, _const_spec((1, D_MODEL)), _const_spec((D_MODEL, D_MODEL)),
                  kv_spec, kv_spec, _const_spec((D_MODEL, D_MODEL))],
        out_specs=x_spec,
        out_shape=jax.ShapeDtypeStruct(x.shape, F32),
        compiler_params=_params(("parallel", "parallel")),
        name="xattn_prompt",
    )(x, g, wq, k, v, wo)


def _xattn_sample_kernel(x_ref, g_ref, wq_ref, k0_ref, k1_ref, v0_ref, v1_ref, wo_ref, o_ref, att_scr, s_scr,
                         *, bc, tq):
    x = x_ref[...]
    q = _dot(_rmsnorm(x, g_ref[...]).astype(BF16), wq_ref[...])
    halves = [[r.reshape(bc * MEM_LEN * XA_HEADS, LANES) for r in pair]
              for pair in ((k0_ref, k1_ref), (v0_ref, v1_ref))]

    def head(pair, b, hd):
        rows = pl.ds(b * MEM_LEN * XA_HEADS + hd, MEM_LEN, stride=XA_HEADS)
        return jnp.concatenate([r[rows, :] for r in pair], axis=1).astype(BF16)

    pairs = [(b, hd) for b in range(bc) for hd in range(XA_HEADS)]
    for n, (b, hd) in enumerate(pairs):
        qh = q[b * tq:(b + 1) * tq, hd * XA_HEAD_DIM:(hd + 1) * XA_HEAD_DIM].astype(BF16)
        s_scr[n * tq:(n + 1) * tq, :] = lax.dot_general(
            qh, head(halves[0], b, hd), (((1,), (1,)), ((), ())),
            preferred_element_type=F32) * (XA_HEAD_DIM ** -0.5)
    s = s_scr[...]
    e = jnp.exp(s - jnp.max(s, axis=-1, keepdims=True))
    s_scr[...] = e / jnp.sum(e, axis=-1, keepdims=True)
    for n, (b, hd) in enumerate(pairs):
        p = s_scr[n * tq:(n + 1) * tq, :].astype(BF16)
        att_scr[b * tq:(b + 1) * tq, hd * XA_HEAD_DIM:(hd + 1) * XA_HEAD_DIM] = _dot(p, head(halves[1], b, hd))
    o_ref[...] = x + _dot(att_scr[...].astype(BF16), wo_ref[...])


def _xattn_sample(x, g, wq, k, v, wo, *, bc):
    n_b = k.shape[0]
    tq = x.shape[0] // n_b
    x_spec = pl.BlockSpec((bc * tq, D_MODEL), lambda j: (j, 0))
    kv_specs = [pl.BlockSpec((bc, MEM_LEN, XA_HEADS, LANES), functools.partial(lambda half, j: (j, 0, 0, half), half))
                for half in range(XA_HEAD_DIM // LANES)]
    return pl.pallas_call(
        functools.partial(_xattn_sample_kernel, bc=bc, tq=tq),
        grid=(n_b // bc,),
        in_specs=[x_spec, _const_spec((1, D_MODEL)), _const_spec((D_MODEL, D_MODEL)),
                  *kv_specs, *kv_specs, _const_spec((D_MODEL, D_MODEL))],
        out_specs=x_spec,
        out_shape=jax.ShapeDtypeStruct(x.shape, F32),
        scratch_shapes=[pltpu.VMEM((bc * tq, D_MODEL), F32),
                        pltpu.VMEM((bc * XA_HEADS * tq, MEM_LEN), F32)],
        compiler_params=_params(("parallel",)),
        name="xattn_sample",
    )(x, g, wq, k, k, v, v, wo)


def _ff_cols(w):
    return w.reshape(D_MODEL, N_FF_CHUNKS, FF_CHUNK).transpose(1, 0, 2).astype(BF16)


def _block_diag_tiles(w):
    groups, dg, _ = w.shape
    per_tile = MXU_TILE // dg
    w = w.reshape(groups // per_tile, per_tile, dg, dg)
    eye = jnp.eye(per_tile, dtype=w.dtype)
    tiles = jnp.einsum("npij,pq->npiqj", w, eye)
    return tiles.reshape(groups // per_tile, MXU_TILE, MXU_TILE).astype(BF16)


def _row(v):
    return v.reshape(1, -1).astype(F32)


def _layer(x_prompt, x_sample, mem_prompt, sconv, slru, spool, cache_k, cache_v, p, final_norm):
    n_b, n_t, _ = x_prompt.shape
    s_b, s_t, _ = x_sample.shape
    ffn1 = (_row(p["ffn1_norm"]), _ff_cols(p["ffn1_w_gate"]), _ff_cols(p["ffn1_w_up"]),
            p["ffn1_w_down"].reshape(N_FF_CHUNKS, FF_CHUNK, D_MODEL).astype(BF16), _row(final_norm))
    ffn2 = (_row(p["ffn2_norm"]), _ff_cols(p["ffn2_w_gate"]), _ff_cols(p["ffn2_w_up"]),
            p["ffn2_w_down"].reshape(N_FF_CHUNKS, FF_CHUNK, D_MODEL).astype(BF16), _row(final_norm))
    mixw = {
        "mix_norm": _row(p["mix_norm"]), "w_in": p["w_in"].astype(BF16),
        "conv_w": p["conv_w"].astype(F32), "conv_b": _row(p["conv_b"]),
        "lru_wa": _block_diag_tiles(p["lru_wa"]), "lru_ba": _row(p["lru_ba"]),
        "lru_wx": _block_diag_tiles(p["lru_wx"]), "lru_bx": _row(p["lru_bx"]),
        "lru_lambda": _row(p["lru_lambda"]),
        "pool_w": _block_diag_tiles(p["pool_w"]), "pool_scale": _row(p["pool_scale"]),
        "w_out": p["w_out"].astype(BF16),
    }
    gx, wq, wo = _row(p["xattn_norm"]), p["xattn_wq"].astype(BF16), p["xattn_wo"].astype(BF16)

    mk, mv, mk_bf, mv_bf = _memkv(mem_prompt, _row(p["mem_norm"]),
                                  p["xattn_wk"].astype(BF16), p["xattn_wv"].astype(BF16))

    xp = _ffn(x_prompt.reshape(n_b * n_t, D_MODEL), *ffn1, final_norm=False)
    xp, p_conv, p_lru, p_pool = _mix(
        xp.reshape(n_b, n_t, D_MODEL), mixw, jnp.zeros((CONV_HIST, n_b, LRU_WIDTH), F32),
        jnp.zeros((n_b, LRU_WIDTH), F32), jnp.zeros((POOL_HIST, n_b, POOL_WIDTH), F32),
        tt=ROW_TILE // n_b, bb=n_b, pos0=0)
    xp = _xattn_prompt(xp, gx, wq, mk_bf, mv_bf, wo)
    yp = _ffn(xp.reshape(n_b * n_t, D_MODEL), *ffn2, final_norm=True).reshape(n_b, n_t, D_MODEL)

    xs = _ffn(x_sample.reshape(s_b * s_t, D_MODEL), *ffn1, final_norm=False)
    xs, s_conv, s_lru, s_pool = _mix(
        xs.reshape(s_b, s_t, D_MODEL), mixw, jnp.swapaxes(sconv, 0, 1), slru, jnp.swapaxes(spool, 0, 1),
        tt=s_t, bb=ROW_TILE // s_t, pos0=PAST_LEN)
    xs = _xattn_sample(xs.reshape(s_b * s_t, D_MODEL), gx, wq, cache_k, cache_v, wo, bc=8)
    ys = _ffn(xs, *ffn2, final_norm=True).reshape(s_b, s_t, D_MODEL)

    states = (jnp.swapaxes(p_conv, 0, 1), p_lru, jnp.swapaxes(p_pool, 0, 1), mk, mv,
              jnp.swapaxes(s_conv, 0, 1), s_lru, jnp.swapaxes(s_pool, 0, 1))
    return yp, ys, states


def kernel(x_prompt, x_sample, mem_prompt, state_conv, state_lru, state_pool, cache_mem_k, cache_mem_v, ffn1_norm, ffn1_w_gate, ffn1_w_up, ffn1_w_down, mix_norm, w_in, conv_w, conv_b, lru_wa, lru_ba, lru_wx, lru_bx, lru_lambda, pool_w, pool_scale, w_out, xattn_norm, mem_norm, xattn_wq, xattn_wk, xattn_wv, xattn_wo, ffn2_norm, ffn2_w_gate, ffn2_w_up, ffn2_w_down, final_norm):
    depth = ffn1_norm.shape[0]
    assert depth == 1, "the final RMSNorm is fused into the last layer's second FFN"
    names = ("ffn1_norm", "ffn1_w_gate", "ffn1_w_up", "ffn1_w_down", "mix_norm", "w_in", "conv_w", "conv_b",
             "lru_wa", "lru_ba", "lru_wx", "lru_bx", "lru_lambda", "pool_w", "pool_scale", "w_out",
             "xattn_norm", "mem_norm", "xattn_wq", "xattn_wk", "xattn_wv", "xattn_wo",
             "ffn2_norm", "ffn2_w_gate", "ffn2_w_up", "ffn2_w_down")
    stacked = (ffn1_norm, ffn1_w_gate, ffn1_w_up, ffn1_w_down, mix_norm, w_in, conv_w, conv_b,
               lru_wa, lru_ba, lru_wx, lru_bx, lru_lambda, pool_w, pool_scale, w_out,
               xattn_norm, mem_norm, xattn_wq, xattn_wk, xattn_wv, xattn_wo,
               ffn2_norm, ffn2_w_gate, ffn2_w_up, ffn2_w_down)
    p = {n: a[0] for n, a in zip(names, stacked)}
    yp, ys, st = _layer(x_prompt, x_sample, mem_prompt, state_conv[0], state_lru[0], state_pool[0],
                        cache_mem_k[0], cache_mem_v[0], p, final_norm)
    p_conv, p_lru, p_pool, p_mk, p_mv, s_conv, s_lru, s_pool = (s[None] for s in st)
    return (yp, ys, p_conv, p_lru, p_pool, p_mk, p_mv, s_conv, s_lru, s_pool)
```

```python
import functools

import jax
import jax.numpy as jnp
from jax import lax
from jax.experimental import pallas as pl
from jax.experimental.pallas import tpu as pltpu

F32 = jnp.float32
BF16 = jnp.bfloat16

D_MODEL = 1024
D_FF = 2816
LRU_WIDTH = 512
LRU_HEADS = 8
LRU_C = 8.0
CONV_W = 4
POOL_WIDTH = 512
POOL_WINDOWS = (2, 4, 8, 16)
POOL_GROUP_DIM = POOL_WIDTH // len(POOL_WINDOWS)
POOL_HIST = max(POOL_WINDOWS) - 1
CONV_HIST = CONV_W - 1
MEM_LEN = 256
XA_HEADS = 4
XA_HEAD_DIM = D_MODEL // XA_HEADS
EPS = 1e-6
PAST_LEN = 16384

MXU_TILE = 256
LANES = 128
assert XA_HEAD_DIM == 2 * LANES
FF_CHUNK = MXU_TILE
N_FF_CHUNKS = D_FF // FF_CHUNK
ROW_TILE = 512
VMEM_LIMIT_BYTES = 56 * 1024 * 1024


def _rmsnorm(x, g):
    var = jnp.mean(x * x, axis=-1, keepdims=True)
    return x * lax.rsqrt(var + EPS) * g


def _dot(a, b):
    return jnp.dot(a, b, preferred_element_type=F32)


def _const_spec(shape):
    zeros = (0,) * len(shape)
    return pl.BlockSpec(shape, lambda *_: zeros, pipeline_mode=pl.Buffered(1))


def _params(semantics):
    return pltpu.CompilerParams(dimension_semantics=semantics, vmem_limit_bytes=VMEM_LIMIT_BYTES)


def _ffn_kernel(x_ref, g_ref, wg_hbm, wu_hbm, wd_hbm, fn_ref, o_ref,
                wg_scr, wu_scr, wd_scr, col_stage, row_stage, sems, h_scr, a_scr, acc_scr, *, final_norm):
    i = pl.program_id(0)

    @pl.when(i == 0)
    def _():
        items = []
        for c in range(N_FF_CHUNKS):
            cols = pl.ds(c * FF_CHUNK, FF_CHUNK)
            items.append((wg_hbm.at[:, cols], col_stage, 0, wg_scr.at[c]))
            items.append((wu_hbm.at[:, cols], col_stage, 0, wu_scr.at[c]))
            items.append((wd_hbm.at[cols, :], row_stage, 1, wd_scr.at[c]))
        uses = [0, 0]
        copies = []
        for src, stage, kind, dst in items:
            slot = uses[kind] % 2
            uses[kind] += 1
            copies.append((pltpu.make_async_copy(src, stage.at[slot], sems.at[kind, slot]), stage.at[slot], dst))
        copies[0][0].start()
        for n, (copy, staged, dst) in enumerate(copies):
            if n + 1 < len(copies):
                copies[n + 1][0].start()
            copy.wait()
            dst[...] = staged[...].astype(BF16)

    x = x_ref[...]
    h_scr[...] = _rmsnorm(x, g_ref[...]).astype(BF16)

    def gate_up(c):
        h = h_scr[...]
        g = _dot(h, wg_scr[c])
        u = _dot(h, wu_scr[c])
        a_scr[c % 2] = (g * jax.nn.sigmoid(g) * u).astype(BF16)

    def down(c):
        d = _dot(a_scr[c % 2], wd_scr[c])
        if c == 0:
            acc_scr[...] = d
        else:
            acc_scr[...] += d

    gate_up(0)
    for c in range(1, N_FF_CHUNKS):
        gate_up(c)
        down(c - 1)
    down(N_FF_CHUNKS - 1)
    y = x + 0.5 * acc_scr[...]
    if final_norm:
        y = _rmsnorm(y, fn_ref[...])
    o_ref[...] = y


def _ffn(x, g, wg, wu, wd, fn, *, final_norm):
    row_spec = pl.BlockSpec((ROW_TILE, D_MODEL), lambda i: (i, 0))
    hbm = pl.BlockSpec(memory_space=pl.ANY)
    return pl.pallas_call(
        functools.partial(_ffn_kernel, final_norm=final_norm),
        grid=(x.shape[0] // ROW_TILE,),
        in_specs=[row_spec, _const_spec((1, D_MODEL)), hbm, hbm, hbm, _const_spec((1, D_MODEL))],
        out_specs=row_spec,
        out_shape=jax.ShapeDtypeStruct(x.shape, F32),
        scratch_shapes=[pltpu.VMEM((N_FF_CHUNKS, D_MODEL, FF_CHUNK), BF16),
                        pltpu.VMEM((N_FF_CHUNKS, D_MODEL, FF_CHUNK), BF16),
                        pltpu.VMEM((N_FF_CHUNKS, FF_CHUNK, D_MODEL), BF16),
                        pltpu.VMEM((2, D_MODEL, FF_CHUNK), F32),
                        pltpu.VMEM((2, FF_CHUNK, D_MODEL), F32),
                        pltpu.SemaphoreType.DMA((2, 2)),
                        pltpu.VMEM((ROW_TILE, D_MODEL), BF16), pltpu.VMEM((2, ROW_TILE, FF_CHUNK), BF16),
                        pltpu.VMEM((ROW_TILE, D_MODEL), F32)],
        compiler_params=_params(("arbitrary",)),
        name="ffn_final" if final_norm else "ffn",
    )(x, g, wg, wu, wd, fn)


def _block_diag_dot(u, w_ref):
    return jnp.concatenate([_dot(u[:, :MXU_TILE], w_ref[0]), _dot(u[:, MXU_TILE:], w_ref[1])], axis=1)


def _mix_kernel(x_ref, gm_ref, win_ref, cw_ref, cb_ref, wa_ref, ba_ref, wx_ref, bx_ref, lam_ref,
                pw_ref, ps_ref, wout_ref, sconv_ref, slru_ref, spool_ref,
                o_ref, oconv_ref, olru_ref, opool_ref,
                lru_ext, pool_ext, h_scr, a_scr, b_scr, *, tt, bb, pos0):
    rows = tt * bb
    i = pl.program_id(1)

    @pl.when(i == 0)
    def _():
        lru_ext[0:CONV_HIST * bb, :] = sconv_ref[...].reshape(CONV_HIST * bb, LRU_WIDTH)
        pool_ext[0:POOL_HIST * bb, :] = spool_ref[...].reshape(POOL_HIST * bb, POOL_WIDTH)
        h_scr[...] = slru_ref[...]

    x = jnp.swapaxes(x_ref[...], 0, 1).reshape(rows, D_MODEL)
    h = _rmsnorm(x, gm_ref[...]).astype(BF16)
    proj = _dot(h, win_ref[...])
    lru_ext[CONV_HIST * bb:, :] = proj[:, :LRU_WIDTH]
    gate = proj[:, LRU_WIDTH:2 * LRU_WIDTH]
    u_pool = proj[:, 2 * LRU_WIDTH:]
    pool_ext[POOL_HIST * bb:, :] = u_pool

    cw = cw_ref[...]
    u = cb_ref[...] + sum(lru_ext[k * bb:k * bb + rows, :] * cw[k:k + 1, :] for k in range(CONV_W))
    new_conv = lru_ext[tt * bb:(tt + CONV_HIST) * bb, :]
    oconv_ref[...] = new_conv.reshape(CONV_HIST, bb, LRU_WIDTH)
    lru_ext[0:CONV_HIST * bb, :] = new_conv

    ub = u.astype(BF16)
    r = jax.nn.sigmoid(_block_diag_dot(ub, wa_ref) + ba_ref[...])
    ig = jax.nn.sigmoid(_block_diag_dot(ub, wx_ref) + bx_ref[...])
    neg_lam = -lam_ref[...]
    softplus = jnp.maximum(neg_lam, 0.0) + jnp.log1p(jnp.exp(-jnp.abs(neg_lam)))
    log_a = (-LRU_C * softplus) * r
    a = jnp.exp(log_a)
    a_scr[...] = a
    b_scr[...] = jnp.sqrt(jnp.maximum(1.0 - a * a, 0.0)) * ig * u

    hcur = h_scr[...]
    for t in range(tt):
        sl = slice(t * bb, (t + 1) * bb)
        hcur = a_scr[sl, :] * hcur + b_scr[sl, :]
        b_scr[sl, :] = hcur
    h_scr[...] = hcur
    olru_ref[...] = hcur
    y_lru = (jax.nn.gelu(gate) * b_scr[...]).astype(BF16)

    t_idx = lax.broadcasted_iota(jnp.int32, (rows, POOL_GROUP_DIM), 0) // bb
    pos = pos0 + i * tt + t_idx
    deltas = []
    for g, win in enumerate(POOL_WINDOWS):
        lanes = slice(g * POOL_GROUP_DIM, (g + 1) * POOL_GROUP_DIM)
        s = pool_ext[:, lanes]
        span = 1
        while span < win:
            s = s[span * bb:, :] + s[:-span * bb, :]
            span *= 2
        s = s[s.shape[0] - rows:, :]
        cnt = jnp.minimum(pos + 1, win).astype(F32)
        deltas.append(s / cnt - u_pool[:, lanes])
    delta = jnp.concatenate(deltas, axis=1).astype(BF16)
    y_pool = (ps_ref[...] * _block_diag_dot(delta, pw_ref)).astype(BF16)
    new_pool = pool_ext[tt * bb:(tt + POOL_HIST) * bb, :]
    opool_ref[...] = new_pool.reshape(POOL_HIST, bb, POOL_WIDTH)
    pool_ext[0:POOL_HIST * bb, :] = new_pool

    out = x + _dot(y_lru, wout_ref[0:LRU_WIDTH, :]) + _dot(y_pool, wout_ref[LRU_WIDTH:, :])
    o_ref[...] = jnp.swapaxes(out.reshape(tt, bb, D_MODEL), 0, 1)


def _mix(x, w, sconv, slru, spool, *, tt, bb, pos0):
    n_b, n_t = x.shape[0], x.shape[1]
    rows = tt * bb
    x_spec = pl.BlockSpec((bb, tt, D_MODEL), lambda j, i: (j, i, 0))
    conv_spec = pl.BlockSpec((CONV_HIST, bb, LRU_WIDTH), lambda j, i: (0, j, 0))
    lru_spec = pl.BlockSpec((bb, LRU_WIDTH), lambda j, i: (j, 0))
    pool_spec = pl.BlockSpec((POOL_HIST, bb, POOL_WIDTH), lambda j, i: (0, j, 0))
    return pl.pallas_call(
        functools.partial(_mix_kernel, tt=tt, bb=bb, pos0=pos0),
        grid=(n_b // bb, n_t // tt),
        in_specs=[x_spec, _const_spec((1, D_MODEL)), _const_spec((D_MODEL, 3 * LRU_WIDTH)),
                  _const_spec((CONV_W, LRU_WIDTH)), _const_spec((1, LRU_WIDTH)),
                  _const_spec((2, MXU_TILE, MXU_TILE)), _const_spec((1, LRU_WIDTH)),
                  _const_spec((2, MXU_TILE, MXU_TILE)), _const_spec((1, LRU_WIDTH)),
                  _const_spec((1, LRU_WIDTH)),
                  _const_spec((2, MXU_TILE, MXU_TILE)), _const_spec((1, POOL_WIDTH)),
                  _const_spec((D_MODEL, D_MODEL)),
                  conv_spec, lru_spec, pool_spec],
        out_specs=[x_spec, conv_spec, lru_spec, pool_spec],
        out_shape=[jax.ShapeDtypeStruct(x.shape, F32),
                   jax.ShapeDtypeStruct((CONV_HIST, n_b, LRU_WIDTH), F32),
                   jax.ShapeDtypeStruct((n_b, LRU_WIDTH), F32),
                   jax.ShapeDtypeStruct((POOL_HIST, n_b, POOL_WIDTH), F32)],
        scratch_shapes=[pltpu.VMEM(((CONV_HIST + tt) * bb, LRU_WIDTH), F32),
                        pltpu.VMEM(((POOL_HIST + tt) * bb, POOL_WIDTH), F32),
                        pltpu.VMEM((bb, LRU_WIDTH), F32),
                        pltpu.VMEM((rows, LRU_WIDTH), F32),
                        pltpu.VMEM((rows, LRU_WIDTH), F32)],
        compiler_params=_params(("parallel", "arbitrary")),
        name="mix",
    )(x, w["mix_norm"], w["w_in"], w["conv_w"], w["conv_b"], w["lru_wa"], w["lru_ba"],
      w["lru_wx"], w["lru_bx"], w["lru_lambda"], w["pool_w"], w["pool_scale"], w["w_out"],
      sconv, slru, spool)


def _memkv_kernel(m_ref, g_ref, wk_ref, wv_ref, k_ref, v_ref, kb_ref, vb_ref):
    m = _rmsnorm(m_ref[0], g_ref[...]).astype(BF16)
    for w_ref, o_ref, ob_ref in ((wk_ref, k_ref, kb_ref), (wv_ref, v_ref, vb_ref)):
        kv = _dot(m, w_ref[...])
        ob_ref[0] = kv.astype(BF16)
        for hd in range(XA_HEADS):
            o_ref[0, :, hd, :] = kv[:, hd * XA_HEAD_DIM:(hd + 1) * XA_HEAD_DIM]


def _memkv(mem, g, wk, wv):
    n_b = mem.shape[0]
    row_spec = pl.BlockSpec((1, MEM_LEN, D_MODEL), lambda i: (i, 0, 0))
    head_spec = pl.BlockSpec((1, MEM_LEN, XA_HEADS, XA_HEAD_DIM), lambda i: (i, 0, 0, 0))
    return pl.pallas_call(
        _memkv_kernel,
        grid=(n_b,),
        in_specs=[row_spec, _const_spec((1, D_MODEL)), _const_spec((D_MODEL, D_MODEL)),
                  _const_spec((D_MODEL, D_MODEL))],
        out_specs=[head_spec, head_spec, row_spec, row_spec],
        out_shape=[jax.ShapeDtypeStruct((n_b, MEM_LEN, XA_HEADS, XA_HEAD_DIM), F32)] * 2
        + [jax.ShapeDtypeStruct((n_b, MEM_LEN, D_MODEL), BF16)] * 2,
        compiler_params=_params(("parallel",)),
        name="memkv",
    )(mem, g, wk, wv)


def _attend(q, head_k, head_v):
    outs = []
    for hd in range(XA_HEADS):
        cols = slice(hd * XA_HEAD_DIM, (hd + 1) * XA_HEAD_DIM)
        s = lax.dot_general(q[:, cols].astype(BF16), head_k(hd), (((1,), (1,)), ((), ())),
                            preferred_element_type=F32) * (XA_HEAD_DIM ** -0.5)
        e = jnp.exp(s - jnp.max(s, axis=-1, keepdims=True))
        p = (e / jnp.sum(e, axis=-1, keepdims=True)).astype(BF16)
        outs.append(_dot(p, head_v(hd)).astype(BF16))
    return jnp.concatenate(outs, axis=1)


def _xattn_prompt_kernel(x_ref, g_ref, wq_ref, k_ref, v_ref, wo_ref, o_ref):
    x = x_ref[...]
    q = _dot(_rmsnorm(x, g_ref[...]).astype(BF16), wq_ref[...])
    o = _attend(q, lambda hd: k_ref[:, hd * XA_HEAD_DIM:(hd + 1) * XA_HEAD_DIM],
                lambda hd: v_ref[:, hd * XA_HEAD_DIM:(hd + 1) * XA_HEAD_DIM])
    o_ref[...] = x + _dot(o, wo_ref[...])


def _xattn_prompt(x, g, wq, k, v, wo):
    n_b, n_t = x.shape[0], x.shape[1]
    x_spec = pl.BlockSpec((None, ROW_TILE, D_MODEL), lambda b, i: (b, i, 0))
    kv_spec = pl.BlockSpec((None, MEM_LEN, D_MODEL), lambda b, i: (b, 0, 0))
    return pl.pallas_call(
        _xattn_prompt_kernel,
        grid=(n_b, n_t // ROW_TILE),
        in_specs=[x_spec, _const_spec((1, D_MODEL)), _const_spec((D_MODEL, D_MODEL)),
                  kv_spec, kv_spec, _const_spec((D_MODEL, D_MODEL))],
        out_specs=x_spec,
        out_shape=jax.ShapeDtypeStruct(x.shape, F32),
        compiler_params=_params(("parallel", "parallel")),
        name="xattn_prompt",
    )(x, g, wq, k, v, wo)


def _xattn_sample_kernel(x_ref, g_ref, wq_ref, k0_ref, k1_ref, v0_ref, v1_ref, wo_ref, o_ref, att_scr, s_scr,
                         *, bc, tq):
    x = x_ref[...]
    q = _dot(_rmsnorm(x, g_ref[...]).astype(BF16), wq_ref[...])
    halves = [[r.reshape(bc * MEM_LEN * XA_HEADS, LANES) for r in pair]
              for pair in ((k0_ref, k1_ref), (v0_ref, v1_ref))]

    def head(pair, b, hd):
        rows = pl.ds(b * MEM_LEN * XA_HEADS + hd, MEM_LEN, stride=XA_HEADS)
        return jnp.concatenate([r[rows, :] for r in pair], axis=1).astype(BF16)

    pairs = [(b, hd) for b in range(bc) for hd in range(XA_HEADS)]
    for n, (b, hd) in enumerate(pairs):
        qh = q[b * tq:(b + 1) * tq, hd * XA_HEAD_DIM:(hd + 1) * XA_HEAD_DIM].astype(BF16)
        s_scr[n * tq:(n + 1) * tq, :] = lax.dot_general(
            qh, head(halves[0], b, hd), (((1,), (1,)), ((), ())),
            preferred_element_type=F32) * (XA_HEAD_DIM ** -0.5)
    s = s_scr[...]
    e = jnp.exp(s - jnp.max(s, axis=-1, keepdims=True))
    s_scr[...] = e / jnp.sum(e, axis=-1, keepdims=True)
    for n, (b, hd) in enumerate(pairs):
        p = s_scr[n * tq:(n + 1) * tq, :].astype(BF16)
        att_scr[b * tq:(b + 1) * tq, hd * XA_HEAD_DIM:(hd + 1) * XA_HEAD_DIM] = _dot(p, head(halves[1], b, hd))
    o_ref[...] = x + _dot(att_scr[...].astype(BF16), wo_ref[...])


def _xattn_sample(x, g, wq, k, v, wo, *, bc):
    n_b = k.shape[0]
    tq = x.shape[0] // n_b
    x_spec = pl.BlockSpec((bc * tq, D_MODEL), lambda j: (j, 0))
    kv_specs = [pl.BlockSpec((bc, MEM_LEN, XA_HEADS, LANES), functools.partial(lambda half, j: (j, 0, 0, half), half))
                for half in range(XA_HEAD_DIM // LANES)]
    return pl.pallas_call(
        functools.partial(_xattn_sample_kernel, bc=bc, tq=tq),
        grid=(n_b // bc,),
        in_specs=[x_spec, _const_spec((1, D_MODEL)), _const_spec((D_MODEL, D_MODEL)),
                  *kv_specs, *kv_specs, _const_spec((D_MODEL, D_MODEL))],
        out_specs=x_spec,
        out_shape=jax.ShapeDtypeStruct(x.shape, F32),
        scratch_shapes=[pltpu.VMEM((bc * tq, D_MODEL), F32),
                        pltpu.VMEM((bc * XA_HEADS * tq, MEM_LEN), F32)],
        compiler_params=_params(("parallel",)),
        name="xattn_sample",
    )(x, g, wq, k, k, v, v, wo)


def _block_diag_tiles(w):
    groups, dg, _ = w.shape
    per_tile = MXU_TILE // dg
    w = w.reshape(groups // per_tile, per_tile, dg, dg)
    eye = jnp.eye(per_tile, dtype=w.dtype)
    tiles = jnp.einsum("npij,pq->npiqj", w, eye)
    return tiles.reshape(groups // per_tile, MXU_TILE, MXU_TILE).astype(BF16)


def _row(v):
    return v.reshape(1, -1).astype(F32)


def _layer(x_prompt, x_sample, mem_prompt, sconv, slru, spool, cache_k, cache_v, p, final_norm):
    n_b, n_t, _ = x_prompt.shape
    s_b, s_t, _ = x_sample.shape
    ffn1 = (_row(p["ffn1_norm"]), p["ffn1_w_gate"], p["ffn1_w_up"], p["ffn1_w_down"], _row(final_norm))
    ffn2 = (_row(p["ffn2_norm"]), p["ffn2_w_gate"], p["ffn2_w_up"], p["ffn2_w_down"], _row(final_norm))
    mixw = {
        "mix_norm": _row(p["mix_norm"]), "w_in": p["w_in"].astype(BF16),
        "conv_w": p["conv_w"].astype(F32), "conv_b": _row(p["conv_b"]),
        "lru_wa": _block_diag_tiles(p["lru_wa"]), "lru_ba": _row(p["lru_ba"]),
        "lru_wx": _block_diag_tiles(p["lru_wx"]), "lru_bx": _row(p["lru_bx"]),
        "lru_lambda": _row(p["lru_lambda"]),
        "pool_w": _block_diag_tiles(p["pool_w"]), "pool_scale": _row(p["pool_scale"]),
        "w_out": p["w_out"].astype(BF16),
    }
    gx, wq, wo = _row(p["xattn_norm"]), p["xattn_wq"].astype(BF16), p["xattn_wo"].astype(BF16)

    mk, mv, mk_bf, mv_bf = _memkv(mem_prompt, _row(p["mem_norm"]),
                                  p["xattn_wk"].astype(BF16), p["xattn_wv"].astype(BF16))

    xp = _ffn(x_prompt.reshape(n_b * n_t, D_MODEL), *ffn1, final_norm=False)
    xs = _ffn(x_sample.reshape(s_b * s_t, D_MODEL), *ffn1, final_norm=False)

    xp, p_conv, p_lru, p_pool = _mix(
        xp.reshape(n_b, n_t, D_MODEL), mixw, jnp.zeros((CONV_HIST, n_b, LRU_WIDTH), F32),
        jnp.zeros((n_b, LRU_WIDTH), F32), jnp.zeros((POOL_HIST, n_b, POOL_WIDTH), F32),
        tt=ROW_TILE // n_b, bb=n_b, pos0=0)
    xp = _xattn_prompt(xp, gx, wq, mk_bf, mv_bf, wo)

    xs, s_conv, s_lru, s_pool = _mix(
        xs.reshape(s_b, s_t, D_MODEL), mixw, jnp.swapaxes(sconv, 0, 1), slru, jnp.swapaxes(spool, 0, 1),
        tt=s_t, bb=ROW_TILE // s_t, pos0=PAST_LEN)
    xs = _xattn_sample(xs.reshape(s_b * s_t, D_MODEL), gx, wq, cache_k, cache_v, wo, bc=8)

    yp = _ffn(xp.reshape(n_b * n_t, D_MODEL), *ffn2, final_norm=True).reshape(n_b, n_t, D_MODEL)
    ys = _ffn(xs, *ffn2, final_norm=True).reshape(s_b, s_t, D_MODEL)

    states = (jnp.swapaxes(p_conv, 0, 1), p_lru, jnp.swapaxes(p_pool, 0, 1), mk, mv,
              jnp.swapaxes(s_conv, 0, 1), s_lru, jnp.swapaxes(s_pool, 0, 1))
    return yp, ys, states


def kernel(x_prompt, x_sample, mem_prompt, state_conv, state_lru, state_pool, cache_mem_k, cache_mem_v, ffn1_norm, ffn1_w_gate, ffn1_w_up, ffn1_w_down, mix_norm, w_in, conv_w, conv_b, lru_wa, lru_ba, lru_wx, lru_bx, lru_lambda, pool_w, pool_scale, w_out, xattn_norm, mem_norm, xattn_wq, xattn_wk, xattn_wv, xattn_wo, ffn2_norm, ffn2_w_gate, ffn2_w_up, ffn2_w_down, final_norm):
    depth = ffn1_norm.shape[0]
    assert depth == 1, "the final RMSNorm is fused into the last layer's second FFN"
    names = ("ffn1_norm", "ffn1_w_gate", "ffn1_w_up", "ffn1_w_down", "mix_norm", "w_in", "conv_w", "conv_b",
             "lru_wa", "lru_ba", "lru_wx", "lru_bx", "lru_lambda", "pool_w", "pool_scale", "w_out",
             "xattn_norm", "mem_norm", "xattn_wq", "xattn_wk", "xattn_wv", "xattn_wo",
             "ffn2_norm", "ffn2_w_gate", "ffn2_w_up", "ffn2_w_down")
    stacked = (ffn1_norm, ffn1_w_gate, ffn1_w_up, ffn1_w_down, mix_norm, w_in, conv_w, conv_b,
               lru_wa, lru_ba, lru_wx, lru_bx, lru_lambda, pool_w, pool_scale, w_out,
               xattn_norm, mem_norm, xattn_wq, xattn_wk, xattn_wv, xattn_wo,
               ffn2_norm, ffn2_w_gate, ffn2_w_up, ffn2_w_down)
    p = {n: a[0] for n, a in zip(names, stacked)}
    yp, ys, st = _layer(x_prompt, x_sample, mem_prompt, state_conv[0], state_lru[0], state_pool[0],
                        cache_mem_k[0], cache_mem_v[0], p, final_norm)
    p_conv, p_lru, p_pool, p_mk, p_mv, s_conv, s_lru, s_pool = (s[None] for s in st)
    return (yp, ys, p_conv, p_lru, p_pool, p_mk, p_mv, s_conv, s_lru, s_pool)
```

```python
import functools

import jax
import jax.numpy as jnp
from jax import lax
from jax.experimental import pallas as pl
from jax.experimental.pallas import tpu as pltpu

F32 = jnp.float32
BF16 = jnp.bfloat16

D_MODEL = 1024
D_FF = 2816
LRU_WIDTH = 512
LRU_HEADS = 8
LRU_C = 8.0
CONV_W = 4
POOL_WIDTH = 512
POOL_WINDOWS = (2, 4, 8, 16)
POOL_GROUP_DIM = POOL_WIDTH // len(POOL_WINDOWS)
POOL_HIST = max(POOL_WINDOWS) - 1
CONV_HIST = CONV_W - 1
MEM_LEN = 256
XA_HEADS = 4
XA_HEAD_DIM = D_MODEL // XA_HEADS
EPS = 1e-6
PAST_LEN = 16384

MXU_TILE = 256
LANES = 128
assert XA_HEAD_DIM == 2 * LANES
FF_CHUNK = MXU_TILE
N_FF_CHUNKS = D_FF // FF_CHUNK
ROW_TILE = 512
SUB_TILES = 2
WEIGHT_LOOKAHEAD = 4
COL_STAGE_SLOTS = 4
ROW_STAGE_SLOTS = 2
VMEM_LIMIT_BYTES = 56 * 1024 * 1024


def _rmsnorm(x, g):
    var = jnp.mean(x * x, axis=-1, keepdims=True)
    return x * lax.rsqrt(var + EPS) * g


def _dot(a, b):
    return jnp.dot(a, b, preferred_element_type=F32)


def _const_spec(shape):
    zeros = (0,) * len(shape)
    return pl.BlockSpec(shape, lambda *_: zeros, pipeline_mode=pl.Buffered(1))


def _params(semantics):
    return pltpu.CompilerParams(dimension_semantics=semantics, vmem_limit_bytes=VMEM_LIMIT_BYTES)


def _load_ffn_weights(wg_hbm, wu_hbm, wd_hbm, wg_scr, wu_scr, wd_scr, col_stage, row_stage, sems):
    stages = (col_stage, row_stage)
    items = []
    for c in range(N_FF_CHUNKS):
        cols = pl.ds(c * FF_CHUNK, FF_CHUNK)
        items += [(wg_hbm.at[:, cols], 0, wg_scr.at[c]), (wu_hbm.at[:, cols], 0, wu_scr.at[c]),
                  (wd_hbm.at[cols, :], 1, wd_scr.at[c])]
    uses = [0, 0]
    plan = []
    for src, kind, dst in items:
        slot = uses[kind] % stages[kind].shape[0]
        uses[kind] += 1
        staged = stages[kind].at[slot]
        plan.append((pltpu.make_async_copy(src, staged, sems.at[kind, slot]), staged, dst, (kind, slot)))
    slot_free_after = {}
    converted = -1

    def start(m):
        key = plan[m][3]
        assert slot_free_after.get(key, -1) <= converted, "staging slot reused before it was converted"
        slot_free_after[key] = m
        plan[m][0].start()

    for m in range(min(WEIGHT_LOOKAHEAD, len(plan))):
        start(m)
    for n, (copy, staged, dst, _) in enumerate(plan):
        if n + WEIGHT_LOOKAHEAD < len(plan):
            start(n + WEIGHT_LOOKAHEAD)
        copy.wait()
        dst[...] = staged[...].astype(BF16)
        converted = n


def _ffn_kernel(x_ref, g_ref, wg_hbm, wu_hbm, wd_hbm, fn_ref, o_ref,
                wg_scr, wu_scr, wd_scr, col_stage, row_stage, sems, h_scr, a_scr, acc_scr, *, final_norm):
    @pl.when(pl.program_id(0) == 0)
    def _():
        _load_ffn_weights(wg_hbm, wu_hbm, wd_hbm, wg_scr, wu_scr, wd_scr, col_stage, row_stage, sems)

    subs = range(SUB_TILES)
    rows = [slice(s * ROW_TILE, (s + 1) * ROW_TILE) for s in subs]
    for s in subs:
        h_scr[s] = _rmsnorm(x_ref[rows[s], :], g_ref[...]).astype(BF16)

    def gate_up(s, c):
        h = h_scr[s]
        g = _dot(h, wg_scr[c])
        u = _dot(h, wu_scr[c])
        a_scr[s, c % 2] = (g * jax.nn.sigmoid(g) * u).astype(BF16)

    def down(s, c):
        d = _dot(a_scr[s, c % 2], wd_scr[c])
        if c == 0:
            acc_scr[s] = d
        else:
            acc_scr[s] += d

    for s in subs:
        gate_up(s, 0)
    for c in range(1, N_FF_CHUNKS):
        for s in subs:
            gate_up(s, c)
            down(s, c - 1)
    for s in subs:
        down(s, N_FF_CHUNKS - 1)
    for s in subs:
        y = x_ref[rows[s], :] + 0.5 * acc_scr[s]
        if final_norm:
            y = _rmsnorm(y, fn_ref[...])
        o_ref[rows[s], :] = y


def _ffn(x, g, wg, wu, wd, fn, *, final_norm):
    row_spec = pl.BlockSpec((SUB_TILES * ROW_TILE, D_MODEL), lambda i: (i, 0))
    hbm = pl.BlockSpec(memory_space=pl.ANY)
    return pl.pallas_call(
        functools.partial(_ffn_kernel, final_norm=final_norm),
        grid=(x.shape[0] // (SUB_TILES * ROW_TILE),),
        in_specs=[row_spec, _const_spec((1, D_MODEL)), hbm, hbm, hbm, _const_spec((1, D_MODEL))],
        out_specs=row_spec,
        out_shape=jax.ShapeDtypeStruct(x.shape, F32),
        scratch_shapes=[pltpu.VMEM((N_FF_CHUNKS, D_MODEL, FF_CHUNK), BF16),
                        pltpu.VMEM((N_FF_CHUNKS, D_MODEL, FF_CHUNK), BF16),
                        pltpu.VMEM((N_FF_CHUNKS, FF_CHUNK, D_MODEL), BF16),
                        pltpu.VMEM((COL_STAGE_SLOTS, D_MODEL, FF_CHUNK), F32),
                        pltpu.VMEM((ROW_STAGE_SLOTS, FF_CHUNK, D_MODEL), F32),
                        pltpu.SemaphoreType.DMA((2, COL_STAGE_SLOTS)),
                        pltpu.VMEM((SUB_TILES, ROW_TILE, D_MODEL), BF16),
                        pltpu.VMEM((SUB_TILES, 2, ROW_TILE, FF_CHUNK), BF16),
                        pltpu.VMEM((SUB_TILES, ROW_TILE, D_MODEL), F32)],
        compiler_params=_params(("arbitrary",)),
        name="ffn_final" if final_norm else "ffn",
    )(x, g, wg, wu, wd, fn)


def _block_diag_dot(u, w_ref):
    return jnp.concatenate([_dot(u[:, :MXU_TILE], w_ref[0]), _dot(u[:, MXU_TILE:], w_ref[1])], axis=1)


def _mix_kernel(x_ref, gm_ref, win_ref, cw_ref, cb_ref, wa_ref, ba_ref, wx_ref, bx_ref, lam_ref,
                pw_ref, ps_ref, wout_ref, sconv_ref, slru_ref, spool_ref,
                o_ref, oconv_ref, olru_ref, opool_ref,
                lru_ext, pool_ext, h_scr, a_scr, b_scr, *, tt, bb, pos0):
    rows = tt * bb
    i = pl.program_id(1)

    @pl.when(i == 0)
    def _():
        lru_ext[0:CONV_HIST * bb, :] = sconv_ref[...].reshape(CONV_HIST * bb, LRU_WIDTH)
        pool_ext[0:POOL_HIST * bb, :] = spool_ref[...].reshape(POOL_HIST * bb, POOL_WIDTH)
        h_scr[...] = slru_ref[...]

    x = jnp.swapaxes(x_ref[...], 0, 1).reshape(rows, D_MODEL)
    h = _rmsnorm(x, gm_ref[...]).astype(BF16)
    proj = _dot(h, win_ref[...])
    lru_ext[CONV_HIST * bb:, :] = proj[:, :LRU_WIDTH]
    gate = proj[:, LRU_WIDTH:2 * LRU_WIDTH]
    u_pool = proj[:, 2 * LRU_WIDTH:]
    pool_ext[POOL_HIST * bb:, :] = u_pool

    cw = cw_ref[...]
    u = cb_ref[...] + sum(lru_ext[k * bb:k * bb + rows, :] * cw[k:k + 1, :] for k in range(CONV_W))
    new_conv = lru_ext[tt * bb:(tt + CONV_HIST) * bb, :]
    oconv_ref[...] = new_conv.reshape(CONV_HIST, bb, LRU_WIDTH)
    lru_ext[0:CONV_HIST * bb, :] = new_conv

    ub = u.astype(BF16)
    r = jax.nn.sigmoid(_block_diag_dot(ub, wa_ref) + ba_ref[...])
    ig = jax.nn.sigmoid(_block_diag_dot(ub, wx_ref) + bx_ref[...])
    neg_lam = -lam_ref[...]
    softplus = jnp.maximum(neg_lam, 0.0) + jnp.log1p(jnp.exp(-jnp.abs(neg_lam)))
    log_a = (-LRU_C * softplus) * r
    a = jnp.exp(log_a)
    a_scr[...] = a
    b_scr[...] = jnp.sqrt(jnp.maximum(1.0 - a * a, 0.0)) * ig * u

    hcur = h_scr[...]
    for t in range(tt):
        sl = slice(t * bb, (t + 1) * bb)
        hcur = a_scr[sl, :] * hcur + b_scr[sl, :]
        b_scr[sl, :] = hcur
    h_scr[...] = hcur
    olru_ref[...] = hcur
    y_lru = (jax.nn.gelu(gate) * b_scr[...]).astype(BF16)

    t_idx = lax.broadcasted_iota(jnp.int32, (rows, POOL_GROUP_DIM), 0) // bb
    pos = pos0 + i * tt + t_idx
    deltas = []
    for g, win in enumerate(POOL_WINDOWS):
        lanes = slice(g * POOL_GROUP_DIM, (g + 1) * POOL_GROUP_DIM)
        s = pool_ext[:, lanes]
        span = 1
        while span < win:
            s = s[span * bb:, :] + s[:-span * bb, :]
            span *= 2
        s = s[s.shape[0] - rows:, :]
        cnt = jnp.minimum(pos + 1, win).astype(F32)
        deltas.append(s / cnt - u_pool[:, lanes])
    delta = jnp.concatenate(deltas, axis=1).astype(BF16)
    y_pool = (ps_ref[...] * _block_diag_dot(delta, pw_ref)).astype(BF16)
    new_pool = pool_ext[tt * bb:(tt + POOL_HIST) * bb, :]
    opool_ref[...] = new_pool.reshape(POOL_HIST, bb, POOL_WIDTH)
    pool_ext[0:POOL_HIST * bb, :] = new_pool

    out = x + _dot(y_lru, wout_ref[0:LRU_WIDTH, :]) + _dot(y_pool, wout_ref[LRU_WIDTH:, :])
    o_ref[...] = jnp.swapaxes(out.reshape(tt, bb, D_MODEL), 0, 1)


def _mix(x, w, sconv, slru, spool, *, tt, bb, pos0):
    n_b, n_t = x.shape[0], x.shape[1]
    rows = tt * bb
    x_spec = pl.BlockSpec((bb, tt, D_MODEL), lambda j, i: (j, i, 0))
    conv_spec = pl.BlockSpec((CONV_HIST, bb, LRU_WIDTH), lambda j, i: (0, j, 0))
    lru_spec = pl.BlockSpec((bb, LRU_WIDTH), lambda j, i: (j, 0))
    pool_spec = pl.BlockSpec((POOL_HIST, bb, POOL_WIDTH), lambda j, i: (0, j, 0))
    return pl.pallas_call(
        functools.partial(_mix_kernel, tt=tt, bb=bb, pos0=pos0),
        grid=(n_b // bb, n_t // tt),
        in_specs=[x_spec, _const_spec((1, D_MODEL)), _const_spec((D_MODEL, 3 * LRU_WIDTH)),
                  _const_spec((CONV_W, LRU_WIDTH)), _const_spec((1, LRU_WIDTH)),
                  _const_spec((2, MXU_TILE, MXU_TILE)), _const_spec((1, LRU_WIDTH)),
                  _const_spec((2, MXU_TILE, MXU_TILE)), _const_spec((1, LRU_WIDTH)),
                  _const_spec((1, LRU_WIDTH)),
                  _const_spec((2, MXU_TILE, MXU_TILE)), _const_spec((1, POOL_WIDTH)),
                  _const_spec((D_MODEL, D_MODEL)),
                  conv_spec, lru_spec, pool_spec],
        out_specs=[x_spec, conv_spec, lru_spec, pool_spec],
        out_shape=[jax.ShapeDtypeStruct(x.shape, F32),
                   jax.ShapeDtypeStruct((CONV_HIST, n_b, LRU_WIDTH), F32),
                   jax.ShapeDtypeStruct((n_b, LRU_WIDTH), F32),
                   jax.ShapeDtypeStruct((POOL_HIST, n_b, POOL_WIDTH), F32)],
        scratch_shapes=[pltpu.VMEM(((CONV_HIST + tt) * bb, LRU_WIDTH), F32),
                        pltpu.VMEM(((POOL_HIST + tt) * bb, POOL_WIDTH), F32),
                        pltpu.VMEM((bb, LRU_WIDTH), F32),
                        pltpu.VMEM((rows, LRU_WIDTH), F32),
                        pltpu.VMEM((rows, LRU_WIDTH), F32)],
        compiler_params=_params(("parallel", "arbitrary")),
        name="mix",
    )(x, w["mix_norm"], w["w_in"], w["conv_w"], w["conv_b"], w["lru_wa"], w["lru_ba"],
      w["lru_wx"], w["lru_bx"], w["lru_lambda"], w["pool_w"], w["pool_scale"], w["w_out"],
      sconv, slru, spool)


def _memkv_kernel(m_ref, g_ref, wk_ref, wv_ref, k_ref, v_ref, kb_ref, vb_ref):
    m = _rmsnorm(m_ref[0], g_ref[...]).astype(BF16)
    for w_ref, o_ref, ob_ref in ((wk_ref, k_ref, kb_ref), (wv_ref, v_ref, vb_ref)):
        kv = _dot(m, w_ref[...])
        ob_ref[0] = kv.astype(BF16)
        for hd in range(XA_HEADS):
            o_ref[0, :, hd, :] = kv[:, hd * XA_HEAD_DIM:(hd + 1) * XA_HEAD_DIM]


def _memkv(mem, g, wk, wv):
    n_b = mem.shape[0]
    row_spec = pl.BlockSpec((1, MEM_LEN, D_MODEL), lambda i: (i, 0, 0))
    head_spec = pl.BlockSpec((1, MEM_LEN, XA_HEADS, XA_HEAD_DIM), lambda i: (i, 0, 0, 0))
    return pl.pallas_call(
        _memkv_kernel,
        grid=(n_b,),
        in_specs=[row_spec, _const_spec((1, D_MODEL)), _const_spec((D_MODEL, D_MODEL)),
                  _const_spec((D_MODEL, D_MODEL))],
        out_specs=[head_spec, head_spec, row_spec, row_spec],
        out_shape=[jax.ShapeDtypeStruct((n_b, MEM_LEN, XA_HEADS, XA_HEAD_DIM), F32)] * 2
        + [jax.ShapeDtypeStruct((n_b, MEM_LEN, D_MODEL), BF16)] * 2,
        compiler_params=_params(("parallel",)),
        name="memkv",
    )(mem, g, wk, wv)


def _xattn_prompt_kernel(x_ref, g_ref, wq_ref, k_ref, v_ref, wo_ref, o_ref, q_scr, s_scr, att_scr):
    subs = range(SUB_TILES)
    rows = [slice(s * ROW_TILE, (s + 1) * ROW_TILE) for s in subs]
    heads = [slice(hd * XA_HEAD_DIM, (hd + 1) * XA_HEAD_DIM) for hd in range(XA_HEADS)]

    def project_q(s):
        q_scr[s] = _dot(_rmsnorm(x_ref[rows[s], :], g_ref[...]).astype(BF16), wq_ref[...]).astype(BF16)

    def scores(s):
        for hd, cols in enumerate(heads):
            s_scr[s, hd] = lax.dot_general(q_scr[s, :, cols], k_ref[:, cols], (((1,), (1,)), ((), ())),
                                           preferred_element_type=F32) * (XA_HEAD_DIM ** -0.5)

    def attend(s):
        sc = s_scr[s]
        e = jnp.exp(sc - jnp.max(sc, axis=-1, keepdims=True))
        s_scr[s] = e / jnp.sum(e, axis=-1, keepdims=True)
        for hd, cols in enumerate(heads):
            att_scr[s, :, cols] = _dot(s_scr[s, hd].astype(BF16), v_ref[:, cols]).astype(BF16)

    def project_out(s):
        o_ref[rows[s], :] = x_ref[rows[s], :] + _dot(att_scr[s], wo_ref[...])

    stages = (project_q, scores, attend, project_out)
    for k in range(SUB_TILES + len(stages) - 1):
        for depth in reversed(range(len(stages))):
            if 0 <= k - depth < SUB_TILES:
                stages[depth](k - depth)


def _xattn_prompt(x, g, wq, k, v, wo):
    n_b, n_t = x.shape[0], x.shape[1]
    tile = SUB_TILES * ROW_TILE
    x_spec = pl.BlockSpec((None, tile, D_MODEL), lambda b, i: (b, i, 0))
    kv_spec = pl.BlockSpec((None, MEM_LEN, D_MODEL), lambda b, i: (b, 0, 0))
    return pl.pallas_call(
        _xattn_prompt_kernel,
        grid=(n_b, n_t // tile),
        in_specs=[x_spec, _const_spec((1, D_MODEL)), _const_spec((D_MODEL, D_MODEL)),
                  kv_spec, kv_spec, _const_spec((D_MODEL, D_MODEL))],
        out_specs=x_spec,
        out_shape=jax.ShapeDtypeStruct(x.shape, F32),
        scratch_shapes=[pltpu.VMEM((SUB_TILES, ROW_TILE, D_MODEL), BF16),
                        pltpu.VMEM((SUB_TILES, XA_HEADS, ROW_TILE, MEM_LEN), F32),
                        pltpu.VMEM((SUB_TILES, ROW_TILE, D_MODEL), BF16)],
        compiler_params=_params(("parallel", "parallel")),
        name="xattn_prompt",
    )(x, g, wq, k, v, wo)


def _xattn_sample_kernel(x_ref, g_ref, wq_ref, k0_ref, k1_ref, v0_ref, v1_ref, wo_ref, o_ref, att_scr, s_scr,
                         *, bc, tq):
    x = x_ref[...]
    q = _dot(_rmsnorm(x, g_ref[...]).astype(BF16), wq_ref[...])
    halves = [[r.reshape(bc * MEM_LEN * XA_HEADS, LANES) for r in pair]
              for pair in ((k0_ref, k1_ref), (v0_ref, v1_ref))]

    def head(pair, b, hd):
        rows = pl.ds(b * MEM_LEN * XA_HEADS + hd, MEM_LEN, stride=XA_HEADS)
        return jnp.concatenate([r[rows, :] for r in pair], axis=1).astype(BF16)

    pairs = [(b, hd) for b in range(bc) for hd in range(XA_HEADS)]
    for n, (b, hd) in enumerate(pairs):
        qh = q[b * tq:(b + 1) * tq, hd * XA_HEAD_DIM:(hd + 1) * XA_HEAD_DIM].astype(BF16)
        s_scr[n * tq:(n + 1) * tq, :] = lax.dot_general(
            qh, head(halves[0], b, hd), (((1,), (1,)), ((), ())),
            preferred_element_type=F32) * (XA_HEAD_DIM ** -0.5)
    s = s_scr[...]
    e = jnp.exp(s - jnp.max(s, axis=-1, keepdims=True))
    s_scr[...] = e / jnp.sum(e, axis=-1, keepdims=True)
    for n, (b, hd) in enumerate(pairs):
        p = s_scr[n * tq:(n + 1) * tq, :].astype(BF16)
        att_scr[b * tq:(b + 1) * tq, hd * XA_HEAD_DIM:(hd + 1) * XA_HEAD_DIM] = _dot(p, head(halves[1], b, hd))
    o_ref[...] = x + _dot(att_scr[...].astype(BF16), wo_ref[...])


def _xattn_sample(x, g, wq, k, v, wo, *, bc):
    n_b = k.shape[0]
    tq = x.shape[0] // n_b
    x_spec = pl.BlockSpec((bc * tq, D_MODEL), lambda j: (j, 0))
    kv_specs = [pl.BlockSpec((bc, MEM_LEN, XA_HEADS, LANES), functools.partial(lambda half, j: (j, 0, 0, half), half))
                for half in range(XA_HEAD_DIM // LANES)]
    return pl.pallas_call(
        functools.partial(_xattn_sample_kernel, bc=bc, tq=tq),
        grid=(n_b // bc,),
        in_specs=[x_spec, _const_spec((1, D_MODEL)), _const_spec((D_MODEL, D_MODEL)),
                  *kv_specs, *kv_specs, _const_spec((D_MODEL, D_MODEL))],
        out_specs=x_spec,
        out_shape=jax.ShapeDtypeStruct(x.shape, F32),
        scratch_shapes=[pltpu.VMEM((bc * tq, D_MODEL), F32),
                        pltpu.VMEM((bc * XA_HEADS * tq, MEM_LEN), F32)],
        compiler_params=_params(("parallel",)),
        name="xattn_sample",
    )(x, g, wq, k, k, v, v, wo)


def _block_diag_tiles(w):
    groups, dg, _ = w.shape
    per_tile = MXU_TILE // dg
    w = w.reshape(groups // per_tile, per_tile, dg, dg)
    eye = jnp.eye(per_tile, dtype=w.dtype)
    tiles = jnp.einsum("npij,pq->npiqj", w, eye)
    return tiles.reshape(groups // per_tile, MXU_TILE, MXU_TILE).astype(BF16)


def _row(v):
    return v.reshape(1, -1).astype(F32)


def _layer(x_prompt, x_sample, mem_prompt, sconv, slru, spool, cache_k, cache_v, p, final_norm):
    n_b, n_t, _ = x_prompt.shape
    s_b, s_t, _ = x_sample.shape
    ffn1 = (_row(p["ffn1_norm"]), p["ffn1_w_gate"], p["ffn1_w_up"], p["ffn1_w_down"], _row(final_norm))
    ffn2 = (_row(p["ffn2_norm"]), p["ffn2_w_gate"], p["ffn2_w_up"], p["ffn2_w_down"], _row(final_norm))
    mixw = {
        "mix_norm": _row(p["mix_norm"]), "w_in": p["w_in"].astype(BF16),
        "conv_w": p["conv_w"].astype(F32), "conv_b": _row(p["conv_b"]),
        "lru_wa": _block_diag_tiles(p["lru_wa"]), "lru_ba": _row(p["lru_ba"]),
        "lru_wx": _block_diag_tiles(p["lru_wx"]), "lru_bx": _row(p["lru_bx"]),
        "lru_lambda": _row(p["lru_lambda"]),
        "pool_w": _block_diag_tiles(p["pool_w"]), "pool_scale": _row(p["pool_scale"]),
        "w_out": p["w_out"].astype(BF16),
    }
    gx, wq, wo = _row(p["xattn_norm"]), p["xattn_wq"].astype(BF16), p["xattn_wo"].astype(BF16)

    mk, mv, mk_bf, mv_bf = _memkv(mem_prompt, _row(p["mem_norm"]),
                                  p["xattn_wk"].astype(BF16), p["xattn_wv"].astype(BF16))

    xp = _ffn(x_prompt.reshape(n_b * n_t, D_MODEL), *ffn1, final_norm=False)
    xs = _ffn(x_sample.reshape(s_b * s_t, D_MODEL), *ffn1, final_norm=False)

    xp, p_conv, p_lru, p_pool = _mix(
        xp.reshape(n_b, n_t, D_MODEL), mixw, jnp.zeros((CONV_HIST, n_b, LRU_WIDTH), F32),
        jnp.zeros((n_b, LRU_WIDTH), F32), jnp.zeros((POOL_HIST, n_b, POOL_WIDTH), F32),
        tt=ROW_TILE // n_b, bb=n_b, pos0=0)
    xp = _xattn_prompt(xp, gx, wq, mk_bf, mv_bf, wo)

    xs, s_conv, s_lru, s_pool = _mix(
        xs.reshape(s_b, s_t, D_MODEL), mixw, jnp.swapaxes(sconv, 0, 1), slru, jnp.swapaxes(spool, 0, 1),
        tt=s_t, bb=ROW_TILE // s_t, pos0=PAST_LEN)
    xs = _xattn_sample(xs.reshape(s_b * s_t, D_MODEL), gx, wq, cache_k, cache_v, wo, bc=8)

    yp = _ffn(xp.reshape(n_b * n_t, D_MODEL), *ffn2, final_norm=True).reshape(n_b, n_t, D_MODEL)
    ys = _ffn(xs, *ffn2, final_norm=True).reshape(s_b, s_t, D_MODEL)

    states = (jnp.swapaxes(p_conv, 0, 1), p_lru, jnp.swapaxes(p_pool, 0, 1), mk, mv,
              jnp.swapaxes(s_conv, 0, 1), s_lru, jnp.swapaxes(s_pool, 0, 1))
    return yp, ys, states


def kernel(x_prompt, x_sample, mem_prompt, state_conv, state_lru, state_pool, cache_mem_k, cache_mem_v, ffn1_norm, ffn1_w_gate, ffn1_w_up, ffn1_w_down, mix_norm, w_in, conv_w, conv_b, lru_wa, lru_ba, lru_wx, lru_bx, lru_lambda, pool_w, pool_scale, w_out, xattn_norm, mem_norm, xattn_wq, xattn_wk, xattn_wv, xattn_wo, ffn2_norm, ffn2_w_gate, ffn2_w_up, ffn2_w_down, final_norm):
    depth = ffn1_norm.shape[0]
    assert depth == 1, "the final RMSNorm is fused into the last layer's second FFN"
    names = ("ffn1_norm", "ffn1_w_gate", "ffn1_w_up", "ffn1_w_down", "mix_norm", "w_in", "conv_w", "conv_b",
             "lru_wa", "lru_ba", "lru_wx", "lru_bx", "lru_lambda", "pool_w", "pool_scale", "w_out",
             "xattn_norm", "mem_norm", "xattn_wq", "xattn_wk", "xattn_wv", "xattn_wo",
             "ffn2_norm", "ffn2_w_gate", "ffn2_w_up", "ffn2_w_down")
    stacked = (ffn1_norm, ffn1_w_gate, ffn1_w_up, ffn1_w_down, mix_norm, w_in, conv_w, conv_b,
               lru_wa, lru_ba, lru_wx, lru_bx, lru_lambda, pool_w, pool_scale, w_out,
               xattn_norm, mem_norm, xattn_wq, xattn_wk, xattn_wv, xattn_wo,
               ffn2_norm, ffn2_w_gate, ffn2_w_up, ffn2_w_down)
    p = {n: a[0] for n, a in zip(names, stacked)}
    yp, ys, st = _layer(x_prompt, x_sample, mem_prompt, state_conv[0], state_lru[0], state_pool[0],
                        cache_mem_k[0], cache_mem_v[0], p, final_norm)
    p_conv, p_lru, p_pool, p_mk, p_mv, s_conv, s_lru, s_pool = (s[None] for s in st)
    return (yp, ys, p_conv, p_lru, p_pool, p_mk, p_mv, s_conv, s_lru, s_pool)
```

```python
import functools
import itertools

import jax
import jax.numpy as jnp
from jax import lax
from jax.experimental import pallas as pl
from jax.experimental.pallas import tpu as pltpu

F32 = jnp.float32
BF16 = jnp.bfloat16

D_MODEL = 1024
D_FF = 2816
LRU_WIDTH = 512
LRU_HEADS = 8
LRU_C = 8.0
CONV_W = 4
POOL_WIDTH = 512
POOL_WINDOWS = (2, 4, 8, 16)
POOL_GROUP_DIM = POOL_WIDTH // len(POOL_WINDOWS)
POOL_HIST = max(POOL_WINDOWS) - 1
CONV_HIST = CONV_W - 1
MEM_LEN = 256
XA_HEADS = 4
XA_HEAD_DIM = D_MODEL // XA_HEADS
EPS = 1e-6
PAST_LEN = 16384

MXU_TILE = 256
LANES = 128
assert XA_HEAD_DIM == 2 * LANES
FF_CHUNK = MXU_TILE
N_FF_CHUNKS = D_FF // FF_CHUNK
ROW_TILE = 512
SUB_TILES = 2
WEIGHT_LOOKAHEAD = 4
COL_STAGE_SLOTS = 4
ROW_STAGE_SLOTS = 2
VMEM_LIMIT_BYTES = 56 * 1024 * 1024


def _rmsnorm(x, g):
    var = jnp.mean(x * x, axis=-1, keepdims=True)
    return x * lax.rsqrt(var + EPS) * g


def _dot(a, b):
    return jnp.dot(a, b, preferred_element_type=F32)


def _const_spec(shape):
    zeros = (0,) * len(shape)
    return pl.BlockSpec(shape, lambda *_: zeros, pipeline_mode=pl.Buffered(1))


def _params(semantics):
    return pltpu.CompilerParams(dimension_semantics=semantics, vmem_limit_bytes=VMEM_LIMIT_BYTES)


def _load_ffn_weights(wg_hbm, wu_hbm, wd_hbm, wg_scr, wu_scr, wd_scr, col_stage, row_stage, sems):
    stages = (col_stage, row_stage)
    items = []
    for c in range(N_FF_CHUNKS):
        cols = pl.ds(c * FF_CHUNK, FF_CHUNK)
        items += [(wg_hbm.at[:, cols], 0, wg_scr.at[c]), (wu_hbm.at[:, cols], 0, wu_scr.at[c]),
                  (wd_hbm.at[cols, :], 1, wd_scr.at[c])]
    uses = [0, 0]
    plan = []
    for src, kind, dst in items:
        slot = uses[kind] % stages[kind].shape[0]
        uses[kind] += 1
        staged = stages[kind].at[slot]
        plan.append((pltpu.make_async_copy(src, staged, sems.at[kind, slot]), staged, dst, (kind, slot)))
    slot_free_after = {}
    converted = -1

    def start(m):
        key = plan[m][3]
        assert slot_free_after.get(key, -1) <= converted, "staging slot reused before it was converted"
        slot_free_after[key] = m
        plan[m][0].start()

    for m in range(min(WEIGHT_LOOKAHEAD, len(plan))):
        start(m)
    for n, (copy, staged, dst, _) in enumerate(plan):
        if n + WEIGHT_LOOKAHEAD < len(plan):
            start(n + WEIGHT_LOOKAHEAD)
        copy.wait()
        dst[...] = staged[...].astype(BF16)
        converted = n


def _ffn_kernel(x_ref, g_ref, wg_hbm, wu_hbm, wd_hbm, fn_ref, o_ref,
                wg_scr, wu_scr, wd_scr, col_stage, row_stage, sems, h_scr, a_scr, acc_scr, *, final_norm):
    @pl.when(pl.program_id(0) == 0)
    def _():
        _load_ffn_weights(wg_hbm, wu_hbm, wd_hbm, wg_scr, wu_scr, wd_scr, col_stage, row_stage, sems)

    subs = range(SUB_TILES)
    rows = [slice(s * ROW_TILE, (s + 1) * ROW_TILE) for s in subs]
    for s in subs:
        h_scr[s] = _rmsnorm(x_ref[rows[s], :], g_ref[...]).astype(BF16)

    def gate_up(s, c):
        h = h_scr[s]
        g = _dot(h, wg_scr[c])
        u = _dot(h, wu_scr[c])
        a_scr[s, c % 2] = (g * jax.nn.sigmoid(g) * u).astype(BF16)

    def down(s, c):
        d = _dot(a_scr[s, c % 2], wd_scr[c])
        if c == 0:
            acc_scr[s] = d
        else:
            acc_scr[s] += d

    for s in subs:
        gate_up(s, 0)
    for c in range(1, N_FF_CHUNKS):
        for s in subs:
            gate_up(s, c)
            down(s, c - 1)
    for s in subs:
        down(s, N_FF_CHUNKS - 1)
    for s in subs:
        y = x_ref[rows[s], :] + 0.5 * acc_scr[s]
        if final_norm:
            y = _rmsnorm(y, fn_ref[...])
        o_ref[rows[s], :] = y


def _ffn(x, g, wg, wu, wd, fn, *, final_norm):
    row_spec = pl.BlockSpec((SUB_TILES * ROW_TILE, D_MODEL), lambda i: (i, 0))
    hbm = pl.BlockSpec(memory_space=pl.ANY)
    return pl.pallas_call(
        functools.partial(_ffn_kernel, final_norm=final_norm),
        grid=(x.shape[0] // (SUB_TILES * ROW_TILE),),
        in_specs=[row_spec, _const_spec((1, D_MODEL)), hbm, hbm, hbm, _const_spec((1, D_MODEL))],
        out_specs=row_spec,
        out_shape=jax.ShapeDtypeStruct(x.shape, F32),
        scratch_shapes=[pltpu.VMEM((N_FF_CHUNKS, D_MODEL, FF_CHUNK), BF16),
                        pltpu.VMEM((N_FF_CHUNKS, D_MODEL, FF_CHUNK), BF16),
                        pltpu.VMEM((N_FF_CHUNKS, FF_CHUNK, D_MODEL), BF16),
                        pltpu.VMEM((COL_STAGE_SLOTS, D_MODEL, FF_CHUNK), F32),
                        pltpu.VMEM((ROW_STAGE_SLOTS, FF_CHUNK, D_MODEL), F32),
                        pltpu.SemaphoreType.DMA((2, COL_STAGE_SLOTS)),
                        pltpu.VMEM((SUB_TILES, ROW_TILE, D_MODEL), BF16),
                        pltpu.VMEM((SUB_TILES, 2, ROW_TILE, FF_CHUNK), BF16),
                        pltpu.VMEM((SUB_TILES, ROW_TILE, D_MODEL), F32)],
        compiler_params=_params(("arbitrary",)),
        name="ffn_final" if final_norm else "ffn",
    )(x, g, wg, wu, wd, fn)


def _mix_kernel(x_ref, gm_ref, win_ref, cw_ref, cb_ref, wa_ref, ba_ref, wx_ref, bx_ref, lam_ref,
                pw_ref, ps_ref, wout_ref, sconv_ref, slru_ref, spool_ref,
                o_ref, oconv_ref, olru_ref, opool_ref,
                lru_ext, pool_ext, h_scr, x_scr, hn_scr, gate_scr, a_scr, b_scr, y_scr, *, tt, bb, n_sub, pos0):
    rows = tt * bb
    i = pl.program_id(1)

    @pl.when(i == 0)
    def _():
        lru_ext[0:CONV_HIST * bb, :] = sconv_ref[...].reshape(CONV_HIST * bb, LRU_WIDTH)
        pool_ext[0:POOL_HIST * bb, :] = spool_ref[...].reshape(POOL_HIST * bb, POOL_WIDTH)
        h_scr[...] = slru_ref[...]

    half = MXU_TILE

    def prepare_pieces(s):
        n_pieces = max(1, min(4, tt // 8))
        tq = tt // n_pieces

        def piece(q):
            x = jnp.swapaxes(x_ref[:, s * tt + q * tq:s * tt + (q + 1) * tq, :], 0, 1)
            x = x.reshape(tq * bb, D_MODEL)
            x_scr[s, q * tq * bb:(q + 1) * tq * bb, :] = x
            hn_scr[s, q * tq * bb:(q + 1) * tq * bb, :] = _rmsnorm(x, gm_ref[...]).astype(BF16)
        return [functools.partial(piece, q) for q in range(n_pieces)]

    def project_in_pieces(s):
        lru_rows = slice((CONV_HIST + s * tt) * bb, (CONV_HIST + (s + 1) * tt) * bb)
        pool_rows = slice((POOL_HIST + s * tt) * bb, (POOL_HIST + (s + 1) * tt) * bb)

        def piece(n):
            p = _dot(hn_scr[s], win_ref[:, n * half:(n + 1) * half])
            cols = slice((n % 2) * half, (n % 2 + 1) * half)
            if n < 2:
                lru_ext[lru_rows, cols] = p
            elif n < 4:
                gate_scr[s, :, cols] = p
            else:
                pool_ext[pool_rows, cols] = p
        return [functools.partial(piece, n) for n in range(3 * LRU_WIDTH // half)]

    def lru_piece(s, j):
        t0 = s * tt
        lanes = slice(j * half, (j + 1) * half)
        cw = cw_ref[:, lanes]
        u = cb_ref[:, lanes] + sum(lru_ext[(t0 + k) * bb:(t0 + k) * bb + rows, lanes] * cw[k:k + 1, :]
                                   for k in range(CONV_W))
        ub = u.astype(BF16)
        r = jax.nn.sigmoid(_dot(ub, wa_ref[j]) + ba_ref[:, lanes])
        ig = jax.nn.sigmoid(_dot(ub, wx_ref[j]) + bx_ref[:, lanes])
        neg_lam = -lam_ref[:, lanes]
        softplus = jnp.maximum(neg_lam, 0.0) + jnp.log1p(jnp.exp(-jnp.abs(neg_lam)))
        log_a = (-LRU_C * softplus) * r
        a = jnp.exp(log_a)
        a_scr[s, :, lanes] = a
        m = jnp.maximum(1.0 - a * a, 0.0)
        b_scr[s, :, lanes] = jnp.where(m > 0.0, m * lax.rsqrt(m), 0.0) * ig * u
        hcur = h_scr[:, lanes]
        for t in range(tt):
            sl = slice(t * bb, (t + 1) * bb)
            hcur = a_scr[s, sl, lanes] * hcur + b_scr[s, sl, lanes]
            b_scr[s, sl, lanes] = hcur
        h_scr[:, lanes] = hcur
        y_scr[s, :, lanes] = (jax.nn.gelu(gate_scr[s, :, lanes]) * b_scr[s, :, lanes]).astype(BF16)

    def pool_piece(s, j):
        t0 = s * tt
        t_idx = lax.broadcasted_iota(jnp.int32, (rows, POOL_GROUP_DIM), 0) // bb
        pos = pos0 + (i * n_sub + s) * tt + t_idx
        groups_per_tile = half // POOL_GROUP_DIM
        deltas = []
        for g in range(j * groups_per_tile, (j + 1) * groups_per_tile):
            win = POOL_WINDOWS[g]
            lanes = slice(g * POOL_GROUP_DIM, (g + 1) * POOL_GROUP_DIM)
            w_sum = pool_ext[t0 * bb:(t0 + POOL_HIST + tt) * bb, lanes]
            span = 1
            while span < win:
                w_sum = w_sum[span * bb:, :] + w_sum[:-span * bb, :]
                span *= 2
            w_sum = w_sum[w_sum.shape[0] - rows:, :]
            cnt = jnp.minimum(pos + 1, win).astype(F32)
            u_pool = pool_ext[(t0 + POOL_HIST) * bb:(t0 + POOL_HIST + tt) * bb, lanes]
            deltas.append(w_sum / cnt - u_pool)
        delta = jnp.concatenate(deltas, axis=1).astype(BF16)
        lanes = slice(j * half, (j + 1) * half)
        y_scr[s, :, LRU_WIDTH + j * half:LRU_WIDTH + (j + 1) * half] = (
            ps_ref[:, lanes] * _dot(delta, pw_ref[j])).astype(BF16)

    def mix_pieces(s):
        return [functools.partial(piece, s, j) for j in range(LRU_WIDTH // half) for piece in (lru_piece, pool_piece)]

    def project_out_pieces(s):
        def piece(n):
            cols = slice(n * half, (n + 1) * half)
            out = x_scr[s, :, cols] + _dot(y_scr[s], wout_ref[:, cols])
            o_ref[:, s * tt:(s + 1) * tt, cols] = jnp.swapaxes(out.reshape(tt, bb, half), 0, 1)
        return [functools.partial(piece, n) for n in range(D_MODEL // half)]

    def emit(*streams):
        for pieces in itertools.zip_longest(*streams):
            for piece in pieces:
                if piece is not None:
                    piece()

    emit(prepare_pieces(0))
    for k in range(1, n_sub + 3):
        mxu, valu = [], []
        if k - 1 < n_sub:
            mxu += project_in_pieces(k - 1)
        if 0 <= k - 3 < n_sub:
            mxu += project_out_pieces(k - 3)
        if k < n_sub:
            valu += prepare_pieces(k)
        if 0 <= k - 2 < n_sub:
            valu += mix_pieces(k - 2)
        emit(mxu, valu)

    new_conv = lru_ext[n_sub * tt * bb:(n_sub * tt + CONV_HIST) * bb, :]
    oconv_ref[...] = new_conv.reshape(CONV_HIST, bb, LRU_WIDTH)
    lru_ext[0:CONV_HIST * bb, :] = new_conv
    new_pool = pool_ext[n_sub * tt * bb:(n_sub * tt + POOL_HIST) * bb, :]
    opool_ref[...] = new_pool.reshape(POOL_HIST, bb, POOL_WIDTH)
    pool_ext[0:POOL_HIST * bb, :] = new_pool
    olru_ref[...] = h_scr[...]


def _mix(x, w, sconv, slru, spool, *, tt, bb, n_sub, pos0):
    n_b, n_t = x.shape[0], x.shape[1]
    rows = tt * bb
    x_spec = pl.BlockSpec((bb, n_sub * tt, D_MODEL), lambda j, i: (j, i, 0))
    conv_spec = pl.BlockSpec((CONV_HIST, bb, LRU_WIDTH), lambda j, i: (0, j, 0))
    lru_spec = pl.BlockSpec((bb, LRU_WIDTH), lambda j, i: (j, 0))
    pool_spec = pl.BlockSpec((POOL_HIST, bb, POOL_WIDTH), lambda j, i: (0, j, 0))
    return pl.pallas_call(
        functools.partial(_mix_kernel, tt=tt, bb=bb, n_sub=n_sub, pos0=pos0),
        grid=(n_b // bb, n_t // (n_sub * tt)),
        in_specs=[x_spec, _const_spec((1, D_MODEL)), _const_spec((D_MODEL, 3 * LRU_WIDTH)),
                  _const_spec((CONV_W, LRU_WIDTH)), _const_spec((1, LRU_WIDTH)),
                  _const_spec((2, MXU_TILE, MXU_TILE)), _const_spec((1, LRU_WIDTH)),
                  _const_spec((2, MXU_TILE, MXU_TILE)), _const_spec((1, LRU_WIDTH)),
                  _const_spec((1, LRU_WIDTH)),
                  _const_spec((2, MXU_TILE, MXU_TILE)), _const_spec((1, POOL_WIDTH)),
                  _const_spec((D_MODEL, D_MODEL)),
                  conv_spec, lru_spec, pool_spec],
        out_specs=[x_spec, conv_spec, lru_spec, pool_spec],
        out_shape=[jax.ShapeDtypeStruct(x.shape, F32),
                   jax.ShapeDtypeStruct((CONV_HIST, n_b, LRU_WIDTH), F32),
                   jax.ShapeDtypeStruct((n_b, LRU_WIDTH), F32),
                   jax.ShapeDtypeStruct((POOL_HIST, n_b, POOL_WIDTH), F32)],
        scratch_shapes=[pltpu.VMEM(((CONV_HIST + n_sub * tt) * bb, LRU_WIDTH), F32),
                        pltpu.VMEM(((POOL_HIST + n_sub * tt) * bb, POOL_WIDTH), F32),
                        pltpu.VMEM((bb, LRU_WIDTH), F32),
                        pltpu.VMEM((n_sub, rows, D_MODEL), F32),
                        pltpu.VMEM((n_sub, rows, D_MODEL), BF16),
                        pltpu.VMEM((n_sub, rows, LRU_WIDTH), F32),
                        pltpu.VMEM((n_sub, rows, LRU_WIDTH), F32),
                        pltpu.VMEM((n_sub, rows, LRU_WIDTH), F32),
                        pltpu.VMEM((n_sub, rows, LRU_WIDTH + POOL_WIDTH), BF16)],
        compiler_params=_params(("parallel", "arbitrary")),
        name="mix",
    )(x, w["mix_norm"], w["w_in"], w["conv_w"], w["conv_b"], w["lru_wa"], w["lru_ba"],
      w["lru_wx"], w["lru_bx"], w["lru_lambda"], w["pool_w"], w["pool_scale"], w["w_out"],
      sconv, slru, spool)


def _memkv_kernel(m_ref, g_ref, wk_ref, wv_ref, k_ref, v_ref, kb_ref, vb_ref):
    m = _rmsnorm(m_ref[0], g_ref[...]).astype(BF16)
    for w_ref, o_ref, ob_ref in ((wk_ref, k_ref, kb_ref), (wv_ref, v_ref, vb_ref)):
        kv = _dot(m, w_ref[...])
        ob_ref[0] = kv.astype(BF16)
        for hd in range(XA_HEADS):
            o_ref[0, :, hd, :] = kv[:, hd * XA_HEAD_DIM:(hd + 1) * XA_HEAD_DIM]


def _memkv(mem, g, wk, wv):
    n_b = mem.shape[0]
    row_spec = pl.BlockSpec((1, MEM_LEN, D_MODEL), lambda i: (i, 0, 0))
    head_spec = pl.BlockSpec((1, MEM_LEN, XA_HEADS, XA_HEAD_DIM), lambda i: (i, 0, 0, 0))
    return pl.pallas_call(
        _memkv_kernel,
        grid=(n_b,),
        in_specs=[row_spec, _const_spec((1, D_MODEL)), _const_spec((D_MODEL, D_MODEL)),
                  _const_spec((D_MODEL, D_MODEL))],
        out_specs=[head_spec, head_spec, row_spec, row_spec],
        out_shape=[jax.ShapeDtypeStruct((n_b, MEM_LEN, XA_HEADS, XA_HEAD_DIM), F32)] * 2
        + [jax.ShapeDtypeStruct((n_b, MEM_LEN, D_MODEL), BF16)] * 2,
        compiler_params=_params(("parallel",)),
        name="memkv",
    )(mem, g, wk, wv)


def _xattn_prompt_kernel(x_ref, g_ref, wq_ref, k_ref, v_ref, wo_ref, o_ref, q_scr, s_scr, att_scr):
    subs = range(SUB_TILES)
    rows = [slice(s * ROW_TILE, (s + 1) * ROW_TILE) for s in subs]
    heads = [slice(hd * XA_HEAD_DIM, (hd + 1) * XA_HEAD_DIM) for hd in range(XA_HEADS)]

    def project_q(s):
        q_scr[s] = _dot(_rmsnorm(x_ref[rows[s], :], g_ref[...]).astype(BF16), wq_ref[...]).astype(BF16)

    def scores(s):
        for hd, cols in enumerate(heads):
            s_scr[s, hd] = lax.dot_general(q_scr[s, :, cols], k_ref[:, cols], (((1,), (1,)), ((), ())),
                                           preferred_element_type=F32) * (XA_HEAD_DIM ** -0.5)

    def attend(s):
        sc = s_scr[s]
        e = jnp.exp(sc - jnp.max(sc, axis=-1, keepdims=True))
        s_scr[s] = e / jnp.sum(e, axis=-1, keepdims=True)
        for hd, cols in enumerate(heads):
            att_scr[s, :, cols] = _dot(s_scr[s, hd].astype(BF16), v_ref[:, cols]).astype(BF16)

    def project_out(s):
        o_ref[rows[s], :] = x_ref[rows[s], :] + _dot(att_scr[s], wo_ref[...])

    stages = (project_q, scores, attend, project_out)
    for k in range(SUB_TILES + len(stages) - 1):
        for depth in reversed(range(len(stages))):
            if 0 <= k - depth < SUB_TILES:
                stages[depth](k - depth)


def _xattn_prompt(x, g, wq, k, v, wo):
    n_b, n_t = x.shape[0], x.shape[1]
    tile = SUB_TILES * ROW_TILE
    x_spec = pl.BlockSpec((None, tile, D_MODEL), lambda b, i: (b, i, 0))
    kv_spec = pl.BlockSpec((None, MEM_LEN, D_MODEL), lambda b, i: (b, 0, 0))
    return pl.pallas_call(
        _xattn_prompt_kernel,
        grid=(n_b, n_t // tile),
        in_specs=[x_spec, _const_spec((1, D_MODEL)), _const_spec((D_MODEL, D_MODEL)),
                  kv_spec, kv_spec, _const_spec((D_MODEL, D_MODEL))],
        out_specs=x_spec,
        out_shape=jax.ShapeDtypeStruct(x.shape, F32),
        scratch_shapes=[pltpu.VMEM((SUB_TILES, ROW_TILE, D_MODEL), BF16),
                        pltpu.VMEM((SUB_TILES, XA_HEADS, ROW_TILE, MEM_LEN), F32),
                        pltpu.VMEM((SUB_TILES, ROW_TILE, D_MODEL), BF16)],
        compiler_params=_params(("parallel", "parallel")),
        name="xattn_prompt",
    )(x, g, wq, k, v, wo)


def _xattn_sample_kernel(x_ref, g_ref, wq_ref, k0_ref, k1_ref, v0_ref, v1_ref, wo_ref, o_ref, att_scr, s_scr,
                         *, bc, tq):
    x = x_ref[...]
    q = _dot(_rmsnorm(x, g_ref[...]).astype(BF16), wq_ref[...])
    halves = [[r.reshape(bc * MEM_LEN * XA_HEADS, LANES) for r in pair]
              for pair in ((k0_ref, k1_ref), (v0_ref, v1_ref))]

    def head(pair, b, hd):
        rows = pl.ds(b * MEM_LEN * XA_HEADS + hd, MEM_LEN, stride=XA_HEADS)
        return jnp.concatenate([r[rows, :] for r in pair], axis=1).astype(BF16)

    pairs = [(b, hd) for b in range(bc) for hd in range(XA_HEADS)]
    for n, (b, hd) in enumerate(pairs):
        qh = q[b * tq:(b + 1) * tq, hd * XA_HEAD_DIM:(hd + 1) * XA_HEAD_DIM].astype(BF16)
        s_scr[n * tq:(n + 1) * tq, :] = lax.dot_general(
            qh, head(halves[0], b, hd), (((1,), (1,)), ((), ())),
            preferred_element_type=F32) * (XA_HEAD_DIM ** -0.5)
    s = s_scr[...]
    e = jnp.exp(s - jnp.max(s, axis=-1, keepdims=True))
    s_scr[...] = e / jnp.sum(e, axis=-1, keepdims=True)
    for n, (b, hd) in enumerate(pairs):
        p = s_scr[n * tq:(n + 1) * tq, :].astype(BF16)
        att_scr[b * tq:(b + 1) * tq, hd * XA_HEAD_DIM:(hd + 1) * XA_HEAD_DIM] = _dot(p, head(halves[1], b, hd))
    o_ref[...] = x + _dot(att_scr[...].astype(BF16), wo_ref[...])


def _xattn_sample(x, g, wq, k, v, wo, *, bc):
    n_b = k.shape[0]
    tq = x.shape[0] // n_b
    x_spec = pl.BlockSpec((bc * tq, D_MODEL), lambda j: (j, 0))
    kv_specs = [pl.BlockSpec((bc, MEM_LEN, XA_HEADS, LANES), functools.partial(lambda half, j: (j, 0, 0, half), half))
                for half in range(XA_HEAD_DIM // LANES)]
    return pl.pallas_call(
        functools.partial(_xattn_sample_kernel, bc=bc, tq=tq),
        grid=(n_b // bc,),
        in_specs=[x_spec, _const_spec((1, D_MODEL)), _const_spec((D_MODEL, D_MODEL)),
                  *kv_specs, *kv_specs, _const_spec((D_MODEL, D_MODEL))],
        out_specs=x_spec,
        out_shape=jax.ShapeDtypeStruct(x.shape, F32),
        scratch_shapes=[pltpu.VMEM((bc * tq, D_MODEL), F32),
                        pltpu.VMEM((bc * XA_HEADS * tq, MEM_LEN), F32)],
        compiler_params=_params(("parallel",)),
        name="xattn_sample",
    )(x, g, wq, k, k, v, v, wo)


def _block_diag_tiles(w):
    groups, dg, _ = w.shape
    per_tile = MXU_TILE // dg
    w = w.reshape(groups // per_tile, per_tile, dg, dg)
    eye = jnp.eye(per_tile, dtype=w.dtype)
    tiles = jnp.einsum("npij,pq->npiqj", w, eye)
    return tiles.reshape(groups // per_tile, MXU_TILE, MXU_TILE).astype(BF16)


def _row(v):
    return v.reshape(1, -1).astype(F32)


def _layer(x_prompt, x_sample, mem_prompt, sconv, slru, spool, cache_k, cache_v, p, final_norm):
    n_b, n_t, _ = x_prompt.shape
    s_b, s_t, _ = x_sample.shape
    ffn1 = (_row(p["ffn1_norm"]), p["ffn1_w_gate"], p["ffn1_w_up"], p["ffn1_w_down"], _row(final_norm))
    ffn2 = (_row(p["ffn2_norm"]), p["ffn2_w_gate"], p["ffn2_w_up"], p["ffn2_w_down"], _row(final_norm))
    mixw = {
        "mix_norm": _row(p["mix_norm"]), "w_in": p["w_in"].astype(BF16),
        "conv_w": p["conv_w"].astype(F32), "conv_b": _row(p["conv_b"]),
        "lru_wa": _block_diag_tiles(p["lru_wa"]), "lru_ba": _row(p["lru_ba"]),
        "lru_wx": _block_diag_tiles(p["lru_wx"]), "lru_bx": _row(p["lru_bx"]),
        "lru_lambda": _row(p["lru_lambda"]),
        "pool_w": _block_diag_tiles(p["pool_w"]), "pool_scale": _row(p["pool_scale"]),
        "w_out": p["w_out"].astype(BF16),
    }
    gx, wq, wo = _row(p["xattn_norm"]), p["xattn_wq"].astype(BF16), p["xattn_wo"].astype(BF16)

    mk, mv, mk_bf, mv_bf = _memkv(mem_prompt, _row(p["mem_norm"]),
                                  p["xattn_wk"].astype(BF16), p["xattn_wv"].astype(BF16))

    xp = _ffn(x_prompt.reshape(n_b * n_t, D_MODEL), *ffn1, final_norm=False)
    xs = _ffn(x_sample.reshape(s_b * s_t, D_MODEL), *ffn1, final_norm=False)

    xp, p_conv, p_lru, p_pool = _mix(
        xp.reshape(n_b, n_t, D_MODEL), mixw, jnp.zeros((CONV_HIST, n_b, LRU_WIDTH), F32),
        jnp.zeros((n_b, LRU_WIDTH), F32), jnp.zeros((POOL_HIST, n_b, POOL_WIDTH), F32),
        tt=ROW_TILE // n_b, bb=n_b, n_sub=SUB_TILES, pos0=0)
    xp = _xattn_prompt(xp, gx, wq, mk_bf, mv_bf, wo)

    xs, s_conv, s_lru, s_pool = _mix(
        xs.reshape(s_b, s_t, D_MODEL), mixw, jnp.swapaxes(sconv, 0, 1), slru, jnp.swapaxes(spool, 0, 1),
        tt=s_t, bb=ROW_TILE // s_t, n_sub=1, pos0=PAST_LEN)
    xs = _xattn_sample(xs.reshape(s_b * s_t, D_MODEL), gx, wq, cache_k, cache_v, wo, bc=8)

    yp = _ffn(xp.reshape(n_b * n_t, D_MODEL), *ffn2, final_norm=True).reshape(n_b, n_t, D_MODEL)
    ys = _ffn(xs, *ffn2, final_norm=True).reshape(s_b, s_t, D_MODEL)

    states = (jnp.swapaxes(p_conv, 0, 1), p_lru, jnp.swapaxes(p_pool, 0, 1), mk, mv,
              jnp.swapaxes(s_conv, 0, 1), s_lru, jnp.swapaxes(s_pool, 0, 1))
    return yp, ys, states


def kernel(x_prompt, x_sample, mem_prompt, state_conv, state_lru, state_pool, cache_mem_k, cache_mem_v, ffn1_norm, ffn1_w_gate, ffn1_w_up, ffn1_w_down, mix_norm, w_in, conv_w, conv_b, lru_wa, lru_ba, lru_wx, lru_bx, lru_lambda, pool_w, pool_scale, w_out, xattn_norm, mem_norm, xattn_wq, xattn_wk, xattn_wv, xattn_wo, ffn2_norm, ffn2_w_gate, ffn2_w_up, ffn2_w_down, final_norm):
    depth = ffn1_norm.shape[0]
    assert depth == 1, "the final RMSNorm is fused into the last layer's second FFN"
    names = ("ffn1_norm", "ffn1_w_gate", "ffn1_w_up", "ffn1_w_down", "mix_norm", "w_in", "conv_w", "conv_b",
             "lru_wa", "lru_ba", "lru_wx", "lru_bx", "lru_lambda", "pool_w", "pool_scale", "w_out",
             "xattn_norm", "mem_norm", "xattn_wq", "xattn_wk", "xattn_wv", "xattn_wo",
             "ffn2_norm", "ffn2_w_gate", "ffn2_w_up", "ffn2_w_down")
    stacked = (ffn1_norm, ffn1_w_gate, ffn1_w_up, ffn1_w_down, mix_norm, w_in, conv_w, conv_b,
               lru_wa, lru_ba, lru_wx, lru_bx, lru_lambda, pool_w, pool_scale, w_out,
               xattn_norm, mem_norm, xattn_wq, xattn_wk, xattn_wv, xattn_wo,
               ffn2_norm, ffn2_w_gate, ffn2_w_up, ffn2_w_down)
    p = {n: a[0] for n, a in zip(names, stacked)}
    yp, ys, st = _layer(x_prompt, x_sample, mem_prompt, state_conv[0], state_lru[0], state_pool[0],
                        cache_mem_k[0], cache_mem_v[0], p, final_norm)
    p_conv, p_lru, p_pool, p_mk, p_mv, s_conv, s_lru, s_pool = (s[None] for s in st)
    return (yp, ys, p_conv, p_lru, p_pool, p_mk, p_mv, s_conv, s_lru, s_pool)
```

```python
import functools
import itertools

import jax
import jax.numpy as jnp
from jax import lax
from jax.experimental import pallas as pl
from jax.experimental.pallas import tpu as pltpu

F32 = jnp.float32
BF16 = jnp.bfloat16

D_MODEL = 1024
D_FF = 2816
LRU_WIDTH = 512
LRU_HEADS = 8
LRU_C = 8.0
CONV_W = 4
POOL_WIDTH = 512
POOL_WINDOWS = (2, 4, 8, 16)
POOL_GROUP_DIM = POOL_WIDTH // len(POOL_WINDOWS)
POOL_HIST = max(POOL_WINDOWS) - 1
CONV_HIST = CONV_W - 1
MEM_LEN = 256
XA_HEADS = 4
XA_HEAD_DIM = D_MODEL // XA_HEADS
EPS = 1e-6
PAST_LEN = 16384

MXU_TILE = 256
LANES = 128
assert XA_HEAD_DIM == 2 * LANES
FF_CHUNK = MXU_TILE
N_FF_CHUNKS = D_FF // FF_CHUNK
ROW_TILE = 512
SUB_TILES = 2
WEIGHT_LOOKAHEAD = 4
COL_STAGE_SLOTS = 4
ROW_STAGE_SLOTS = 2
VMEM_LIMIT_BYTES = 56 * 1024 * 1024


def _rmsnorm(x, g):
    var = jnp.mean(x * x, axis=-1, keepdims=True)
    return x * lax.rsqrt(var + EPS) * g


def _dot(a, b):
    return jnp.dot(a, b, preferred_element_type=F32)


def _const_spec(shape):
    zeros = (0,) * len(shape)
    return pl.BlockSpec(shape, lambda *_: zeros, pipeline_mode=pl.Buffered(1))


def _params(semantics):
    return pltpu.CompilerParams(dimension_semantics=semantics, vmem_limit_bytes=VMEM_LIMIT_BYTES)


def _ffn_weight_fetcher(wg_hbm, wu_hbm, wd_hbm, wg_scr, wu_scr, wd_scr, col_stage, row_stage, sems):
    stages = (col_stage, row_stage)

    def chunk(c):
        cols = pl.ds(c * FF_CHUNK, FF_CHUNK)
        return {"gate": (wg_hbm.at[:, cols], 0, wg_scr.at[c]), "up": (wu_hbm.at[:, cols], 0, wu_scr.at[c]),
                "down": (wd_hbm.at[cols, :], 1, wd_scr.at[c])}

    items = [chunk(0)["gate"], chunk(0)["up"]]
    for c in range(1, N_FF_CHUNKS):
        items += [chunk(c)["gate"], chunk(c)["up"], chunk(c - 1)["down"]]
    items.append(chunk(N_FF_CHUNKS - 1)["down"])
    uses = [0, 0]
    plan = []
    for src, kind, dst in items:
        slot = uses[kind] % stages[kind].shape[0]
        uses[kind] += 1
        staged = stages[kind].at[slot]
        plan.append((pltpu.make_async_copy(src, staged, sems.at[kind, slot]), staged, dst, (kind, slot)))
    slot_free_after = {}
    state = {"next": 0}

    def start(m):
        key = plan[m][3]
        assert slot_free_after.get(key, -1) < state["next"], "staging slot reused before it was converted"
        slot_free_after[key] = m
        plan[m][0].start()

    for m in range(min(WEIGHT_LOOKAHEAD, len(plan))):
        start(m)

    def fetch():
        n = state["next"]
        copy, staged, dst, _ = plan[n]
        copy.wait()
        dst[...] = staged[...].astype(BF16)
        state["next"] = n + 1
        if n + WEIGHT_LOOKAHEAD < len(plan):
            start(n + WEIGHT_LOOKAHEAD)

    return fetch, len(plan)


def _ffn_kernel(x_ref, g_ref, wg_hbm, wu_hbm, wd_hbm, fn_ref, o_ref,
                wg_scr, wu_scr, wd_scr, col_stage, row_stage, sems, h_scr, a_scr, acc_scr, *, final_norm):
    subs = range(SUB_TILES)
    rows = [slice(s * ROW_TILE, (s + 1) * ROW_TILE) for s in subs]

    def gate_up(s, c):
        h = h_scr[s]
        g = _dot(h, wg_scr[c])
        u = _dot(h, wu_scr[c])
        a_scr[s, c % 2] = (g * jax.nn.sigmoid(g) * u).astype(BF16)

    def down(s, c):
        d = _dot(a_scr[s, c % 2], wd_scr[c])
        if c == 0:
            acc_scr[s] = d
        else:
            acc_scr[s] += d

    def body(load_weights):
        if load_weights:
            fetch, n_items = _ffn_weight_fetcher(wg_hbm, wu_hbm, wd_hbm, wg_scr, wu_scr, wd_scr,
                                                 col_stage, row_stage, sems)
        else:
            fetch, n_items = (lambda: None), 0
        for s in subs:
            h_scr[s] = _rmsnorm(x_ref[rows[s], :], g_ref[...]).astype(BF16)
        fetch()
        fetch()
        for s in subs:
            gate_up(s, 0)
        for c in range(1, N_FF_CHUNKS):
            fetch()
            fetch()
            fetch()
            for s in subs:
                gate_up(s, c)
                down(s, c - 1)
        fetch()
        assert n_items in (0, 3 * N_FF_CHUNKS)
        for s in subs:
            down(s, N_FF_CHUNKS - 1)
        for s in subs:
            y = x_ref[rows[s], :] + 0.5 * acc_scr[s]
            if final_norm:
                y = _rmsnorm(y, fn_ref[...])
            o_ref[rows[s], :] = y

    first = pl.program_id(0) == 0
    pl.when(first)(functools.partial(body, True))
    pl.when(jnp.logical_not(first))(functools.partial(body, False))


def _ffn(x, g, wg, wu, wd, fn, *, final_norm):
    row_spec = pl.BlockSpec((SUB_TILES * ROW_TILE, D_MODEL), lambda i: (i, 0))
    hbm = pl.BlockSpec(memory_space=pl.ANY)
    return pl.pallas_call(
        functools.partial(_ffn_kernel, final_norm=final_norm),
        grid=(x.shape[0] // (SUB_TILES * ROW_TILE),),
        in_specs=[row_spec, _const_spec((1, D_MODEL)), hbm, hbm, hbm, _const_spec((1, D_MODEL))],
        out_specs=row_spec,
        out_shape=jax.ShapeDtypeStruct(x.shape, F32),
        scratch_shapes=[pltpu.VMEM((N_FF_CHUNKS, D_MODEL, FF_CHUNK), BF16),
                        pltpu.VMEM((N_FF_CHUNKS, D_MODEL, FF_CHUNK), BF16),
                        pltpu.VMEM((N_FF_CHUNKS, FF_CHUNK, D_MODEL), BF16),
                        pltpu.VMEM((COL_STAGE_SLOTS, D_MODEL, FF_CHUNK), F32),
                        pltpu.VMEM((ROW_STAGE_SLOTS, FF_CHUNK, D_MODEL), F32),
                        pltpu.SemaphoreType.DMA((2, COL_STAGE_SLOTS)),
                        pltpu.VMEM((SUB_TILES, ROW_TILE, D_MODEL), BF16),
                        pltpu.VMEM((SUB_TILES, 2, ROW_TILE, FF_CHUNK), BF16),
                        pltpu.VMEM((SUB_TILES, ROW_TILE, D_MODEL), F32)],
        compiler_params=_params(("arbitrary",)),
        name="ffn_final" if final_norm else "ffn",
    )(x, g, wg, wu, wd, fn)


def _mix_kernel(x_ref, gm_ref, win_ref, cw_ref, cb_ref, wa_ref, ba_ref, wx_ref, bx_ref, lam_ref,
                pw_ref, ps_ref, wout_ref, sconv_ref, slru_ref, spool_ref,
                o_ref, oconv_ref, olru_ref, opool_ref,
                lru_ext, pool_ext, h_scr, x_scr, hn_scr, gate_scr, a_scr, b_scr, y_scr, *, tt, bb, n_sub, pos0):
    rows = tt * bb
    i = pl.program_id(1)

    @pl.when(i == 0)
    def _():
        lru_ext[0:CONV_HIST * bb, :] = sconv_ref[...].reshape(CONV_HIST * bb, LRU_WIDTH)
        pool_ext[0:POOL_HIST * bb, :] = spool_ref[...].reshape(POOL_HIST * bb, POOL_WIDTH)
        h_scr[...] = slru_ref[...]

    half = MXU_TILE

    def prepare_pieces(s):
        n_pieces = max(1, min(4, tt // 8))
        tq = tt // n_pieces

        def piece(q):
            x = jnp.swapaxes(x_ref[:, s * tt + q * tq:s * tt + (q + 1) * tq, :], 0, 1)
            x = x.reshape(tq * bb, D_MODEL)
            x_scr[s, q * tq * bb:(q + 1) * tq * bb, :] = x
            hn_scr[s, q * tq * bb:(q + 1) * tq * bb, :] = _rmsnorm(x, gm_ref[...]).astype(BF16)
        return [functools.partial(piece, q) for q in range(n_pieces)]

    def project_in_pieces(s):
        lru_rows = slice((CONV_HIST + s * tt) * bb, (CONV_HIST + (s + 1) * tt) * bb)
        pool_rows = slice((POOL_HIST + s * tt) * bb, (POOL_HIST + (s + 1) * tt) * bb)

        def piece(n):
            p = _dot(hn_scr[s], win_ref[:, n * half:(n + 1) * half])
            cols = slice((n % 2) * half, (n % 2 + 1) * half)
            if n < 2:
                lru_ext[lru_rows, cols] = p
            elif n < 4:
                gate_scr[s, :, cols] = p
            else:
                pool_ext[pool_rows, cols] = p
        return [functools.partial(piece, n) for n in range(3 * LRU_WIDTH // half)]

    def lru_piece(s, j):
        t0 = s * tt
        lanes = slice(j * half, (j + 1) * half)
        cw = cw_ref[:, lanes]
        u = cb_ref[:, lanes] + sum(lru_ext[(t0 + k) * bb:(t0 + k) * bb + rows, lanes] * cw[k:k + 1, :]
                                   for k in range(CONV_W))
        ub = u.astype(BF16)
        r = jax.nn.sigmoid(_dot(ub, wa_ref[j]) + ba_ref[:, lanes])
        ig = jax.nn.sigmoid(_dot(ub, wx_ref[j]) + bx_ref[:, lanes])
        neg_lam = -lam_ref[:, lanes]
        softplus = jnp.maximum(neg_lam, 0.0) + jnp.log1p(jnp.exp(-jnp.abs(neg_lam)))
        log_a = (-LRU_C * softplus) * r
        a = jnp.exp(log_a)
        a_scr[s, :, lanes] = a
        m = jnp.maximum(1.0 - a * a, 0.0)
        b_scr[s, :, lanes] = jnp.where(m > 0.0, m * lax.rsqrt(m), 0.0) * ig * u
        hcur = h_scr[:, lanes]
        for t in range(tt):
            sl = slice(t * bb, (t + 1) * bb)
            hcur = a_scr[s, sl, lanes] * hcur + b_scr[s, sl, lanes]
            b_scr[s, sl, lanes] = hcur
        h_scr[:, lanes] = hcur
        y_scr[s, :, lanes] = (jax.nn.gelu(gate_scr[s, :, lanes]) * b_scr[s, :, lanes]).astype(BF16)

    def pool_piece(s, j):
        t0 = s * tt
        t_idx = lax.broadcasted_iota(jnp.int32, (rows, POOL_GROUP_DIM), 0) // bb
        pos = pos0 + (i * n_sub + s) * tt + t_idx
        groups_per_tile = half // POOL_GROUP_DIM
        deltas = []
        for g in range(j * groups_per_tile, (j + 1) * groups_per_tile):
            win = POOL_WINDOWS[g]
            lanes = slice(g * POOL_GROUP_DIM, (g + 1) * POOL_GROUP_DIM)
            w_sum = pool_ext[t0 * bb:(t0 + POOL_HIST + tt) * bb, lanes]
            span = 1
            while span < win:
                w_sum = w_sum[span * bb:, :] + w_sum[:-span * bb, :]
                span *= 2
            w_sum = w_sum[w_sum.shape[0] - rows:, :]
            cnt = jnp.minimum(pos + 1, win).astype(F32)
            u_pool = pool_ext[(t0 + POOL_HIST) * bb:(t0 + POOL_HIST + tt) * bb, lanes]
            deltas.append(w_sum / cnt - u_pool)
        delta = jnp.concatenate(deltas, axis=1).astype(BF16)
        lanes = slice(j * half, (j + 1) * half)
        y_scr[s, :, LRU_WIDTH + j * half:LRU_WIDTH + (j + 1) * half] = (
            ps_ref[:, lanes] * _dot(delta, pw_ref[j])).astype(BF16)

    def mix_pieces(s):
        return [functools.partial(piece, s, j) for j in range(LRU_WIDTH // half) for piece in (lru_piece, pool_piece)]

    def project_out_pieces(s):
        def piece(n):
            cols = slice(n * half, (n + 1) * half)
            out = x_scr[s, :, cols] + _dot(y_scr[s], wout_ref[:, cols])
            o_ref[:, s * tt:(s + 1) * tt, cols] = jnp.swapaxes(out.reshape(tt, bb, half), 0, 1)
        return [functools.partial(piece, n) for n in range(D_MODEL // half)]

    def emit(*streams):
        for pieces in itertools.zip_longest(*streams):
            for piece in pieces:
                if piece is not None:
                    piece()

    emit(prepare_pieces(0))
    for k in range(1, n_sub + 3):
        mxu, valu = [], []
        if k - 1 < n_sub:
            mxu += project_in_pieces(k - 1)
        if 0 <= k - 3 < n_sub:
            mxu += project_out_pieces(k - 3)
        if k < n_sub:
            valu += prepare_pieces(k)
        if 0 <= k - 2 < n_sub:
            valu += mix_pieces(k - 2)
        emit(mxu, valu)

    new_conv = lru_ext[n_sub * tt * bb:(n_sub * tt + CONV_HIST) * bb, :]
    oconv_ref[...] = new_conv.reshape(CONV_HIST, bb, LRU_WIDTH)
    lru_ext[0:CONV_HIST * bb, :] = new_conv
    new_pool = pool_ext[n_sub * tt * bb:(n_sub * tt + POOL_HIST) * bb, :]
    opool_ref[...] = new_pool.reshape(POOL_HIST, bb, POOL_WIDTH)
    pool_ext[0:POOL_HIST * bb, :] = new_pool
    olru_ref[...] = h_scr[...]


def _mix(x, w, sconv, slru, spool, *, tt, bb, n_sub, pos0):
    n_b, n_t = x.shape[0], x.shape[1]
    rows = tt * bb
    x_spec = pl.BlockSpec((bb, n_sub * tt, D_MODEL), lambda j, i: (j, i, 0))
    conv_spec = pl.BlockSpec((CONV_HIST, bb, LRU_WIDTH), lambda j, i: (0, j, 0))
    lru_spec = pl.BlockSpec((bb, LRU_WIDTH), lambda j, i: (j, 0))
    pool_spec = pl.BlockSpec((POOL_HIST, bb, POOL_WIDTH), lambda j, i: (0, j, 0))
    return pl.pallas_call(
        functools.partial(_mix_kernel, tt=tt, bb=bb, n_sub=n_sub, pos0=pos0),
        grid=(n_b // bb, n_t // (n_sub * tt)),
        in_specs=[x_spec, _const_spec((1, D_MODEL)), _const_spec((D_MODEL, 3 * LRU_WIDTH)),
                  _const_spec((CONV_W, LRU_WIDTH)), _const_spec((1, LRU_WIDTH)),
                  _const_spec((2, MXU_TILE, MXU_TILE)), _const_spec((1, LRU_WIDTH)),
                  _const_spec((2, MXU_TILE, MXU_TILE)), _const_spec((1, LRU_WIDTH)),
                  _const_spec((1, LRU_WIDTH)),
                  _const_spec((2, MXU_TILE, MXU_TILE)), _const_spec((1, POOL_WIDTH)),
                  _const_spec((D_MODEL, D_MODEL)),
                  conv_spec, lru_spec, pool_spec],
        out_specs=[x_spec, conv_spec, lru_spec, pool_spec],
        out_shape=[jax.ShapeDtypeStruct(x.shape, F32),
                   jax.ShapeDtypeStruct((CONV_HIST, n_b, LRU_WIDTH), F32),
                   jax.ShapeDtypeStruct((n_b, LRU_WIDTH), F32),
                   jax.ShapeDtypeStruct((POOL_HIST, n_b, POOL_WIDTH), F32)],
        scratch_shapes=[pltpu.VMEM(((CONV_HIST + n_sub * tt) * bb, LRU_WIDTH), F32),
                        pltpu.VMEM(((POOL_HIST + n_sub * tt) * bb, POOL_WIDTH), F32),
                        pltpu.VMEM((bb, LRU_WIDTH), F32),
                        pltpu.VMEM((n_sub, rows, D_MODEL), F32),
                        pltpu.VMEM((n_sub, rows, D_MODEL), BF16),
                        pltpu.VMEM((n_sub, rows, LRU_WIDTH), F32),
                        pltpu.VMEM((n_sub, rows, LRU_WIDTH), F32),
                        pltpu.VMEM((n_sub, rows, LRU_WIDTH), F32),
                        pltpu.VMEM((n_sub, rows, LRU_WIDTH + POOL_WIDTH), BF16)],
        compiler_params=_params(("parallel", "arbitrary")),
        name="mix",
    )(x, w["mix_norm"], w["w_in"], w["conv_w"], w["conv_b"], w["lru_wa"], w["lru_ba"],
      w["lru_wx"], w["lru_bx"], w["lru_lambda"], w["pool_w"], w["pool_scale"], w["w_out"],
      sconv, slru, spool)


def _memkv_kernel(m_ref, g_ref, wk_ref, wv_ref, k_ref, v_ref, kb_ref, vb_ref):
    m = _rmsnorm(m_ref[0], g_ref[...]).astype(BF16)
    for w_ref, o_ref, ob_ref in ((wk_ref, k_ref, kb_ref), (wv_ref, v_ref, vb_ref)):
        kv = _dot(m, w_ref[...])
        ob_ref[0] = kv.astype(BF16)
        for hd in range(XA_HEADS):
            o_ref[0, :, hd, :] = kv[:, hd * XA_HEAD_DIM:(hd + 1) * XA_HEAD_DIM]


def _memkv(mem, g, wk, wv):
    n_b = mem.shape[0]
    row_spec = pl.BlockSpec((1, MEM_LEN, D_MODEL), lambda i: (i, 0, 0))
    head_spec = pl.BlockSpec((1, MEM_LEN, XA_HEADS, XA_HEAD_DIM), lambda i: (i, 0, 0, 0))
    return pl.pallas_call(
        _memkv_kernel,
        grid=(n_b,),
        in_specs=[row_spec, _const_spec((1, D_MODEL)), _const_spec((D_MODEL, D_MODEL)),
                  _const_spec((D_MODEL, D_MODEL))],
        out_specs=[head_spec, head_spec, row_spec, row_spec],
        out_shape=[jax.ShapeDtypeStruct((n_b, MEM_LEN, XA_HEADS, XA_HEAD_DIM), F32)] * 2
        + [jax.ShapeDtypeStruct((n_b, MEM_LEN, D_MODEL), BF16)] * 2,
        compiler_params=_params(("parallel",)),
        name="memkv",
    )(mem, g, wk, wv)


def _xattn_prompt_kernel(x_ref, g_ref, wq_ref, k_ref, v_ref, wo_ref, o_ref, q_scr, s_scr, att_scr):
    subs = range(SUB_TILES)
    rows = [slice(s * ROW_TILE, (s + 1) * ROW_TILE) for s in subs]
    heads = [slice(hd * XA_HEAD_DIM, (hd + 1) * XA_HEAD_DIM) for hd in range(XA_HEADS)]

    def project_q(s):
        q_scr[s] = _dot(_rmsnorm(x_ref[rows[s], :], g_ref[...]).astype(BF16), wq_ref[...]).astype(BF16)

    def scores(s):
        for hd, cols in enumerate(heads):
            s_scr[s, hd] = lax.dot_general(q_scr[s, :, cols], k_ref[:, cols], (((1,), (1,)), ((), ())),
                                           preferred_element_type=F32) * (XA_HEAD_DIM ** -0.5)

    def attend(s):
        sc = s_scr[s]
        e = jnp.exp(sc - jnp.max(sc, axis=-1, keepdims=True))
        s_scr[s] = e / jnp.sum(e, axis=-1, keepdims=True)
        for hd, cols in enumerate(heads):
            att_scr[s, :, cols] = _dot(s_scr[s, hd].astype(BF16), v_ref[:, cols]).astype(BF16)

    def project_out(s):
        o_ref[rows[s], :] = x_ref[rows[s], :] + _dot(att_scr[s], wo_ref[...])

    stages = (project_q, scores, attend, project_out)
    for k in range(SUB_TILES + len(stages) - 1):
        for depth in reversed(range(len(stages))):
            if 0 <= k - depth < SUB_TILES:
                stages[depth](k - depth)


def _xattn_prompt(x, g, wq, k, v, wo):
    n_b, n_t = x.shape[0], x.shape[1]
    tile = SUB_TILES * ROW_TILE
    x_spec = pl.BlockSpec((None, tile, D_MODEL), lambda b, i: (b, i, 0))
    kv_spec = pl.BlockSpec((None, MEM_LEN, D_MODEL), lambda b, i: (b, 0, 0))
    return pl.pallas_call(
        _xattn_prompt_kernel,
        grid=(n_b, n_t // tile),
        in_specs=[x_spec, _const_spec((1, D_MODEL)), _const_spec((D_MODEL, D_MODEL)),
                  kv_spec, kv_spec, _const_spec((D_MODEL, D_MODEL))],
        out_specs=x_spec,
        out_shape=jax.ShapeDtypeStruct(x.shape, F32),
        scratch_shapes=[pltpu.VMEM((SUB_TILES, ROW_TILE, D_MODEL), BF16),
                        pltpu.VMEM((SUB_TILES, XA_HEADS, ROW_TILE, MEM_LEN), F32),
                        pltpu.VMEM((SUB_TILES, ROW_TILE, D_MODEL), BF16)],
        compiler_params=_params(("parallel", "parallel")),
        name="xattn_prompt",
    )(x, g, wq, k, v, wo)


def _xattn_sample_kernel(x_ref, g_ref, wq_ref, k0_ref, k1_ref, v0_ref, v1_ref, wo_ref, o_ref, att_scr, s_scr,
                         *, bc, tq):
    x = x_ref[...]
    q = _dot(_rmsnorm(x, g_ref[...]).astype(BF16), wq_ref[...])
    halves = [[r.reshape(bc * MEM_LEN * XA_HEADS, LANES) for r in pair]
              for pair in ((k0_ref, k1_ref), (v0_ref, v1_ref))]

    def head(pair, b, hd):
        rows = pl.ds(b * MEM_LEN * XA_HEADS + hd, MEM_LEN, stride=XA_HEADS)
        return jnp.concatenate([r[rows, :] for r in pair], axis=1).astype(BF16)

    pairs = [(b, hd) for b in range(bc) for hd in range(XA_HEADS)]
    for n, (b, hd) in enumerate(pairs):
        qh = q[b * tq:(b + 1) * tq, hd * XA_HEAD_DIM:(hd + 1) * XA_HEAD_DIM].astype(BF16)
        s_scr[n * tq:(n + 1) * tq, :] = lax.dot_general(
            qh, head(halves[0], b, hd), (((1,), (1,)), ((), ())),
            preferred_element_type=F32) * (XA_HEAD_DIM ** -0.5)
    s = s_scr[...]
    e = jnp.exp(s - jnp.max(s, axis=-1, keepdims=True))
    s_scr[...] = e / jnp.sum(e, axis=-1, keepdims=True)
    for n, (b, hd) in enumerate(pairs):
        p = s_scr[n * tq:(n + 1) * tq, :].astype(BF16)
        att_scr[b * tq:(b + 1) * tq, hd * XA_HEAD_DIM:(hd + 1) * XA_HEAD_DIM] = _dot(p, head(halves[1], b, hd))
    o_ref[...] = x + _dot(att_scr[...].astype(BF16), wo_ref[...])


def _xattn_sample(x, g, wq, k, v, wo, *, bc):
    n_b = k.shape[0]
    tq = x.shape[0] // n_b
    x_spec = pl.BlockSpec((bc * tq, D_MODEL), lambda j: (j, 0))
    kv_specs = [pl.BlockSpec((bc, MEM_LEN, XA_HEADS, LANES), functools.partial(lambda half, j: (j, 0, 0, half), half))
                for half in range(XA_HEAD_DIM // LANES)]
    return pl.pallas_call(
        functools.partial(_xattn_sample_kernel, bc=bc, tq=tq),
        grid=(n_b // bc,),
        in_specs=[x_spec, _const_spec((1, D_MODEL)), _const_spec((D_MODEL, D_MODEL)),
                  *kv_specs, *kv_specs, _const_spec((D_MODEL, D_MODEL))],
        out_specs=x_spec,
        out_shape=jax.ShapeDtypeStruct(x.shape, F32),
        scratch_shapes=[pltpu.VMEM((bc * tq, D_MODEL), F32),
                        pltpu.VMEM((bc * XA_HEADS * tq, MEM_LEN), F32)],
        compiler_params=_params(("parallel",)),
        name="xattn_sample",
    )(x, g, wq, k, k, v, v, wo)


def _block_diag_tiles(w):
    groups, dg, _ = w.shape
    per_tile = MXU_TILE // dg
    w = w.reshape(groups // per_tile, per_tile, dg, dg)
    eye = jnp.eye(per_tile, dtype=w.dtype)
    tiles = jnp.einsum("npij,pq->npiqj", w, eye)
    return tiles.reshape(groups // per_tile, MXU_TILE, MXU_TILE).astype(BF16)


def _row(v):
    return v.reshape(1, -1).astype(F32)


def _layer(x_prompt, x_sample, mem_prompt, sconv, slru, spool, cache_k, cache_v, p, final_norm):
    n_b, n_t, _ = x_prompt.shape
    s_b, s_t, _ = x_sample.shape
    ffn1 = (_row(p["ffn1_norm"]), p["ffn1_w_gate"], p["ffn1_w_up"], p["ffn1_w_down"], _row(final_norm))
    ffn2 = (_row(p["ffn2_norm"]), p["ffn2_w_gate"], p["ffn2_w_up"], p["ffn2_w_down"], _row(final_norm))
    mixw = {
        "mix_norm": _row(p["mix_norm"]), "w_in": p["w_in"].astype(BF16),
        "conv_w": p["conv_w"].astype(F32), "conv_b": _row(p["conv_b"]),
        "lru_wa": _block_diag_tiles(p["lru_wa"]), "lru_ba": _row(p["lru_ba"]),
        "lru_wx": _block_diag_tiles(p["lru_wx"]), "lru_bx": _row(p["lru_bx"]),
        "lru_lambda": _row(p["lru_lambda"]),
        "pool_w": _block_diag_tiles(p["pool_w"]), "pool_scale": _row(p["pool_scale"]),
        "w_out": p["w_out"].astype(BF16),
    }
    gx, wq, wo = _row(p["xattn_norm"]), p["xattn_wq"].astype(BF16), p["xattn_wo"].astype(BF16)

    mk, mv, mk_bf, mv_bf = _memkv(mem_prompt, _row(p["mem_norm"]),
                                  p["xattn_wk"].astype(BF16), p["xattn_wv"].astype(BF16))

    xp = _ffn(x_prompt.reshape(n_b * n_t, D_MODEL), *ffn1, final_norm=False)
    xs = _ffn(x_sample.reshape(s_b * s_t, D_MODEL), *ffn1, final_norm=False)

    xp, p_conv, p_lru, p_pool = _mix(
        xp.reshape(n_b, n_t, D_MODEL), mixw, jnp.zeros((CONV_HIST, n_b, LRU_WIDTH), F32),
        jnp.zeros((n_b, LRU_WIDTH), F32), jnp.zeros((POOL_HIST, n_b, POOL_WIDTH), F32),
        tt=ROW_TILE // n_b, bb=n_b, n_sub=SUB_TILES, pos0=0)
    xp = _xattn_prompt(xp, gx, wq, mk_bf, mv_bf, wo)

    xs, s_conv, s_lru, s_pool = _mix(
        xs.reshape(s_b, s_t, D_MODEL), mixw, jnp.swapaxes(sconv, 0, 1), slru, jnp.swapaxes(spool, 0, 1),
        tt=s_t, bb=ROW_TILE // s_t, n_sub=1, pos0=PAST_LEN)
    xs = _xattn_sample(xs.reshape(s_b * s_t, D_MODEL), gx, wq, cache_k, cache_v, wo, bc=8)

    yp = _ffn(xp.reshape(n_b * n_t, D_MODEL), *ffn2, final_norm=True).reshape(n_b, n_t, D_MODEL)
    ys = _ffn(xs, *ffn2, final_norm=True).reshape(s_b, s_t, D_MODEL)

    states = (jnp.swapaxes(p_conv, 0, 1), p_lru, jnp.swapaxes(p_pool, 0, 1), mk, mv,
              jnp.swapaxes(s_conv, 0, 1), s_lru, jnp.swapaxes(s_pool, 0, 1))
    return yp, ys, states


def kernel(x_prompt, x_sample, mem_prompt, state_conv, state_lru, state_pool, cache_mem_k, cache_mem_v, ffn1_norm, ffn1_w_gate, ffn1_w_up, ffn1_w_down, mix_norm, w_in, conv_w, conv_b, lru_wa, lru_ba, lru_wx, lru_bx, lru_lambda, pool_w, pool_scale, w_out, xattn_norm, mem_norm, xattn_wq, xattn_wk, xattn_wv, xattn_wo, ffn2_norm, ffn2_w_gate, ffn2_w_up, ffn2_w_down, final_norm):
    depth = ffn1_norm.shape[0]
    assert depth == 1, "the final RMSNorm is fused into the last layer's second FFN"
    names = ("ffn1_norm", "ffn1_w_gate", "ffn1_w_up", "ffn1_w_down", "mix_norm", "w_in", "conv_w", "conv_b",
             "lru_wa", "lru_ba", "lru_wx", "lru_bx", "lru_lambda", "pool_w", "pool_scale", "w_out",
             "xattn_norm", "mem_norm", "xattn_wq", "xattn_wk", "xattn_wv", "xattn_wo",
             "ffn2_norm", "ffn2_w_gate", "ffn2_w_up", "ffn2_w_down")
    stacked = (ffn1_norm, ffn1_w_gate, ffn1_w_up, ffn1_w_down, mix_norm, w_in, conv_w, conv_b,
               lru_wa, lru_ba, lru_wx, lru_bx, lru_lambda, pool_w, pool_scale, w_out,
               xattn_norm, mem_norm, xattn_wq, xattn_wk, xattn_wv, xattn_wo,
               ffn2_norm, ffn2_w_gate, ffn2_w_up, ffn2_w_down)
    p = {n: a[0] for n, a in zip(names, stacked)}
    yp, ys, st = _layer(x_prompt, x_sample, mem_prompt, state_conv[0], state_lru[0], state_pool[0],
                        cache_mem_k[0], cache_mem_v[0], p, final_norm)
    p_conv, p_lru, p_pool, p_mk, p_mv, s_conv, s_lru, s_pool = (s[None] for s in st)
    return (yp, ys, p_conv, p_lru, p_pool, p_mk, p_mv, s_conv, s_lru, s_pool)
```

```python
import functools
import itertools

import jax
import jax.numpy as jnp
from jax import lax
from jax.experimental import pallas as pl
from jax.experimental.pallas import tpu as pltpu

F32 = jnp.float32
BF16 = jnp.bfloat16

D_MODEL = 1024
D_FF = 2816
LRU_WIDTH = 512
LRU_HEADS = 8
LRU_C = 8.0
CONV_W = 4
POOL_WIDTH = 512
POOL_WINDOWS = (2, 4, 8, 16)
POOL_GROUP_DIM = POOL_WIDTH // len(POOL_WINDOWS)
POOL_HIST = max(POOL_WINDOWS) - 1
CONV_HIST = CONV_W - 1
MEM_LEN = 256
XA_HEADS = 4
XA_HEAD_DIM = D_MODEL // XA_HEADS
EPS = 1e-6
PAST_LEN = 16384

MXU_TILE = 256
LANES = 128
assert XA_HEAD_DIM == 2 * LANES
FF_CHUNK = MXU_TILE
N_FF_CHUNKS = D_FF // FF_CHUNK
ROW_TILE = 512
SUB_TILES = 2
WEIGHT_LOOKAHEAD = 4
COL_STAGE_SLOTS = 4
ROW_STAGE_SLOTS = 2
VMEM_LIMIT_BYTES = 56 * 1024 * 1024


def _rmsnorm(x, g):
    var = jnp.mean(x * x, axis=-1, keepdims=True)
    return x * lax.rsqrt(var + EPS) * g


def _gelu_tanh(x):
    c0 = (2.0 / jnp.pi) ** 0.5
    half_x = 0.5 * x
    return half_x + half_x * jnp.tanh(x * (c0 + (c0 * 0.044715) * (x * x)))


def _dot(a, b):
    return jnp.dot(a, b, preferred_element_type=F32)


def _const_spec(shape):
    zeros = (0,) * len(shape)
    return pl.BlockSpec(shape, lambda *_: zeros, pipeline_mode=pl.Buffered(1))


def _params(semantics):
    return pltpu.CompilerParams(dimension_semantics=semantics, vmem_limit_bytes=VMEM_LIMIT_BYTES)


def _ffn_weight_fetcher(wg_hbm, wu_hbm, wd_hbm, wg_scr, wu_scr, wd_scr, col_stage, row_stage, sems):
    stages = (col_stage, row_stage)

    def chunk(c):
        cols = pl.ds(c * FF_CHUNK, FF_CHUNK)
        return {"gate": (wg_hbm.at[:, cols], 0, wg_scr.at[c]), "up": (wu_hbm.at[:, cols], 0, wu_scr.at[c]),
                "down": (wd_hbm.at[cols, :], 1, wd_scr.at[c])}

    items = [chunk(0)["gate"], chunk(0)["up"]]
    for c in range(1, N_FF_CHUNKS):
        items += [chunk(c)["gate"], chunk(c)["up"], chunk(c - 1)["down"]]
    items.append(chunk(N_FF_CHUNKS - 1)["down"])
    uses = [0, 0]
    plan = []
    for src, kind, dst in items:
        slot = uses[kind] % stages[kind].shape[0]
        uses[kind] += 1
        staged = stages[kind].at[slot]
        plan.append((pltpu.make_async_copy(src, staged, sems.at[kind, slot]), staged, dst, (kind, slot)))
    slot_free_after = {}
    state = {"next": 0}

    def start(m):
        key = plan[m][3]
        assert slot_free_after.get(key, -1) < state["next"], "staging slot reused before it was converted"
        slot_free_after[key] = m
        plan[m][0].start()

    for m in range(min(WEIGHT_LOOKAHEAD, len(plan))):
        start(m)

    def fetch():
        n = state["next"]
        copy, staged, dst, _ = plan[n]
        copy.wait()
        dst[...] = staged[...].astype(BF16)
        state["next"] = n + 1
        if n + WEIGHT_LOOKAHEAD < len(plan):
            start(n + WEIGHT_LOOKAHEAD)

    return fetch, len(plan)


def _ffn_kernel(x_ref, g_ref, wg_hbm, wu_hbm, wd_hbm, fn_ref, o_ref,
                wg_scr, wu_scr, wd_scr, col_stage, row_stage, sems, h_scr, a_scr, acc_scr, *, final_norm):
    subs = range(SUB_TILES)
    rows = [slice(s * ROW_TILE, (s + 1) * ROW_TILE) for s in subs]

    def gate_up(s, c):
        h = h_scr[s]
        g = _dot(h, wg_scr[c])
        u = _dot(h, wu_scr[c])
        a_scr[s, c % 2] = (g * jax.nn.sigmoid(g) * u).astype(BF16)

    def down(s, c):
        d = _dot(a_scr[s, c % 2], wd_scr[c])
        if c == 0:
            acc_scr[s] = d
        else:
            acc_scr[s] += d

    def body(load_weights):
        if load_weights:
            fetch, n_items = _ffn_weight_fetcher(wg_hbm, wu_hbm, wd_hbm, wg_scr, wu_scr, wd_scr,
                                                 col_stage, row_stage, sems)
        else:
            fetch, n_items = (lambda: None), 0
        for s in subs:
            h_scr[s] = _rmsnorm(x_ref[rows[s], :], g_ref[...]).astype(BF16)
        fetch()
        fetch()
        for s in subs:
            gate_up(s, 0)
        for c in range(1, N_FF_CHUNKS):
            fetch()
            fetch()
            fetch()
            for s in subs:
                gate_up(s, c)
                down(s, c - 1)
        fetch()
        assert n_items in (0, 3 * N_FF_CHUNKS)
        for s in subs:
            down(s, N_FF_CHUNKS - 1)
        for s in subs:
            y = x_ref[rows[s], :] + 0.5 * acc_scr[s]
            if final_norm:
                y = _rmsnorm(y, fn_ref[...])
            o_ref[rows[s], :] = y

    first = pl.program_id(0) == 0
    pl.when(first)(functools.partial(body, True))
    pl.when(jnp.logical_not(first))(functools.partial(body, False))


def _ffn(x, g, wg, wu, wd, fn, *, final_norm):
    row_spec = pl.BlockSpec((SUB_TILES * ROW_TILE, D_MODEL), lambda i: (i, 0))
    hbm = pl.BlockSpec(memory_space=pl.ANY)
    return pl.pallas_call(
        functools.partial(_ffn_kernel, final_norm=final_norm),
        grid=(x.shape[0] // (SUB_TILES * ROW_TILE),),
        in_specs=[row_spec, _const_spec((1, D_MODEL)), hbm, hbm, hbm, _const_spec((1, D_MODEL))],
        out_specs=row_spec,
        out_shape=jax.ShapeDtypeStruct(x.shape, F32),
        scratch_shapes=[pltpu.VMEM((N_FF_CHUNKS, D_MODEL, FF_CHUNK), BF16),
                        pltpu.VMEM((N_FF_CHUNKS, D_MODEL, FF_CHUNK), BF16),
                        pltpu.VMEM((N_FF_CHUNKS, FF_CHUNK, D_MODEL), BF16),
                        pltpu.VMEM((COL_STAGE_SLOTS, D_MODEL, FF_CHUNK), F32),
                        pltpu.VMEM((ROW_STAGE_SLOTS, FF_CHUNK, D_MODEL), F32),
                        pltpu.SemaphoreType.DMA((2, COL_STAGE_SLOTS)),
                        pltpu.VMEM((SUB_TILES, ROW_TILE, D_MODEL), BF16),
                        pltpu.VMEM((SUB_TILES, 2, ROW_TILE, FF_CHUNK), BF16),
                        pltpu.VMEM((SUB_TILES, ROW_TILE, D_MODEL), F32)],
        compiler_params=_params(("arbitrary",)),
        name="ffn_final" if final_norm else "ffn",
    )(x, g, wg, wu, wd, fn)


def _mix_kernel(x_ref, gm_ref, win_ref, cw_ref, cb_ref, wa_ref, ba_ref, wx_ref, bx_ref, lam_ref,
                pw_ref, ps_ref, wout_ref, sconv_ref, slru_ref, spool_ref,
                o_ref, oconv_ref, olru_ref, opool_ref,
                lru_ext, pool_ext, h_scr, x_scr, hn_scr, gate_scr, a_scr, b_scr, y_scr, *, tt, bb, n_sub, pos0):
    rows = tt * bb
    i = pl.program_id(1)

    @pl.when(i == 0)
    def _():
        lru_ext[0:CONV_HIST * bb, :] = sconv_ref[...].reshape(CONV_HIST * bb, LRU_WIDTH)
        pool_ext[0:POOL_HIST * bb, :] = spool_ref[...].reshape(POOL_HIST * bb, POOL_WIDTH)
        h_scr[...] = slru_ref[...]

    half = MXU_TILE

    def prepare_pieces(s):
        n_pieces = max(1, min(4, tt // 8))
        tq = tt // n_pieces

        def piece(q):
            x = jnp.swapaxes(x_ref[:, s * tt + q * tq:s * tt + (q + 1) * tq, :], 0, 1)
            x = x.reshape(tq * bb, D_MODEL)
            x_scr[s, q * tq * bb:(q + 1) * tq * bb, :] = x
            hn_scr[s, q * tq * bb:(q + 1) * tq * bb, :] = _rmsnorm(x, gm_ref[...]).astype(BF16)
        return [functools.partial(piece, q) for q in range(n_pieces)]

    def project_in_pieces(s):
        lru_rows = slice((CONV_HIST + s * tt) * bb, (CONV_HIST + (s + 1) * tt) * bb)
        pool_rows = slice((POOL_HIST + s * tt) * bb, (POOL_HIST + (s + 1) * tt) * bb)

        def piece(n):
            p = _dot(hn_scr[s], win_ref[:, n * half:(n + 1) * half])
            cols = slice((n % 2) * half, (n % 2 + 1) * half)
            if n < 2:
                lru_ext[lru_rows, cols] = p
            elif n < 4:
                gate_scr[s, :, cols] = p
            else:
                pool_ext[pool_rows, cols] = p
        return [functools.partial(piece, n) for n in range(3 * LRU_WIDTH // half)]

    def lru_piece(s, j):
        t0 = s * tt
        lanes = slice(j * half, (j + 1) * half)
        cw = cw_ref[:, lanes]
        u = cb_ref[:, lanes] + sum(lru_ext[(t0 + k) * bb:(t0 + k) * bb + rows, lanes] * cw[k:k + 1, :]
                                   for k in range(CONV_W))
        ub = u.astype(BF16)
        r = jax.nn.sigmoid(_dot(ub, wa_ref[j]) + ba_ref[:, lanes])
        ig = jax.nn.sigmoid(_dot(ub, wx_ref[j]) + bx_ref[:, lanes])
        neg_lam = -lam_ref[:, lanes]
        softplus = jnp.maximum(neg_lam, 0.0) + jnp.log1p(jnp.exp(-jnp.abs(neg_lam)))
        log_a = (-LRU_C * softplus) * r
        a = jnp.exp(log_a)
        a_scr[s, :, lanes] = a
        m = jnp.maximum(1.0 - a * a, 0.0)
        b_scr[s, :, lanes] = jnp.where(m > 0.0, m * lax.rsqrt(m), 0.0) * ig * u
        hcur = h_scr[:, lanes]
        for t in range(tt):
            sl = slice(t * bb, (t + 1) * bb)
            hcur = a_scr[s, sl, lanes] * hcur + b_scr[s, sl, lanes]
            b_scr[s, sl, lanes] = hcur
        h_scr[:, lanes] = hcur
        y_scr[s, :, lanes] = (_gelu_tanh(gate_scr[s, :, lanes]) * b_scr[s, :, lanes]).astype(BF16)

    def pool_piece(s, j):
        t0 = s * tt
        t_idx = lax.broadcasted_iota(jnp.int32, (rows, POOL_GROUP_DIM), 0) // bb
        pos = pos0 + (i * n_sub + s) * tt + t_idx
        groups_per_tile = half // POOL_GROUP_DIM
        deltas = []
        for g in range(j * groups_per_tile, (j + 1) * groups_per_tile):
            win = POOL_WINDOWS[g]
            lanes = slice(g * POOL_GROUP_DIM, (g + 1) * POOL_GROUP_DIM)
            w_sum = pool_ext[t0 * bb:(t0 + POOL_HIST + tt) * bb, lanes]
            span = 1
            while span < win:
                w_sum = w_sum[span * bb:, :] + w_sum[:-span * bb, :]
                span *= 2
            w_sum = w_sum[w_sum.shape[0] - rows:, :]
            cnt = jnp.minimum(pos + 1, win).astype(F32)
            u_pool = pool_ext[(t0 + POOL_HIST) * bb:(t0 + POOL_HIST + tt) * bb, lanes]
            deltas.append(w_sum / cnt - u_pool)
        delta = jnp.concatenate(deltas, axis=1).astype(BF16)
        lanes = slice(j * half, (j + 1) * half)
        y_scr[s, :, LRU_WIDTH + j * half:LRU_WIDTH + (j + 1) * half] = (
            ps_ref[:, lanes] * _dot(delta, pw_ref[j])).astype(BF16)

    def mix_pieces(s):
        return [functools.partial(piece, s, j) for j in range(LRU_WIDTH // half) for piece in (lru_piece, pool_piece)]

    def project_out_pieces(s):
        def piece(n):
            cols = slice(n * half, (n + 1) * half)
            out = x_scr[s, :, cols] + _dot(y_scr[s], wout_ref[:, cols])
            o_ref[:, s * tt:(s + 1) * tt, cols] = jnp.swapaxes(out.reshape(tt, bb, half), 0, 1)
        return [functools.partial(piece, n) for n in range(D_MODEL // half)]

    def emit(*streams):
        for pieces in itertools.zip_longest(*streams):
            for piece in pieces:
                if piece is not None:
                    piece()

    emit(prepare_pieces(0))
    for k in range(1, n_sub + 3):
        mxu, valu = [], []
        if k - 1 < n_sub:
            mxu += project_in_pieces(k - 1)
        if 0 <= k - 3 < n_sub:
            mxu += project_out_pieces(k - 3)
        if k < n_sub:
            valu += prepare_pieces(k)
        if 0 <= k - 2 < n_sub:
            valu += mix_pieces(k - 2)
        emit(mxu, valu)

    new_conv = lru_ext[n_sub * tt * bb:(n_sub * tt + CONV_HIST) * bb, :]
    oconv_ref[...] = new_conv.reshape(CONV_HIST, bb, LRU_WIDTH)
    lru_ext[0:CONV_HIST * bb, :] = new_conv
    new_pool = pool_ext[n_sub * tt * bb:(n_sub * tt + POOL_HIST) * bb, :]
    opool_ref[...] = new_pool.reshape(POOL_HIST, bb, POOL_WIDTH)
    pool_ext[0:POOL_HIST * bb, :] = new_pool
    olru_ref[...] = h_scr[...]


def _mix(x, w, sconv, slru, spool, *, tt, bb, n_sub, pos0):
    n_b, n_t = x.shape[0], x.shape[1]
    rows = tt * bb
    x_spec = pl.BlockSpec((bb, n_sub * tt, D_MODEL), lambda j, i: (j, i, 0))
    conv_spec = pl.BlockSpec((CONV_HIST, bb, LRU_WIDTH), lambda j, i: (0, j, 0))
    lru_spec = pl.BlockSpec((bb, LRU_WIDTH), lambda j, i: (j, 0))
    pool_spec = pl.BlockSpec((POOL_HIST, bb, POOL_WIDTH), lambda j, i: (0, j, 0))
    return pl.pallas_call(
        functools.partial(_mix_kernel, tt=tt, bb=bb, n_sub=n_sub, pos0=pos0),
        grid=(n_b // bb, n_t // (n_sub * tt)),
        in_specs=[x_spec, _const_spec((1, D_MODEL)), _const_spec((D_MODEL, 3 * LRU_WIDTH)),
                  _const_spec((CONV_W, LRU_WIDTH)), _const_spec((1, LRU_WIDTH)),
                  _const_spec((2, MXU_TILE, MXU_TILE)), _const_spec((1, LRU_WIDTH)),
                  _const_spec((2, MXU_TILE, MXU_TILE)), _const_spec((1, LRU_WIDTH)),
                  _const_spec((1, LRU_WIDTH)),
                  _const_spec((2, MXU_TILE, MXU_TILE)), _const_spec((1, POOL_WIDTH)),
                  _const_spec((D_MODEL, D_MODEL)),
                  conv_spec, lru_spec, pool_spec],
        out_specs=[x_spec, conv_spec, lru_spec, pool_spec],
        out_shape=[jax.ShapeDtypeStruct(x.shape, F32),
                   jax.ShapeDtypeStruct((CONV_HIST, n_b, LRU_WIDTH), F32),
                   jax.ShapeDtypeStruct((n_b, LRU_WIDTH), F32),
                   jax.ShapeDtypeStruct((POOL_HIST, n_b, POOL_WIDTH), F32)],
        scratch_shapes=[pltpu.VMEM(((CONV_HIST + n_sub * tt) * bb, LRU_WIDTH), F32),
                        pltpu.VMEM(((POOL_HIST + n_sub * tt) * bb, POOL_WIDTH), F32),
                        pltpu.VMEM((bb, LRU_WIDTH), F32),
                        pltpu.VMEM((n_sub, rows, D_MODEL), F32),
                        pltpu.VMEM((n_sub, rows, D_MODEL), BF16),
                        pltpu.VMEM((n_sub, rows, LRU_WIDTH), F32),
                        pltpu.VMEM((n_sub, rows, LRU_WIDTH), F32),
                        pltpu.VMEM((n_sub, rows, LRU_WIDTH), F32),
                        pltpu.VMEM((n_sub, rows, LRU_WIDTH + POOL_WIDTH), BF16)],
        compiler_params=_params(("parallel", "arbitrary")),
        name="mix",
    )(x, w["mix_norm"], w["w_in"], w["conv_w"], w["conv_b"], w["lru_wa"], w["lru_ba"],
      w["lru_wx"], w["lru_bx"], w["lru_lambda"], w["pool_w"], w["pool_scale"], w["w_out"],
      sconv, slru, spool)


def _memkv_kernel(m_ref, g_ref, wk_ref, wv_ref, k_ref, v_ref, kb_ref, vb_ref):
    n_b = m_ref.shape[0]
    m = _rmsnorm(m_ref[...].reshape(n_b * MEM_LEN, D_MODEL), g_ref[...]).astype(BF16)
    for w_ref, o_ref, ob_ref in ((wk_ref, k_ref, kb_ref), (wv_ref, v_ref, vb_ref)):
        kv = _dot(m, w_ref[...])
        for b in range(n_b):
            kv_b = kv[b * MEM_LEN:(b + 1) * MEM_LEN, :]
            ob_ref[b] = kv_b.astype(BF16)
            for hd in range(XA_HEADS):
                o_ref[b, :, hd, :] = kv_b[:, hd * XA_HEAD_DIM:(hd + 1) * XA_HEAD_DIM]


def _memkv(mem, g, wk, wv):
    n_b = mem.shape[0]
    per_step = ROW_TILE // MEM_LEN
    row_spec = pl.BlockSpec((per_step, MEM_LEN, D_MODEL), lambda i: (i, 0, 0))
    head_spec = pl.BlockSpec((per_step, MEM_LEN, XA_HEADS, XA_HEAD_DIM), lambda i: (i, 0, 0, 0))
    return pl.pallas_call(
        _memkv_kernel,
        grid=(n_b // per_step,),
        in_specs=[row_spec, _const_spec((1, D_MODEL)), _const_spec((D_MODEL, D_MODEL)),
                  _const_spec((D_MODEL, D_MODEL))],
        out_specs=[head_spec, head_spec, row_spec, row_spec],
        out_shape=[jax.ShapeDtypeStruct((n_b, MEM_LEN, XA_HEADS, XA_HEAD_DIM), F32)] * 2
        + [jax.ShapeDtypeStruct((n_b, MEM_LEN, D_MODEL), BF16)] * 2,
        compiler_params=_params(("parallel",)),
        name="memkv",
    )(mem, g, wk, wv)


def _xattn_prompt_kernel(x_ref, g_ref, wq_ref, k_ref, v_ref, wo_ref, o_ref, q_scr, s_scr, att_scr):
    subs = range(SUB_TILES)
    rows = [slice(s * ROW_TILE, (s + 1) * ROW_TILE) for s in subs]
    heads = [slice(hd * XA_HEAD_DIM, (hd + 1) * XA_HEAD_DIM) for hd in range(XA_HEADS)]

    def project_q(s):
        q_scr[s] = _dot(_rmsnorm(x_ref[rows[s], :], g_ref[...]).astype(BF16), wq_ref[...]).astype(BF16)

    def scores(s):
        for hd, cols in enumerate(heads):
            s_scr[s, hd] = lax.dot_general(q_scr[s, :, cols], k_ref[:, cols], (((1,), (1,)), ((), ())),
                                           preferred_element_type=F32) * (XA_HEAD_DIM ** -0.5)

    def attend(s):
        sc = s_scr[s]
        e = jnp.exp(sc - jnp.max(sc, axis=-1, keepdims=True))
        s_scr[s] = e / jnp.sum(e, axis=-1, keepdims=True)
        for hd, cols in enumerate(heads):
            att_scr[s, :, cols] = _dot(s_scr[s, hd].astype(BF16), v_ref[:, cols]).astype(BF16)

    def project_out(s):
        o_ref[rows[s], :] = x_ref[rows[s], :] + _dot(att_scr[s], wo_ref[...])

    stages = (project_q, scores, attend, project_out)
    for k in range(SUB_TILES + len(stages) - 1):
        for depth in reversed(range(len(stages))):
            if 0 <= k - depth < SUB_TILES:
                stages[depth](k - depth)


def _xattn_prompt(x, g, wq, k, v, wo):
    n_b, n_t = x.shape[0], x.shape[1]
    tile = SUB_TILES * ROW_TILE
    x_spec = pl.BlockSpec((None, tile, D_MODEL), lambda b, i: (b, i, 0))
    kv_spec = pl.BlockSpec((None, MEM_LEN, D_MODEL), lambda b, i: (b, 0, 0))
    return pl.pallas_call(
        _xattn_prompt_kernel,
        grid=(n_b, n_t // tile),
        in_specs=[x_spec, _const_spec((1, D_MODEL)), _const_spec((D_MODEL, D_MODEL)),
                  kv_spec, kv_spec, _const_spec((D_MODEL, D_MODEL))],
        out_specs=x_spec,
        out_shape=jax.ShapeDtypeStruct(x.shape, F32),
        scratch_shapes=[pltpu.VMEM((SUB_TILES, ROW_TILE, D_MODEL), BF16),
                        pltpu.VMEM((SUB_TILES, XA_HEADS, ROW_TILE, MEM_LEN), F32),
                        pltpu.VMEM((SUB_TILES, ROW_TILE, D_MODEL), BF16)],
        compiler_params=_params(("parallel", "parallel")),
        name="xattn_prompt",
    )(x, g, wq, k, v, wo)


def _xattn_sample_kernel(x_ref, g_ref, wq_ref, k0_ref, k1_ref, v0_ref, v1_ref, wo_ref, o_ref, att_scr, s_scr,
                         *, bc, tq):
    x = x_ref[...]
    q = _dot(_rmsnorm(x, g_ref[...]).astype(BF16), wq_ref[...])
    halves = [[r.reshape(bc * MEM_LEN * XA_HEADS, LANES) for r in pair]
              for pair in ((k0_ref, k1_ref), (v0_ref, v1_ref))]

    def head(pair, b, hd):
        rows = pl.ds(b * MEM_LEN * XA_HEADS + hd, MEM_LEN, stride=XA_HEADS)
        return jnp.concatenate([r[rows, :] for r in pair], axis=1).astype(BF16)

    pairs = [(b, hd) for b in range(bc) for hd in range(XA_HEADS)]
    for n, (b, hd) in enumerate(pairs):
        qh = q[b * tq:(b + 1) * tq, hd * XA_HEAD_DIM:(hd + 1) * XA_HEAD_DIM].astype(BF16)
        s_scr[n * tq:(n + 1) * tq, :] = lax.dot_general(
            qh, head(halves[0], b, hd), (((1,), (1,)), ((), ())),
            preferred_element_type=F32) * (XA_HEAD_DIM ** -0.5)
    s = s_scr[...]
    e = jnp.exp(s - jnp.max(s, axis=-1, keepdims=True))
    s_scr[...] = e / jnp.sum(e, axis=-1, keepdims=True)
    for n, (b, hd) in enumerate(pairs):
        p = s_scr[n * tq:(n + 1) * tq, :].astype(BF16)
        att_scr[b * tq:(b + 1) * tq, hd * XA_HEAD_DIM:(hd + 1) * XA_HEAD_DIM] = _dot(p, head(halves[1], b, hd))
    o_ref[...] = x + _dot(att_scr[...].astype(BF16), wo_ref[...])


def _xattn_sample(x, g, wq, k, v, wo, *, bc):
    n_b = k.shape[0]
    tq = x.shape[0] // n_b
    x_spec = pl.BlockSpec((bc * tq, D_MODEL), lambda j: (j, 0))
    kv_specs = [pl.BlockSpec((bc, MEM_LEN, XA_HEADS, LANES), functools.partial(lambda half, j: (j, 0, 0, half), half))
                for half in range(XA_HEAD_DIM // LANES)]
    return pl.pallas_call(
        functools.partial(_xattn_sample_kernel, bc=bc, tq=tq),
        grid=(n_b // bc,),
        in_specs=[x_spec, _const_spec((1, D_MODEL)), _const_spec((D_MODEL, D_MODEL)),
                  *kv_specs, *kv_specs, _const_spec((D_MODEL, D_MODEL))],
        out_specs=x_spec,
        out_shape=jax.ShapeDtypeStruct(x.shape, F32),
        scratch_shapes=[pltpu.VMEM((bc * tq, D_MODEL), F32),
                        pltpu.VMEM((bc * XA_HEADS * tq, MEM_LEN), F32)],
        compiler_params=_params(("parallel",)),
        name="xattn_sample",
    )(x, g, wq, k, k, v, v, wo)


def _block_diag_tiles(w):
    groups, dg, _ = w.shape
    per_tile = MXU_TILE // dg
    w = w.reshape(groups // per_tile, per_tile, dg, dg)
    eye = jnp.eye(per_tile, dtype=w.dtype)
    tiles = jnp.einsum("npij,pq->npiqj", w, eye)
    return tiles.reshape(groups // per_tile, MXU_TILE, MXU_TILE).astype(BF16)


def _row(v):
    return v.reshape(1, -1).astype(F32)


def _layer(x_prompt, x_sample, mem_prompt, sconv, slru, spool, cache_k, cache_v, p, final_norm):
    n_b, n_t, _ = x_prompt.shape
    s_b, s_t, _ = x_sample.shape
    ffn1 = (_row(p["ffn1_norm"]), p["ffn1_w_gate"], p["ffn1_w_up"], p["ffn1_w_down"], _row(final_norm))
    ffn2 = (_row(p["ffn2_norm"]), p["ffn2_w_gate"], p["ffn2_w_up"], p["ffn2_w_down"], _row(final_norm))
    mixw = {
        "mix_norm": _row(p["mix_norm"]), "w_in": p["w_in"].astype(BF16),
        "conv_w": p["conv_w"].astype(F32), "conv_b": _row(p["conv_b"]),
        "lru_wa": _block_diag_tiles(p["lru_wa"]), "lru_ba": _row(p["lru_ba"]),
        "lru_wx": _block_diag_tiles(p["lru_wx"]), "lru_bx": _row(p["lru_bx"]),
        "lru_lambda": _row(p["lru_lambda"]),
        "pool_w": _block_diag_tiles(p["pool_w"]), "pool_scale": _row(p["pool_scale"]),
        "w_out": p["w_out"].astype(BF16),
    }
    gx, wq, wo = _row(p["xattn_norm"]), p["xattn_wq"].astype(BF16), p["xattn_wo"].astype(BF16)

    mk, mv, mk_bf, mv_bf = _memkv(mem_prompt, _row(p["mem_norm"]),
                                  p["xattn_wk"].astype(BF16), p["xattn_wv"].astype(BF16))

    xp = _ffn(x_prompt.reshape(n_b * n_t, D_MODEL), *ffn1, final_norm=False)
    xs = _ffn(x_sample.reshape(s_b * s_t, D_MODEL), *ffn1, final_norm=False)

    xp, p_conv, p_lru, p_pool = _mix(
        xp.reshape(n_b, n_t, D_MODEL), mixw, jnp.zeros((CONV_HIST, n_b, LRU_WIDTH), F32),
        jnp.zeros((n_b, LRU_WIDTH), F32), jnp.zeros((POOL_HIST, n_b, POOL_WIDTH), F32),
        tt=ROW_TILE // n_b, bb=n_b, n_sub=SUB_TILES, pos0=0)
    xp = _xattn_prompt(xp, gx, wq, mk_bf, mv_bf, wo)

    xs, s_conv, s_lru, s_pool = _mix(
        xs.reshape(s_b, s_t, D_MODEL), mixw, jnp.swapaxes(sconv, 0, 1), slru, jnp.swapaxes(spool, 0, 1),
        tt=s_t, bb=ROW_TILE // s_t, n_sub=1, pos0=PAST_LEN)
    xs = _xattn_sample(xs.reshape(s_b * s_t, D_MODEL), gx, wq, cache_k, cache_v, wo, bc=8)

    yp = _ffn(xp.reshape(n_b * n_t, D_MODEL), *ffn2, final_norm=True).reshape(n_b, n_t, D_MODEL)
    ys = _ffn(xs, *ffn2, final_norm=True).reshape(s_b, s_t, D_MODEL)

    states = (jnp.swapaxes(p_conv, 0, 1), p_lru, jnp.swapaxes(p_pool, 0, 1), mk, mv,
              jnp.swapaxes(s_conv, 0, 1), s_lru, jnp.swapaxes(s_pool, 0, 1))
    return yp, ys, states


def kernel(x_prompt, x_sample, mem_prompt, state_conv, state_lru, state_pool, cache_mem_k, cache_mem_v, ffn1_norm, ffn1_w_gate, ffn1_w_up, ffn1_w_down, mix_norm, w_in, conv_w, conv_b, lru_wa, lru_ba, lru_wx, lru_bx, lru_lambda, pool_w, pool_scale, w_out, xattn_norm, mem_norm, xattn_wq, xattn_wk, xattn_wv, xattn_wo, ffn2_norm, ffn2_w_gate, ffn2_w_up, ffn2_w_down, final_norm):
    depth = ffn1_norm.shape[0]
    assert depth == 1, "the final RMSNorm is fused into the last layer's second FFN"
    names = ("ffn1_norm", "ffn1_w_gate", "ffn1_w_up", "ffn1_w_down", "mix_norm", "w_in", "conv_w", "conv_b",
             "lru_wa", "lru_ba", "lru_wx", "lru_bx", "lru_lambda", "pool_w", "pool_scale", "w_out",
             "xattn_norm", "mem_norm", "xattn_wq", "xattn_wk", "xattn_wv", "xattn_wo",
             "ffn2_norm", "ffn2_w_gate", "ffn2_w_up", "ffn2_w_down")
    stacked = (ffn1_norm, ffn1_w_gate, ffn1_w_up, ffn1_w_down, mix_norm, w_in, conv_w, conv_b,
               lru_wa, lru_ba, lru_wx, lru_bx, lru_lambda, pool_w, pool_scale, w_out,
               xattn_norm, mem_norm, xattn_wq, xattn_wk, xattn_wv, xattn_wo,
               ffn2_norm, ffn2_w_gate, ffn2_w_up, ffn2_w_down)
    p = {n: a[0] for n, a in zip(names, stacked)}
    yp, ys, st = _layer(x_prompt, x_sample, mem_prompt, state_conv[0], state_lru[0], state_pool[0],
                        cache_mem_k[0], cache_mem_v[0], p, final_norm)
    p_conv, p_lru, p_pool, p_mk, p_mv, s_conv, s_lru, s_pool = (s[None] for s in st)
    return (yp, ys, p_conv, p_lru, p_pool, p_mk, p_mv, s_conv, s_lru, s_pool)
```

```python
import functools
import itertools

import jax
import jax.numpy as jnp
from jax import lax
from jax.experimental import pallas as pl
from jax.experimental.pallas import tpu as pltpu

F32 = jnp.float32
BF16 = jnp.bfloat16

D_MODEL = 1024
D_FF = 2816
LRU_WIDTH = 512
LRU_HEADS = 8
LRU_C = 8.0
CONV_W = 4
POOL_WIDTH = 512
POOL_WINDOWS = (2, 4, 8, 16)
POOL_GROUP_DIM = POOL_WIDTH // len(POOL_WINDOWS)
POOL_HIST = max(POOL_WINDOWS) - 1
CONV_HIST = CONV_W - 1
MEM_LEN = 256
XA_HEADS = 4
XA_HEAD_DIM = D_MODEL // XA_HEADS
EPS = 1e-6
PAST_LEN = 16384

MXU_TILE = 256
LANES = 128
assert XA_HEAD_DIM == 2 * LANES
FF_CHUNK = MXU_TILE
N_FF_CHUNKS = D_FF // FF_CHUNK
ROW_TILE = 512
SUB_TILES = 2
WEIGHT_LOOKAHEAD = 4
COL_STAGE_SLOTS = 4
ROW_STAGE_SLOTS = 2
VMEM_LIMIT_BYTES = 56 * 1024 * 1024


def _rmsnorm(x, g):
    var = jnp.mean(x * x, axis=-1, keepdims=True)
    return x * lax.rsqrt(var + EPS) * g


def _gelu_tanh(x):
    c0 = (2.0 / jnp.pi) ** 0.5
    half_x = 0.5 * x
    return half_x + half_x * jnp.tanh(x * (c0 + (c0 * 0.044715) * (x * x)))


def _dot(a, b):
    return jnp.dot(a, b, preferred_element_type=F32)


def _const_spec(shape):
    zeros = (0,) * len(shape)
    return pl.BlockSpec(shape, lambda *_: zeros, pipeline_mode=pl.Buffered(1))


def _params(semantics):
    return pltpu.CompilerParams(dimension_semantics=semantics, vmem_limit_bytes=VMEM_LIMIT_BYTES)


def _ffn_weight_fetcher(wg_hbm, wu_hbm, wd_hbm, wg_scr, wu_scr, wd_scr, col_stage, row_stage, sems):
    stages = (col_stage, row_stage)

    def chunk(c):
        cols = pl.ds(c * FF_CHUNK, FF_CHUNK)
        return {"gate": (wg_hbm.at[:, cols], 0, wg_scr.at[c]), "up": (wu_hbm.at[:, cols], 0, wu_scr.at[c]),
                "down": (wd_hbm.at[cols, :], 1, wd_scr.at[c])}

    items = [chunk(0)["gate"], chunk(0)["up"]]
    for c in range(1, N_FF_CHUNKS):
        items += [chunk(c)["gate"], chunk(c)["up"], chunk(c - 1)["down"]]
    items.append(chunk(N_FF_CHUNKS - 1)["down"])
    uses = [0, 0]
    plan = []
    for src, kind, dst in items:
        slot = uses[kind] % stages[kind].shape[0]
        uses[kind] += 1
        staged = stages[kind].at[slot]
        plan.append((pltpu.make_async_copy(src, staged, sems.at[kind, slot]), staged, dst, (kind, slot)))
    slot_free_after = {}
    state = {"next": 0}

    def start(m):
        key = plan[m][3]
        assert slot_free_after.get(key, -1) < state["next"], "staging slot reused before it was converted"
        slot_free_after[key] = m
        plan[m][0].start()

    for m in range(min(WEIGHT_LOOKAHEAD, len(plan))):
        start(m)

    def fetch():
        n = state["next"]
        copy, staged, dst, _ = plan[n]
        copy.wait()
        dst[...] = staged[...].astype(BF16)
        state["next"] = n + 1
        if n + WEIGHT_LOOKAHEAD < len(plan):
            start(n + WEIGHT_LOOKAHEAD)

    return fetch, len(plan)


def _ffn_kernel(x_ref, g_ref, wg_hbm, wu_hbm, wd_hbm, fn_ref, o_ref,
                wg_scr, wu_scr, wd_scr, col_stage, row_stage, sems, h_scr, a_scr, acc_scr,
                *, n_steps, final_norm):
    subs = range(SUB_TILES)
    rows = [slice(s * ROW_TILE, (s + 1) * ROW_TILE) for s in subs]

    def gate_up(s, c):
        h = h_scr[s]
        g = _dot(h, wg_scr[c])
        u = _dot(h, wu_scr[c])
        a_scr[s, c % 2] = (g * jax.nn.sigmoid(g) * u).astype(BF16)

    def down(s, c):
        d = _dot(a_scr[s, c % 2], wd_scr[c])
        if c == 0:
            acc_scr[s] = d
        else:
            acc_scr[s] += d

    def body(load_weights):
        if load_weights:
            fetch, n_items = _ffn_weight_fetcher(wg_hbm, wu_hbm, wd_hbm, wg_scr, wu_scr, wd_scr,
                                                 col_stage, row_stage, sems)
        else:
            fetch, n_items = (lambda: None), 0
        for s in subs:
            h_scr[s] = _rmsnorm(x_ref[rows[s], :], g_ref[...]).astype(BF16)
        fetch()
        fetch()
        for s in subs:
            gate_up(s, 0)
        for c in range(1, N_FF_CHUNKS):
            fetch()
            fetch()
            fetch()
            for s in subs:
                gate_up(s, c)
                down(s, c - 1)
        fetch()
        assert n_items in (0, 3 * N_FF_CHUNKS)
        for s in subs:
            down(s, N_FF_CHUNKS - 1)
        for s in subs:
            y = x_ref[rows[s], :] + 0.5 * acc_scr[s]
            if final_norm:
                y = _rmsnorm(y, fn_ref[...])
            o_ref[rows[s], :] = y

    if n_steps == 1:
        body(load_weights=True)
    else:
        @pl.when(pl.program_id(0) == 0)
        def _():
            fetch, n_items = _ffn_weight_fetcher(wg_hbm, wu_hbm, wd_hbm, wg_scr, wu_scr, wd_scr,
                                                 col_stage, row_stage, sems)
            for _ in range(n_items):
                fetch()

        body(load_weights=False)


def _ffn(x, g, wg, wu, wd, fn, *, final_norm):
    row_spec = pl.BlockSpec((SUB_TILES * ROW_TILE, D_MODEL), lambda i: (i, 0))
    hbm = pl.BlockSpec(memory_space=pl.ANY)
    n_steps = x.shape[0] // (SUB_TILES * ROW_TILE)
    return pl.pallas_call(
        functools.partial(_ffn_kernel, n_steps=n_steps, final_norm=final_norm),
        grid=(n_steps,),
        in_specs=[row_spec, _const_spec((1, D_MODEL)), hbm, hbm, hbm, _const_spec((1, D_MODEL))],
        out_specs=row_spec,
        out_shape=jax.ShapeDtypeStruct(x.shape, F32),
        scratch_shapes=[pltpu.VMEM((N_FF_CHUNKS, D_MODEL, FF_CHUNK), BF16),
                        pltpu.VMEM((N_FF_CHUNKS, D_MODEL, FF_CHUNK), BF16),
                        pltpu.VMEM((N_FF_CHUNKS, FF_CHUNK, D_MODEL), BF16),
                        pltpu.VMEM((COL_STAGE_SLOTS, D_MODEL, FF_CHUNK), F32),
                        pltpu.VMEM((ROW_STAGE_SLOTS, FF_CHUNK, D_MODEL), F32),
                        pltpu.SemaphoreType.DMA((2, COL_STAGE_SLOTS)),
                        pltpu.VMEM((SUB_TILES, ROW_TILE, D_MODEL), BF16),
                        pltpu.VMEM((SUB_TILES, 2, ROW_TILE, FF_CHUNK), BF16),
                        pltpu.VMEM((SUB_TILES, ROW_TILE, D_MODEL), F32)],
        compiler_params=_params(("arbitrary",)),
        name="ffn_final" if final_norm else "ffn",
    )(x, g, wg, wu, wd, fn)


def _mix_kernel(x_ref, gm_ref, win_ref, cw_ref, cb_ref, wa_ref, ba_ref, wx_ref, bx_ref, lam_ref,
                pw_ref, ps_ref, wout_ref, sconv_ref, slru_ref, spool_ref,
                o_ref, oconv_ref, olru_ref, opool_ref,
                lru_ext, pool_ext, h_scr, x_scr, hn_scr, gate_scr, a_scr, b_scr, y_scr, *, tt, bb, n_sub, pos0):
    rows = tt * bb
    i = pl.program_id(1)

    @pl.when(i == 0)
    def _():
        lru_ext[0:CONV_HIST * bb, :] = sconv_ref[...].reshape(CONV_HIST * bb, LRU_WIDTH)
        pool_ext[0:POOL_HIST * bb, :] = spool_ref[...].reshape(POOL_HIST * bb, POOL_WIDTH)
        h_scr[...] = slru_ref[...]

    half = MXU_TILE

    def prepare_pieces(s):
        n_pieces = max(1, min(4, tt // 8))
        tq = tt // n_pieces

        def piece(q):
            x = jnp.swapaxes(x_ref[:, s * tt + q * tq:s * tt + (q + 1) * tq, :], 0, 1)
            x = x.reshape(tq * bb, D_MODEL)
            x_scr[s, q * tq * bb:(q + 1) * tq * bb, :] = x
            hn_scr[s, q * tq * bb:(q + 1) * tq * bb, :] = _rmsnorm(x, gm_ref[...]).astype(BF16)
        return [functools.partial(piece, q) for q in range(n_pieces)]

    def project_in_pieces(s):
        lru_rows = slice((CONV_HIST + s * tt) * bb, (CONV_HIST + (s + 1) * tt) * bb)
        pool_rows = slice((POOL_HIST + s * tt) * bb, (POOL_HIST + (s + 1) * tt) * bb)

        def piece(n):
            p = _dot(hn_scr[s], win_ref[:, n * half:(n + 1) * half])
            cols = slice((n % 2) * half, (n % 2 + 1) * half)
            if n < 2:
                lru_ext[lru_rows, cols] = p
            elif n < 4:
                gate_scr[s, :, cols] = p
            else:
                pool_ext[pool_rows, cols] = p
        return [functools.partial(piece, n) for n in range(3 * LRU_WIDTH // half)]

    def lru_piece(s, j):
        t0 = s * tt
        lanes = slice(j * half, (j + 1) * half)
        cw = cw_ref[:, lanes]
        u = cb_ref[:, lanes] + sum(lru_ext[(t0 + k) * bb:(t0 + k) * bb + rows, lanes] * cw[k:k + 1, :]
                                   for k in range(CONV_W))
        ub = u.astype(BF16)
        r = jax.nn.sigmoid(_dot(ub, wa_ref[j]) + ba_ref[:, lanes])
        ig = jax.nn.sigmoid(_dot(ub, wx_ref[j]) + bx_ref[:, lanes])
        neg_lam = -lam_ref[:, lanes]
        softplus = jnp.maximum(neg_lam, 0.0) + jnp.log1p(jnp.exp(-jnp.abs(neg_lam)))
        log_a = (-LRU_C * softplus) * r
        a = jnp.exp(log_a)
        a_scr[s, :, lanes] = a
        m = jnp.maximum(1.0 - a * a, 0.0)
        b_scr[s, :, lanes] = jnp.where(m > 0.0, m * lax.rsqrt(m), 0.0) * ig * u
        hcur = h_scr[:, lanes]
        for t in range(tt):
            sl = slice(t * bb, (t + 1) * bb)
            hcur = a_scr[s, sl, lanes] * hcur + b_scr[s, sl, lanes]
            b_scr[s, sl, lanes] = hcur
        h_scr[:, lanes] = hcur
        y_scr[s, :, lanes] = (_gelu_tanh(gate_scr[s, :, lanes]) * b_scr[s, :, lanes]).astype(BF16)

    def pool_piece(s, j):
        t0 = s * tt
        t_idx = lax.broadcasted_iota(jnp.int32, (rows, POOL_GROUP_DIM), 0) // bb
        pos = pos0 + (i * n_sub + s) * tt + t_idx
        groups_per_tile = half // POOL_GROUP_DIM
        deltas = []
        for g in range(j * groups_per_tile, (j + 1) * groups_per_tile):
            win = POOL_WINDOWS[g]
            lanes = slice(g * POOL_GROUP_DIM, (g + 1) * POOL_GROUP_DIM)
            w_sum = pool_ext[t0 * bb:(t0 + POOL_HIST + tt) * bb, lanes]
            span = 1
            while span < win:
                w_sum = w_sum[span * bb:, :] + w_sum[:-span * bb, :]
                span *= 2
            w_sum = w_sum[w_sum.shape[0] - rows:, :]
            cnt = jnp.minimum(pos + 1, win).astype(F32)
            u_pool = pool_ext[(t0 + POOL_HIST) * bb:(t0 + POOL_HIST + tt) * bb, lanes]
            deltas.append(w_sum / cnt - u_pool)
        delta = jnp.concatenate(deltas, axis=1).astype(BF16)
        lanes = slice(j * half, (j + 1) * half)
        y_scr[s, :, LRU_WIDTH + j * half:LRU_WIDTH + (j + 1) * half] = (
            ps_ref[:, lanes] * _dot(delta, pw_ref[j])).astype(BF16)

    def mix_pieces(s):
        return [functools.partial(piece, s, j) for j in range(LRU_WIDTH // half) for piece in (lru_piece, pool_piece)]

    def project_out_pieces(s):
        def piece(n):
            cols = slice(n * half, (n + 1) * half)
            out = x_scr[s, :, cols] + _dot(y_scr[s], wout_ref[:, cols])
            o_ref[:, s * tt:(s + 1) * tt, cols] = jnp.swapaxes(out.reshape(tt, bb, half), 0, 1)
        return [functools.partial(piece, n) for n in range(D_MODEL // half)]

    def emit(*streams):
        for pieces in itertools.zip_longest(*streams):
            for piece in pieces:
                if piece is not None:
                    piece()

    emit(prepare_pieces(0))
    for k in range(1, n_sub + 3):
        mxu, valu = [], []
        if k - 1 < n_sub:
            mxu += project_in_pieces(k - 1)
        if 0 <= k - 3 < n_sub:
            mxu += project_out_pieces(k - 3)
        if k < n_sub:
            valu += prepare_pieces(k)
        if 0 <= k - 2 < n_sub:
            valu += mix_pieces(k - 2)
        emit(mxu, valu)

    new_conv = lru_ext[n_sub * tt * bb:(n_sub * tt + CONV_HIST) * bb, :]
    oconv_ref[...] = new_conv.reshape(CONV_HIST, bb, LRU_WIDTH)
    lru_ext[0:CONV_HIST * bb, :] = new_conv
    new_pool = pool_ext[n_sub * tt * bb:(n_sub * tt + POOL_HIST) * bb, :]
    opool_ref[...] = new_pool.reshape(POOL_HIST, bb, POOL_WIDTH)
    pool_ext[0:POOL_HIST * bb, :] = new_pool
    olru_ref[...] = h_scr[...]


def _mix(x, w, sconv, slru, spool, *, tt, bb, n_sub, pos0):
    n_b, n_t = x.shape[0], x.shape[1]
    rows = tt * bb
    x_spec = pl.BlockSpec((bb, n_sub * tt, D_MODEL), lambda j, i: (j, i, 0))
    conv_spec = pl.BlockSpec((CONV_HIST, bb, LRU_WIDTH), lambda j, i: (0, j, 0))
    lru_spec = pl.BlockSpec((bb, LRU_WIDTH), lambda j, i: (j, 0))
    pool_spec = pl.BlockSpec((POOL_HIST, bb, POOL_WIDTH), lambda j, i: (0, j, 0))
    return pl.pallas_call(
        functools.partial(_mix_kernel, tt=tt, bb=bb, n_sub=n_sub, pos0=pos0),
        grid=(n_b // bb, n_t // (n_sub * tt)),
        in_specs=[x_spec, _const_spec((1, D_MODEL)), _const_spec((D_MODEL, 3 * LRU_WIDTH)),
                  _const_spec((CONV_W, LRU_WIDTH)), _const_spec((1, LRU_WIDTH)),
                  _const_spec((2, MXU_TILE, MXU_TILE)), _const_spec((1, LRU_WIDTH)),
                  _const_spec((2, MXU_TILE, MXU_TILE)), _const_spec((1, LRU_WIDTH)),
                  _const_spec((1, LRU_WIDTH)),
                  _const_spec((2, MXU_TILE, MXU_TILE)), _const_spec((1, POOL_WIDTH)),
                  _const_spec((D_MODEL, D_MODEL)),
                  conv_spec, lru_spec, pool_spec],
        out_specs=[x_spec, conv_spec, lru_spec, pool_spec],
        out_shape=[jax.ShapeDtypeStruct(x.shape, F32),
                   jax.ShapeDtypeStruct((CONV_HIST, n_b, LRU_WIDTH), F32),
                   jax.ShapeDtypeStruct((n_b, LRU_WIDTH), F32),
                   jax.ShapeDtypeStruct((POOL_HIST, n_b, POOL_WIDTH), F32)],
        scratch_shapes=[pltpu.VMEM(((CONV_HIST + n_sub * tt) * bb, LRU_WIDTH), F32),
                        pltpu.VMEM(((POOL_HIST + n_sub * tt) * bb, POOL_WIDTH), F32),
                        pltpu.VMEM((bb, LRU_WIDTH), F32),
                        pltpu.VMEM((n_sub, rows, D_MODEL), F32),
                        pltpu.VMEM((n_sub, rows, D_MODEL), BF16),
                        pltpu.VMEM((n_sub, rows, LRU_WIDTH), F32),
                        pltpu.VMEM((n_sub, rows, LRU_WIDTH), F32),
                        pltpu.VMEM((n_sub, rows, LRU_WIDTH), F32),
                        pltpu.VMEM((n_sub, rows, LRU_WIDTH + POOL_WIDTH), BF16)],
        compiler_params=_params(("parallel", "arbitrary")),
        name="mix",
    )(x, w["mix_norm"], w["w_in"], w["conv_w"], w["conv_b"], w["lru_wa"], w["lru_ba"],
      w["lru_wx"], w["lru_bx"], w["lru_lambda"], w["pool_w"], w["pool_scale"], w["w_out"],
      sconv, slru, spool)


def _memkv_kernel(m_ref, g_ref, wk_ref, wv_ref, k_ref, v_ref, kb_ref, vb_ref):
    n_b = m_ref.shape[0]
    m = _rmsnorm(m_ref[...].reshape(n_b * MEM_LEN, D_MODEL), g_ref[...]).astype(BF16)
    for w_ref, o_ref, ob_ref in ((wk_ref, k_ref, kb_ref), (wv_ref, v_ref, vb_ref)):
        kv = _dot(m, w_ref[...])
        for b in range(n_b):
            kv_b = kv[b * MEM_LEN:(b + 1) * MEM_LEN, :]
            ob_ref[b] = kv_b.astype(BF16)
            for hd in range(XA_HEADS):
                o_ref[b, :, hd, :] = kv_b[:, hd * XA_HEAD_DIM:(hd + 1) * XA_HEAD_DIM]


def _memkv(mem, g, wk, wv):
    n_b = mem.shape[0]
    per_step = ROW_TILE // MEM_LEN
    row_spec = pl.BlockSpec((per_step, MEM_LEN, D_MODEL), lambda i: (i, 0, 0))
    head_spec = pl.BlockSpec((per_step, MEM_LEN, XA_HEADS, XA_HEAD_DIM), lambda i: (i, 0, 0, 0))
    return pl.pallas_call(
        _memkv_kernel,
        grid=(n_b // per_step,),
        in_specs=[row_spec, _const_spec((1, D_MODEL)), _const_spec((D_MODEL, D_MODEL)),
                  _const_spec((D_MODEL, D_MODEL))],
        out_specs=[head_spec, head_spec, row_spec, row_spec],
        out_shape=[jax.ShapeDtypeStruct((n_b, MEM_LEN, XA_HEADS, XA_HEAD_DIM), F32)] * 2
        + [jax.ShapeDtypeStruct((n_b, MEM_LEN, D_MODEL), BF16)] * 2,
        compiler_params=_params(("parallel",)),
        name="memkv",
    )(mem, g, wk, wv)


def _xattn_prompt_kernel(x_ref, g_ref, wq_ref, k_ref, v_ref, wo_ref, o_ref, q_scr, s_scr, att_scr):
    subs = range(SUB_TILES)
    rows = [slice(s * ROW_TILE, (s + 1) * ROW_TILE) for s in subs]
    heads = [slice(hd * XA_HEAD_DIM, (hd + 1) * XA_HEAD_DIM) for hd in range(XA_HEADS)]

    def project_q(s):
        q_scr[s] = _dot(_rmsnorm(x_ref[rows[s], :], g_ref[...]).astype(BF16), wq_ref[...]).astype(BF16)

    def scores(s):
        for hd, cols in enumerate(heads):
            s_scr[s, hd] = lax.dot_general(q_scr[s, :, cols], k_ref[:, cols], (((1,), (1,)), ((), ())),
                                           preferred_element_type=F32) * (XA_HEAD_DIM ** -0.5)

    def attend(s):
        sc = s_scr[s]
        e = jnp.exp(sc - jnp.max(sc, axis=-1, keepdims=True))
        s_scr[s] = e / jnp.sum(e, axis=-1, keepdims=True)
        for hd, cols in enumerate(heads):
            att_scr[s, :, cols] = _dot(s_scr[s, hd].astype(BF16), v_ref[:, cols]).astype(BF16)

    def project_out(s):
        o_ref[rows[s], :] = x_ref[rows[s], :] + _dot(att_scr[s], wo_ref[...])

    stages = (project_q, scores, attend, project_out)
    for k in range(SUB_TILES + len(stages) - 1):
        for depth in reversed(range(len(stages))):
            if 0 <= k - depth < SUB_TILES:
                stages[depth](k - depth)


def _xattn_prompt(x, g, wq, k, v, wo):
    n_b, n_t = x.shape[0], x.shape[1]
    tile = SUB_TILES * ROW_TILE
    x_spec = pl.BlockSpec((None, tile, D_MODEL), lambda b, i: (b, i, 0))
    kv_spec = pl.BlockSpec((None, MEM_LEN, D_MODEL), lambda b, i: (b, 0, 0))
    return pl.pallas_call(
        _xattn_prompt_kernel,
        grid=(n_b, n_t // tile),
        in_specs=[x_spec, _const_spec((1, D_MODEL)), _const_spec((D_MODEL, D_MODEL)),
                  kv_spec, kv_spec, _const_spec((D_MODEL, D_MODEL))],
        out_specs=x_spec,
        out_shape=jax.ShapeDtypeStruct(x.shape, F32),
        scratch_shapes=[pltpu.VMEM((SUB_TILES, ROW_TILE, D_MODEL), BF16),
                        pltpu.VMEM((SUB_TILES, XA_HEADS, ROW_TILE, MEM_LEN), F32),
                        pltpu.VMEM((SUB_TILES, ROW_TILE, D_MODEL), BF16)],
        compiler_params=_params(("parallel", "parallel")),
        name="xattn_prompt",
    )(x, g, wq, k, v, wo)


def _xattn_sample_kernel(x_ref, g_ref, wq_ref, k0_ref, k1_ref, v0_ref, v1_ref, wo_ref, o_ref, att_scr, s_scr,
                         *, bc, tq):
    x = x_ref[...]
    q = _dot(_rmsnorm(x, g_ref[...]).astype(BF16), wq_ref[...])
    halves = [[r.reshape(bc * MEM_LEN * XA_HEADS, LANES) for r in pair]
              for pair in ((k0_ref, k1_ref), (v0_ref, v1_ref))]

    def head(pair, b, hd):
        rows = pl.ds(b * MEM_LEN * XA_HEADS + hd, MEM_LEN, stride=XA_HEADS)
        return jnp.concatenate([r[rows, :] for r in pair], axis=1).astype(BF16)

    pairs = [(b, hd) for b in range(bc) for hd in range(XA_HEADS)]
    for n, (b, hd) in enumerate(pairs):
        qh = q[b * tq:(b + 1) * tq, hd * XA_HEAD_DIM:(hd + 1) * XA_HEAD_DIM].astype(BF16)
        s_scr[n * tq:(n + 1) * tq, :] = lax.dot_general(
            qh, head(halves[0], b, hd), (((1,), (1,)), ((), ())),
            preferred_element_type=F32) * (XA_HEAD_DIM ** -0.5)
    s = s_scr[...]
    e = jnp.exp(s - jnp.max(s, axis=-1, keepdims=True))
    s_scr[...] = e / jnp.sum(e, axis=-1, keepdims=True)
    for n, (b, hd) in enumerate(pairs):
        p = s_scr[n * tq:(n + 1) * tq, :].astype(BF16)
        att_scr[b * tq:(b + 1) * tq, hd * XA_HEAD_DIM:(hd + 1) * XA_HEAD_DIM] = _dot(p, head(halves[1], b, hd))
    o_ref[...] = x + _dot(att_scr[...].astype(BF16), wo_ref[...])


def _xattn_sample(x, g, wq, k, v, wo, *, bc):
    n_b = k.shape[0]
    tq = x.shape[0] // n_b
    x_spec = pl.BlockSpec((bc * tq, D_MODEL), lambda j: (j, 0))
    kv_specs = [pl.BlockSpec((bc, MEM_LEN, XA_HEADS, LANES), functools.partial(lambda half, j: (j, 0, 0, half), half))
                for half in range(XA_HEAD_DIM // LANES)]
    return pl.pallas_call(
        functools.partial(_xattn_sample_kernel, bc=bc, tq=tq),
        grid=(n_b // bc,),
        in_specs=[x_spec, _const_spec((1, D_MODEL)), _const_spec((D_MODEL, D_MODEL)),
                  *kv_specs, *kv_specs, _const_spec((D_MODEL, D_MODEL))],
        out_specs=x_spec,
        out_shape=jax.ShapeDtypeStruct(x.shape, F32),
        scratch_shapes=[pltpu.VMEM((bc * tq, D_MODEL), F32),
                        pltpu.VMEM((bc * XA_HEADS * tq, MEM_LEN), F32)],
        compiler_params=_params(("parallel",)),
        name="xattn_sample",
    )(x, g, wq, k, k, v, v, wo)


def _block_diag_tiles(w):
    groups, dg, _ = w.shape
    per_tile = MXU_TILE // dg
    w = w.reshape(groups // per_tile, per_tile, dg, dg)
    eye = jnp.eye(per_tile, dtype=w.dtype)
    tiles = jnp.einsum("npij,pq->npiqj", w, eye)
    return tiles.reshape(groups // per_tile, MXU_TILE, MXU_TILE).astype(BF16)


def _row(v):
    return v.reshape(1, -1).astype(F32)


def _layer(x_prompt, x_sample, mem_prompt, sconv, slru, spool, cache_k, cache_v, p, final_norm):
    n_b, n_t, _ = x_prompt.shape
    s_b, s_t, _ = x_sample.shape
    ffn1 = (_row(p["ffn1_norm"]), p["ffn1_w_gate"], p["ffn1_w_up"], p["ffn1_w_down"], _row(final_norm))
    ffn2 = (_row(p["ffn2_norm"]), p["ffn2_w_gate"], p["ffn2_w_up"], p["ffn2_w_down"], _row(final_norm))
    mixw = {
        "mix_norm": _row(p["mix_norm"]), "w_in": p["w_in"].astype(BF16),
        "conv_w": p["conv_w"].astype(F32), "conv_b": _row(p["conv_b"]),
        "lru_wa": _block_diag_tiles(p["lru_wa"]), "lru_ba": _row(p["lru_ba"]),
        "lru_wx": _block_diag_tiles(p["lru_wx"]), "lru_bx": _row(p["lru_bx"]),
        "lru_lambda": _row(p["lru_lambda"]),
        "pool_w": _block_diag_tiles(p["pool_w"]), "pool_scale": _row(p["pool_scale"]),
        "w_out": p["w_out"].astype(BF16),
    }
    gx, wq, wo = _row(p["xattn_norm"]), p["xattn_wq"].astype(BF16), p["xattn_wo"].astype(BF16)

    mk, mv, mk_bf, mv_bf = _memkv(mem_prompt, _row(p["mem_norm"]),
                                  p["xattn_wk"].astype(BF16), p["xattn_wv"].astype(BF16))

    xp = _ffn(x_prompt.reshape(n_b * n_t, D_MODEL), *ffn1, final_norm=False)
    xs = _ffn(x_sample.reshape(s_b * s_t, D_MODEL), *ffn1, final_norm=False)

    xp, p_conv, p_lru, p_pool = _mix(
        xp.reshape(n_b, n_t, D_MODEL), mixw, jnp.zeros((CONV_HIST, n_b, LRU_WIDTH), F32),
        jnp.zeros((n_b, LRU_WIDTH), F32), jnp.zeros((POOL_HIST, n_b, POOL_WIDTH), F32),
        tt=ROW_TILE // n_b, bb=n_b, n_sub=SUB_TILES, pos0=0)
    xp = _xattn_prompt(xp, gx, wq, mk_bf, mv_bf, wo)

    xs, s_conv, s_lru, s_pool = _mix(
        xs.reshape(s_b, s_t, D_MODEL), mixw, jnp.swapaxes(sconv, 0, 1), slru, jnp.swapaxes(spool, 0, 1),
        tt=s_t, bb=ROW_TILE // s_t, n_sub=1, pos0=PAST_LEN)
    xs = _xattn_sample(xs.reshape(s_b * s_t, D_MODEL), gx, wq, cache_k, cache_v, wo, bc=8)

    yp = _ffn(xp.reshape(n_b * n_t, D_MODEL), *ffn2, final_norm=True).reshape(n_b, n_t, D_MODEL)
    ys = _ffn(xs, *ffn2, final_norm=True).reshape(s_b, s_t, D_MODEL)

    states = (jnp.swapaxes(p_conv, 0, 1), p_lru, jnp.swapaxes(p_pool, 0, 1), mk, mv,
              jnp.swapaxes(s_conv, 0, 1), s_lru, jnp.swapaxes(s_pool, 0, 1))
    return yp, ys, states


def kernel(x_prompt, x_sample, mem_prompt, state_conv, state_lru, state_pool, cache_mem_k, cache_mem_v, ffn1_norm, ffn1_w_gate, ffn1_w_up, ffn1_w_down, mix_norm, w_in, conv_w, conv_b, lru_wa, lru_ba, lru_wx, lru_bx, lru_lambda, pool_w, pool_scale, w_out, xattn_norm, mem_norm, xattn_wq, xattn_wk, xattn_wv, xattn_wo, ffn2_norm, ffn2_w_gate, ffn2_w_up, ffn2_w_down, final_norm):
    depth = ffn1_norm.shape[0]
    assert depth == 1, "the final RMSNorm is fused into the last layer's second FFN"
    names = ("ffn1_norm", "ffn1_w_gate", "ffn1_w_up", "ffn1_w_down", "mix_norm", "w_in", "conv_w", "conv_b",
             "lru_wa", "lru_ba", "lru_wx", "lru_bx", "lru_lambda", "pool_w", "pool_scale", "w_out",
             "xattn_norm", "mem_norm", "xattn_wq", "xattn_wk", "xattn_wv", "xattn_wo",
             "ffn2_norm", "ffn2_w_gate", "ffn2_w_up", "ffn2_w_down")
    stacked = (ffn1_norm, ffn1_w_gate, ffn1_w_up, ffn1_w_down, mix_norm, w_in, conv_w, conv_b,
               lru_wa, lru_ba, lru_wx, lru_bx, lru_lambda, pool_w, pool_scale, w_out,
               xattn_norm, mem_norm, xattn_wq, xattn_wk, xattn_wv, xattn_wo,
               ffn2_norm, ffn2_w_gate, ffn2_w_up, ffn2_w_down)
    p = {n: a[0] for n, a in zip(names, stacked)}
    yp, ys, st = _layer(x_prompt, x_sample, mem_prompt, state_conv[0], state_lru[0], state_pool[0],
                        cache_mem_k[0], cache_mem_v[0], p, final_norm)
    p_conv, p_lru, p_pool, p_mk, p_mv, s_conv, s_lru, s_pool = (s[None] for s in st)
    return (yp, ys, p_conv, p_lru, p_pool, p_mk, p_mv, s_conv, s_lru, s_pool)
```

```python
import functools
import itertools

import jax
import jax.numpy as jnp
from jax import lax
from jax.experimental import pallas as pl
from jax.experimental.pallas import tpu as pltpu

F32 = jnp.float32
BF16 = jnp.bfloat16

D_MODEL = 1024
D_FF = 2816
LRU_WIDTH = 512
LRU_HEADS = 8
LRU_C = 8.0
CONV_W = 4
POOL_WIDTH = 512
POOL_WINDOWS = (2, 4, 8, 16)
POOL_GROUP_DIM = POOL_WIDTH // len(POOL_WINDOWS)
POOL_HIST = max(POOL_WINDOWS) - 1
CONV_HIST = CONV_W - 1
MEM_LEN = 256
XA_HEADS = 4
XA_HEAD_DIM = D_MODEL // XA_HEADS
EPS = 1e-6
PAST_LEN = 16384

MXU_TILE = 256
LANES = 128
assert XA_HEAD_DIM == 2 * LANES
FF_CHUNK = MXU_TILE
N_FF_CHUNKS = D_FF // FF_CHUNK
ROW_TILE = 512
SUB_TILES = 2
WEIGHT_LOOKAHEAD = 4
COL_STAGE_SLOTS = 4
ROW_STAGE_SLOTS = 2
SAMPLE_PAIR = 2
VMEM_LIMIT_BYTES = 56 * 1024 * 1024


def _rmsnorm(x, g):
    var = jnp.mean(x * x, axis=-1, keepdims=True)
    return x * lax.rsqrt(var + EPS) * g


def _gelu_tanh(x):
    c0 = (2.0 / jnp.pi) ** 0.5
    half_x = 0.5 * x
    return half_x + half_x * jnp.tanh(x * (c0 + (c0 * 0.044715) * (x * x)))


def _dot(a, b):
    return jnp.dot(a, b, preferred_element_type=F32)


def _const_spec(shape):
    zeros = (0,) * len(shape)
    return pl.BlockSpec(shape, lambda *_: zeros, pipeline_mode=pl.Buffered(1))


def _params(semantics):
    return pltpu.CompilerParams(dimension_semantics=semantics, vmem_limit_bytes=VMEM_LIMIT_BYTES)


def _ffn_weight_fetcher(wg_hbm, wu_hbm, wd_hbm, wg_scr, wu_scr, wd_scr, col_stage, row_stage, sems):
    stages = (col_stage, row_stage)

    def chunk(c):
        cols = pl.ds(c * FF_CHUNK, FF_CHUNK)
        return {"gate": (wg_hbm.at[:, cols], 0, wg_scr.at[c]), "up": (wu_hbm.at[:, cols], 0, wu_scr.at[c]),
                "down": (wd_hbm.at[cols, :], 1, wd_scr.at[c])}

    items = [chunk(0)["gate"], chunk(0)["up"]]
    for c in range(1, N_FF_CHUNKS):
        items += [chunk(c)["gate"], chunk(c)["up"], chunk(c - 1)["down"]]
    items.append(chunk(N_FF_CHUNKS - 1)["down"])
    uses = [0, 0]
    plan = []
    for src, kind, dst in items:
        slot = uses[kind] % stages[kind].shape[0]
        uses[kind] += 1
        staged = stages[kind].at[slot]
        plan.append((pltpu.make_async_copy(src, staged, sems.at[kind, slot]), staged, dst, (kind, slot)))
    slot_free_after = {}
    state = {"next": 0}

    def start(m):
        key = plan[m][3]
        assert slot_free_after.get(key, -1) < state["next"], "staging slot reused before it was converted"
        slot_free_after[key] = m
        plan[m][0].start()

    for m in range(min(WEIGHT_LOOKAHEAD, len(plan))):
        start(m)

    def fetch():
        n = state["next"]
        copy, staged, dst, _ = plan[n]
        copy.wait()
        dst[...] = staged[...].astype(BF16)
        state["next"] = n + 1
        if n + WEIGHT_LOOKAHEAD < len(plan):
            start(n + WEIGHT_LOOKAHEAD)

    return fetch, len(plan)


def _ffn_kernel(x_ref, g_ref, wg_hbm, wu_hbm, wd_hbm, fn_ref, o_ref,
                wg_scr, wu_scr, wd_scr, col_stage, row_stage, sems, h_scr, a_scr, acc_scr,
                *, n_steps, final_norm):
    subs = range(SUB_TILES)
    rows = [slice(s * ROW_TILE, (s + 1) * ROW_TILE) for s in subs]

    def gate_up(s, c):
        h = h_scr[s]
        g = _dot(h, wg_scr[c])
        u = _dot(h, wu_scr[c])
        a_scr[s, c % 2] = (g * jax.nn.sigmoid(g) * u).astype(BF16)

    def down(s, c):
        d = _dot(a_scr[s, c % 2], wd_scr[c])
        if c == 0:
            acc_scr[s] = d
        else:
            acc_scr[s] += d

    def body(load_weights):
        if load_weights:
            fetch, n_items = _ffn_weight_fetcher(wg_hbm, wu_hbm, wd_hbm, wg_scr, wu_scr, wd_scr,
                                                 col_stage, row_stage, sems)
        else:
            fetch, n_items = (lambda: None), 0
        fetch()
        fetch()
        for s in subs:
            h_scr[s] = _rmsnorm(x_ref[rows[s], :], g_ref[...]).astype(BF16)
            gate_up(s, 0)
        for c in range(1, N_FF_CHUNKS):
            fetch()
            fetch()
            fetch()
            for s in subs:
                gate_up(s, c)
                down(s, c - 1)
        fetch()
        assert n_items in (0, 3 * N_FF_CHUNKS)
        for s in subs:
            down(s, N_FF_CHUNKS - 1)
            y = x_ref[rows[s], :] + 0.5 * acc_scr[s]
            if final_norm:
                y = _rmsnorm(y, fn_ref[...])
            o_ref[rows[s], :] = y

    if n_steps == 1:
        body(load_weights=True)
    else:
        @pl.when(pl.program_id(0) == 0)
        def _():
            fetch, n_items = _ffn_weight_fetcher(wg_hbm, wu_hbm, wd_hbm, wg_scr, wu_scr, wd_scr,
                                                 col_stage, row_stage, sems)
            for _ in range(n_items):
                fetch()

        body(load_weights=False)


def _ffn(x, g, wg, wu, wd, fn, *, final_norm):
    row_spec = pl.BlockSpec((SUB_TILES * ROW_TILE, D_MODEL), lambda i: (i, 0))
    hbm = pl.BlockSpec(memory_space=pl.ANY)
    n_steps = x.shape[0] // (SUB_TILES * ROW_TILE)
    return pl.pallas_call(
        functools.partial(_ffn_kernel, n_steps=n_steps, final_norm=final_norm),
        grid=(n_steps,),
        in_specs=[row_spec, _const_spec((1, D_MODEL)), hbm, hbm, hbm, _const_spec((1, D_MODEL))],
        out_specs=row_spec,
        out_shape=jax.ShapeDtypeStruct(x.shape, F32),
        scratch_shapes=[pltpu.VMEM((N_FF_CHUNKS, D_MODEL, FF_CHUNK), BF16),
                        pltpu.VMEM((N_FF_CHUNKS, D_MODEL, FF_CHUNK), BF16),
                        pltpu.VMEM((N_FF_CHUNKS, FF_CHUNK, D_MODEL), BF16),
                        pltpu.VMEM((COL_STAGE_SLOTS, D_MODEL, FF_CHUNK), F32),
                        pltpu.VMEM((ROW_STAGE_SLOTS, FF_CHUNK, D_MODEL), F32),
                        pltpu.SemaphoreType.DMA((2, COL_STAGE_SLOTS)),
                        pltpu.VMEM((SUB_TILES, ROW_TILE, D_MODEL), BF16),
                        pltpu.VMEM((SUB_TILES, 2, ROW_TILE, FF_CHUNK), BF16),
                        pltpu.VMEM((SUB_TILES, ROW_TILE, D_MODEL), F32)],
        compiler_params=_params(("arbitrary",)),
        name="ffn_final" if final_norm else "ffn",
    )(x, g, wg, wu, wd, fn)


def _mix_kernel(x_ref, gm_ref, win_ref, cw_ref, cb_ref, wa_ref, ba_ref, wx_ref, bx_ref, lam_ref,
                pw_ref, ps_ref, wout_ref, sconv_ref, slru_ref, spool_ref,
                o_ref, oconv_ref, olru_ref, opool_ref,
                lru_ext, pool_ext, h_scr, x_scr, hn_scr, gate_scr, a_scr, b_scr, y_scr, *, tt, bb, n_sub, pos0):
    rows = tt * bb
    i = pl.program_id(1)

    @pl.when(i == 0)
    def _():
        lru_ext[0:CONV_HIST * bb, :] = sconv_ref[...].reshape(CONV_HIST * bb, LRU_WIDTH)
        pool_ext[0:POOL_HIST * bb, :] = spool_ref[...].reshape(POOL_HIST * bb, POOL_WIDTH)
        h_scr[...] = slru_ref[...]

    half = MXU_TILE

    def prepare_pieces(s):
        n_pieces = max(1, min(4, tt // 8))
        tq = tt // n_pieces

        def piece(q):
            x = jnp.swapaxes(x_ref[:, s * tt + q * tq:s * tt + (q + 1) * tq, :], 0, 1)
            x = x.reshape(tq * bb, D_MODEL)
            x_scr[s, q * tq * bb:(q + 1) * tq * bb, :] = x
            hn_scr[s, q * tq * bb:(q + 1) * tq * bb, :] = _rmsnorm(x, gm_ref[...]).astype(BF16)
        return [functools.partial(piece, q) for q in range(n_pieces)]

    def project_in_pieces(s):
        lru_rows = slice((CONV_HIST + s * tt) * bb, (CONV_HIST + (s + 1) * tt) * bb)
        pool_rows = slice((POOL_HIST + s * tt) * bb, (POOL_HIST + (s + 1) * tt) * bb)

        def piece(n):
            p = _dot(hn_scr[s], win_ref[:, n * half:(n + 1) * half])
            cols = slice((n % 2) * half, (n % 2 + 1) * half)
            if n < 2:
                lru_ext[lru_rows, cols] = p
            elif n < 4:
                gate_scr[s, :, cols] = p
            else:
                pool_ext[pool_rows, cols] = p
        return [functools.partial(piece, n) for n in range(3 * LRU_WIDTH // half)]

    def lru_piece(s, j):
        t0 = s * tt
        lanes = slice(j * half, (j + 1) * half)
        cw = cw_ref[:, lanes]
        u = cb_ref[:, lanes] + sum(lru_ext[(t0 + k) * bb:(t0 + k) * bb + rows, lanes] * cw[k:k + 1, :]
                                   for k in range(CONV_W))
        ub = u.astype(BF16)
        r = jax.nn.sigmoid(_dot(ub, wa_ref[j]) + ba_ref[:, lanes])
        ig = jax.nn.sigmoid(_dot(ub, wx_ref[j]) + bx_ref[:, lanes])
        neg_lam = -lam_ref[:, lanes]
        softplus = jnp.maximum(neg_lam, 0.0) + jnp.log1p(jnp.exp(-jnp.abs(neg_lam)))
        log_a = (-LRU_C * softplus) * r
        a = jnp.exp(log_a)
        a_scr[s, :, lanes] = a
        m = jnp.maximum(1.0 - a * a, 0.0)
        b_scr[s, :, lanes] = jnp.where(m > 0.0, m * lax.rsqrt(m), 0.0) * ig * u
        hcur = h_scr[:, lanes]
        for t in range(tt):
            sl = slice(t * bb, (t + 1) * bb)
            hcur = a_scr[s, sl, lanes] * hcur + b_scr[s, sl, lanes]
            b_scr[s, sl, lanes] = hcur
        h_scr[:, lanes] = hcur
        y_scr[s, :, lanes] = (_gelu_tanh(gate_scr[s, :, lanes]) * b_scr[s, :, lanes]).astype(BF16)

    def pool_piece(s, j):
        t0 = s * tt
        t_idx = lax.broadcasted_iota(jnp.int32, (rows, POOL_GROUP_DIM), 0) // bb
        pos = pos0 + (i * n_sub + s) * tt + t_idx
        groups_per_tile = half // POOL_GROUP_DIM
        deltas = []
        for g in range(j * groups_per_tile, (j + 1) * groups_per_tile):
            win = POOL_WINDOWS[g]
            lanes = slice(g * POOL_GROUP_DIM, (g + 1) * POOL_GROUP_DIM)
            w_sum = pool_ext[t0 * bb:(t0 + POOL_HIST + tt) * bb, lanes]
            span = 1
            while span < win:
                w_sum = w_sum[span * bb:, :] + w_sum[:-span * bb, :]
                span *= 2
            w_sum = w_sum[w_sum.shape[0] - rows:, :]
            cnt = jnp.minimum(pos + 1, win).astype(F32)
            u_pool = pool_ext[(t0 + POOL_HIST) * bb:(t0 + POOL_HIST + tt) * bb, lanes]
            deltas.append(w_sum / cnt - u_pool)
        delta = jnp.concatenate(deltas, axis=1).astype(BF16)
        lanes = slice(j * half, (j + 1) * half)
        y_scr[s, :, LRU_WIDTH + j * half:LRU_WIDTH + (j + 1) * half] = (
            ps_ref[:, lanes] * _dot(delta, pw_ref[j])).astype(BF16)

    def mix_pieces(s):
        return [functools.partial(piece, s, j) for j in range(LRU_WIDTH // half) for piece in (lru_piece, pool_piece)]

    def project_out_pieces(s):
        def piece(n):
            cols = slice(n * half, (n + 1) * half)
            out = x_scr[s, :, cols] + _dot(y_scr[s], wout_ref[:, cols])
            o_ref[:, s * tt:(s + 1) * tt, cols] = jnp.swapaxes(out.reshape(tt, bb, half), 0, 1)
        return [functools.partial(piece, n) for n in range(D_MODEL // half)]

    def emit(*streams):
        for pieces in itertools.zip_longest(*streams):
            for piece in pieces:
                if piece is not None:
                    piece()

    emit(prepare_pieces(0))
    for k in range(1, n_sub + 3):
        mxu, valu = [], []
        if k - 1 < n_sub:
            mxu += project_in_pieces(k - 1)
        if 0 <= k - 3 < n_sub:
            mxu += project_out_pieces(k - 3)
        if k < n_sub:
            valu += prepare_pieces(k)
        if 0 <= k - 2 < n_sub:
            valu += mix_pieces(k - 2)
        emit(mxu, valu)

    new_conv = lru_ext[n_sub * tt * bb:(n_sub * tt + CONV_HIST) * bb, :]
    oconv_ref[...] = new_conv.reshape(CONV_HIST, bb, LRU_WIDTH)
    lru_ext[0:CONV_HIST * bb, :] = new_conv
    new_pool = pool_ext[n_sub * tt * bb:(n_sub * tt + POOL_HIST) * bb, :]
    opool_ref[...] = new_pool.reshape(POOL_HIST, bb, POOL_WIDTH)
    pool_ext[0:POOL_HIST * bb, :] = new_pool
    olru_ref[...] = h_scr[...]


def _mix(x, w, sconv, slru, spool, *, tt, bb, n_sub, pos0):
    n_b, n_t = x.shape[0], x.shape[1]
    rows = tt * bb
    x_spec = pl.BlockSpec((bb, n_sub * tt, D_MODEL), lambda j, i: (j, i, 0))
    conv_spec = pl.BlockSpec((CONV_HIST, bb, LRU_WIDTH), lambda j, i: (0, j, 0))
    lru_spec = pl.BlockSpec((bb, LRU_WIDTH), lambda j, i: (j, 0))
    pool_spec = pl.BlockSpec((POOL_HIST, bb, POOL_WIDTH), lambda j, i: (0, j, 0))
    return pl.pallas_call(
        functools.partial(_mix_kernel, tt=tt, bb=bb, n_sub=n_sub, pos0=pos0),
        grid=(n_b // bb, n_t // (n_sub * tt)),
        in_specs=[x_spec, _const_spec((1, D_MODEL)), _const_spec((D_MODEL, 3 * LRU_WIDTH)),
                  _const_spec((CONV_W, LRU_WIDTH)), _const_spec((1, LRU_WIDTH)),
                  _const_spec((2, MXU_TILE, MXU_TILE)), _const_spec((1, LRU_WIDTH)),
                  _const_spec((2, MXU_TILE, MXU_TILE)), _const_spec((1, LRU_WIDTH)),
                  _const_spec((1, LRU_WIDTH)),
                  _const_spec((2, MXU_TILE, MXU_TILE)), _const_spec((1, POOL_WIDTH)),
                  _const_spec((D_MODEL, D_MODEL)),
                  conv_spec, lru_spec, pool_spec],
        out_specs=[x_spec, conv_spec, lru_spec, pool_spec],
        out_shape=[jax.ShapeDtypeStruct(x.shape, F32),
                   jax.ShapeDtypeStruct((CONV_HIST, n_b, LRU_WIDTH), F32),
                   jax.ShapeDtypeStruct((n_b, LRU_WIDTH), F32),
                   jax.ShapeDtypeStruct((POOL_HIST, n_b, POOL_WIDTH), F32)],
        scratch_shapes=[pltpu.VMEM(((CONV_HIST + n_sub * tt) * bb, LRU_WIDTH), F32),
                        pltpu.VMEM(((POOL_HIST + n_sub * tt) * bb, POOL_WIDTH), F32),
                        pltpu.VMEM((bb, LRU_WIDTH), F32),
                        pltpu.VMEM((n_sub, rows, D_MODEL), F32),
                        pltpu.VMEM((n_sub, rows, D_MODEL), BF16),
                        pltpu.VMEM((n_sub, rows, LRU_WIDTH), F32),
                        pltpu.VMEM((n_sub, rows, LRU_WIDTH), F32),
                        pltpu.VMEM((n_sub, rows, LRU_WIDTH), F32),
                        pltpu.VMEM((n_sub, rows, LRU_WIDTH + POOL_WIDTH), BF16)],
        compiler_params=_params(("parallel", "arbitrary")),
        name="mix",
    )(x, w["mix_norm"], w["w_in"], w["conv_w"], w["conv_b"], w["lru_wa"], w["lru_ba"],
      w["lru_wx"], w["lru_bx"], w["lru_lambda"], w["pool_w"], w["pool_scale"], w["w_out"],
      sconv, slru, spool)


def _memkv_kernel(m_ref, g_ref, wk_ref, wv_ref, k_ref, v_ref, kb_ref, vb_ref):
    n_b = m_ref.shape[0]
    m = _rmsnorm(m_ref[...].reshape(n_b * MEM_LEN, D_MODEL), g_ref[...]).astype(BF16)
    for w_ref, o_ref, ob_ref in ((wk_ref, k_ref, kb_ref), (wv_ref, v_ref, vb_ref)):
        kv = _dot(m, w_ref[...])
        for b in range(n_b):
            kv_b = kv[b * MEM_LEN:(b + 1) * MEM_LEN, :]
            ob_ref[b] = kv_b.astype(BF16)
            for hd in range(XA_HEADS):
                o_ref[b, :, hd, :] = kv_b[:, hd * XA_HEAD_DIM:(hd + 1) * XA_HEAD_DIM]


def _memkv(mem, g, wk, wv):
    n_b = mem.shape[0]
    per_step = ROW_TILE // MEM_LEN
    row_spec = pl.BlockSpec((per_step, MEM_LEN, D_MODEL), lambda i: (i, 0, 0))
    head_spec = pl.BlockSpec((per_step, MEM_LEN, XA_HEADS, XA_HEAD_DIM), lambda i: (i, 0, 0, 0))
    return pl.pallas_call(
        _memkv_kernel,
        grid=(n_b // per_step,),
        in_specs=[row_spec, _const_spec((1, D_MODEL)), _const_spec((D_MODEL, D_MODEL)),
                  _const_spec((D_MODEL, D_MODEL))],
        out_specs=[head_spec, head_spec, row_spec, row_spec],
        out_shape=[jax.ShapeDtypeStruct((n_b, MEM_LEN, XA_HEADS, XA_HEAD_DIM), F32)] * 2
        + [jax.ShapeDtypeStruct((n_b, MEM_LEN, D_MODEL), BF16)] * 2,
        compiler_params=_params(("parallel",)),
        name="memkv",
    )(mem, g, wk, wv)


def _xattn_kernel(x_ref, g_ref, wq_ref, k_ref, v_ref, wo_ref, qs_ref, kc_hbm, vc_hbm, o_ref, atts_ref,
                  q_scr, s_scr, att_scr, ring, sems, ss_scr, *, tq, per_step):
    subs = range(SUB_TILES)
    rows = [slice(s * ROW_TILE, (s + 1) * ROW_TILE) for s in subs]
    heads = [slice(hd * XA_HEAD_DIM, (hd + 1) * XA_HEAD_DIM) for hd in range(XA_HEADS)]
    scale = XA_HEAD_DIM ** -0.5

    def project_q(s):
        q_scr[s] = _dot(_rmsnorm(x_ref[rows[s], :], g_ref[...]).astype(BF16), wq_ref[...]).astype(BF16)

    def scores(s):
        for hd, cols in enumerate(heads):
            s_scr[s, hd] = lax.dot_general(q_scr[s, :, cols], k_ref[:, cols], (((1,), (1,)), ((), ())),
                                           preferred_element_type=F32) * scale

    def attend(s):
        sc = s_scr[s]
        e = jnp.exp(sc - jnp.max(sc, axis=-1, keepdims=True))
        s_scr[s] = e / jnp.sum(e, axis=-1, keepdims=True)
        for hd, cols in enumerate(heads):
            att_scr[s, :, cols] = _dot(s_scr[s, hd].astype(BF16), v_ref[:, cols]).astype(BF16)

    def project_out(s):
        o_ref[rows[s], :] = x_ref[rows[s], :] + _dot(att_scr[s], wo_ref[...])

    step = pl.program_id(0) * pl.num_programs(1) + pl.program_id(1)
    n_steps = pl.num_programs(0) * pl.num_programs(1)
    n_pairs = per_step // SAMPLE_PAIR
    assert n_pairs % 2 == 0, "ring slots must line up across grid steps"

    def pair_copies(slot, first_seq):
        sources = ((kc_hbm, 0), (kc_hbm, 1), (vc_hbm, 0), (vc_hbm, 1))
        return [pltpu.make_async_copy(
            cache.at[pl.ds(first_seq, SAMPLE_PAIR), :, :, pl.ds(half * LANES, LANES)],
            ring.at[slot, j], sems.at[slot, j]) for j, (cache, half) in enumerate(sources)]

    @pl.when(step == 0)
    def _():
        for copy in pair_copies(0, 0):
            copy.start()

    def pair_views(p):
        views = [ring.at[p % 2, j].reshape(SAMPLE_PAIR * MEM_LEN * XA_HEADS, LANES) for j in range(4)]

        def head(kv, b, hd):
            r = pl.ds(b * MEM_LEN * XA_HEADS + hd, MEM_LEN, stride=XA_HEADS)
            return jnp.concatenate([views[2 * kv][r, :], views[2 * kv + 1][r, :]], axis=1).astype(BF16)
        return head

    items = [(b, hd) for b in range(SAMPLE_PAIR) for hd in range(XA_HEADS)]

    def sample_scores(p):
        slot = p % 2
        first_seq = step * per_step + p * SAMPLE_PAIR
        if p + 1 < n_pairs:
            for copy in pair_copies(1 - slot, first_seq + SAMPLE_PAIR):
                copy.start()
        else:
            @pl.when(step + 1 < n_steps)
            def _():
                for copy in pair_copies(1 - slot, first_seq + SAMPLE_PAIR):
                    copy.start()
        for copy in pair_copies(slot, first_seq):
            copy.wait()
        head = pair_views(p)
        for n, (b, hd) in enumerate(items):
            r0 = (p * SAMPLE_PAIR + b) * tq
            ss_scr[slot, n * tq:(n + 1) * tq, :] = lax.dot_general(
                qs_ref[r0:r0 + tq, heads[hd]].astype(BF16), head(0, b, hd), (((1,), (1,)), ((), ())),
                preferred_element_type=F32) * scale

    def sample_attend(p):
        slot = p % 2
        head = pair_views(p)
        sc = ss_scr[slot]
        e = jnp.exp(sc - jnp.max(sc, axis=-1, keepdims=True))
        ss_scr[slot] = e / jnp.sum(e, axis=-1, keepdims=True)
        for n, (b, hd) in enumerate(items):
            r0 = (p * SAMPLE_PAIR + b) * tq
            atts_ref[r0:r0 + tq, heads[hd]] = _dot(ss_scr[slot, n * tq:(n + 1) * tq, :].astype(BF16),
                                                  head(1, b, hd))

    stages = (project_q, scores, attend, project_out)
    n_rounds = SUB_TILES + len(stages) - 1
    assert n_pairs + 1 <= n_rounds
    for k in range(n_rounds):
        for depth in reversed(range(len(stages))):
            if 0 <= k - depth < SUB_TILES:
                stages[depth](k - depth)
        if 1 <= k <= n_pairs:
            sample_attend(k - 1)
        if k < n_pairs:
            sample_scores(k)


def _xattn(x, g, wq, k, v, wo, qs, cache_k, cache_v):
    n_b, n_t = x.shape[0], x.shape[1]
    tile = SUB_TILES * ROW_TILE
    n_i = n_t // tile
    n_seq = cache_k.shape[0]
    tq = qs.shape[0] // n_seq
    per_step = n_seq // (n_b * n_i)
    assert per_step * n_b * n_i == n_seq
    x_spec = pl.BlockSpec((None, tile, D_MODEL), lambda b, i: (b, i, 0))
    kv_spec = pl.BlockSpec((None, MEM_LEN, D_MODEL), lambda b, i: (b, 0, 0))
    qs_spec = pl.BlockSpec((per_step * tq, D_MODEL), lambda b, i: (b * n_i + i, 0))
    hbm = pl.BlockSpec(memory_space=pl.ANY)
    return pl.pallas_call(
        functools.partial(_xattn_kernel, tq=tq, per_step=per_step),
        grid=(n_b, n_i),
        in_specs=[x_spec, _const_spec((1, D_MODEL)), _const_spec((D_MODEL, D_MODEL)),
                  kv_spec, kv_spec, _const_spec((D_MODEL, D_MODEL)), qs_spec, hbm, hbm],
        out_specs=[x_spec, qs_spec],
        out_shape=[jax.ShapeDtypeStruct(x.shape, F32), jax.ShapeDtypeStruct(qs.shape, F32)],
        scratch_shapes=[pltpu.VMEM((SUB_TILES, ROW_TILE, D_MODEL), BF16),
                        pltpu.VMEM((SUB_TILES, XA_HEADS, ROW_TILE, MEM_LEN), F32),
                        pltpu.VMEM((SUB_TILES, ROW_TILE, D_MODEL), BF16),
                        pltpu.VMEM((2, 4, SAMPLE_PAIR, MEM_LEN, XA_HEADS, LANES), F32),
                        pltpu.SemaphoreType.DMA((2, 4)),
                        pltpu.VMEM((2, SAMPLE_PAIR * XA_HEADS * tq, MEM_LEN), F32)],
        compiler_params=_params(("arbitrary", "arbitrary")),
        name="xattn",
    )(x, g, wq, k, v, wo, qs, cache_k, cache_v)


def _sample_q_kernel(x_ref, g_ref, wq_ref, q_ref):
    q_ref[...] = _dot(_rmsnorm(x_ref[...], g_ref[...]).astype(BF16), wq_ref[...])


def _sample_out_kernel(x_ref, a_ref, wo_ref, o_ref):
    o_ref[...] = x_ref[...] + _dot(a_ref[...].astype(BF16), wo_ref[...])


def _sample_q(x, g, wq):
    row_spec = pl.BlockSpec((ROW_TILE, D_MODEL), lambda i: (i, 0))
    return pl.pallas_call(
        _sample_q_kernel, grid=(x.shape[0] // ROW_TILE,),
        in_specs=[row_spec, _const_spec((1, D_MODEL)), _const_spec((D_MODEL, D_MODEL))],
        out_specs=row_spec, out_shape=jax.ShapeDtypeStruct(x.shape, F32),
        compiler_params=_params(("parallel",)), name="sample_q",
    )(x, g, wq)


def _sample_out(x, att, wo):
    row_spec = pl.BlockSpec((ROW_TILE, D_MODEL), lambda i: (i, 0))
    return pl.pallas_call(
        _sample_out_kernel, grid=(x.shape[0] // ROW_TILE,),
        in_specs=[row_spec, row_spec, _const_spec((D_MODEL, D_MODEL))],
        out_specs=row_spec, out_shape=jax.ShapeDtypeStruct(x.shape, F32),
        compiler_params=_params(("parallel",)), name="sample_out",
    )(x, att, wo)


def _block_diag_tiles(w):
    groups, dg, _ = w.shape
    per_tile = MXU_TILE // dg
    w = w.reshape(groups // per_tile, per_tile, dg, dg)
    eye = jnp.eye(per_tile, dtype=w.dtype)
    tiles = jnp.einsum("npij,pq->npiqj", w, eye)
    return tiles.reshape(groups // per_tile, MXU_TILE, MXU_TILE).astype(BF16)


def _row(v):
    return v.reshape(1, -1).astype(F32)


def _layer(x_prompt, x_sample, mem_prompt, sconv, slru, spool, cache_k, cache_v, p, final_norm):
    n_b, n_t, _ = x_prompt.shape
    s_b, s_t, _ = x_sample.shape
    ffn1 = (_row(p["ffn1_norm"]), p["ffn1_w_gate"], p["ffn1_w_up"], p["ffn1_w_down"], _row(final_norm))
    ffn2 = (_row(p["ffn2_norm"]), p["ffn2_w_gate"], p["ffn2_w_up"], p["ffn2_w_down"], _row(final_norm))
    mixw = {
        "mix_norm": _row(p["mix_norm"]), "w_in": p["w_in"].astype(BF16),
        "conv_w": p["conv_w"].astype(F32), "conv_b": _row(p["conv_b"]),
        "lru_wa": _block_diag_tiles(p["lru_wa"]), "lru_ba": _row(p["lru_ba"]),
        "lru_wx": _block_diag_tiles(p["lru_wx"]), "lru_bx": _row(p["lru_bx"]),
        "lru_lambda": _row(p["lru_lambda"]),
        "pool_w": _block_diag_tiles(p["pool_w"]), "pool_scale": _row(p["pool_scale"]),
        "w_out": p["w_out"].astype(BF16),
    }
    gx, wq, wo = _row(p["xattn_norm"]), p["xattn_wq"].astype(BF16), p["xattn_wo"].astype(BF16)

    mk, mv, mk_bf, mv_bf = _memkv(mem_prompt, _row(p["mem_norm"]),
                                  p["xattn_wk"].astype(BF16), p["xattn_wv"].astype(BF16))

    xp = _ffn(x_prompt.reshape(n_b * n_t, D_MODEL), *ffn1, final_norm=False)
    xs = _ffn(x_sample.reshape(s_b * s_t, D_MODEL), *ffn1, final_norm=False)

    xp, p_conv, p_lru, p_pool = _mix(
        xp.reshape(n_b, n_t, D_MODEL), mixw, jnp.zeros((CONV_HIST, n_b, LRU_WIDTH), F32),
        jnp.zeros((n_b, LRU_WIDTH), F32), jnp.zeros((POOL_HIST, n_b, POOL_WIDTH), F32),
        tt=ROW_TILE // n_b, bb=n_b, n_sub=SUB_TILES, pos0=0)
    xs, s_conv, s_lru, s_pool = _mix(
        xs.reshape(s_b, s_t, D_MODEL), mixw, jnp.swapaxes(sconv, 0, 1), slru, jnp.swapaxes(spool, 0, 1),
        tt=s_t, bb=ROW_TILE // s_t, n_sub=1, pos0=PAST_LEN)
    xs = xs.reshape(s_b * s_t, D_MODEL)

    xp, att_s = _xattn(xp, gx, wq, mk_bf, mv_bf, wo, _sample_q(xs, gx, wq), cache_k, cache_v)
    xs = _sample_out(xs, att_s, wo)

    yp = _ffn(xp.reshape(n_b * n_t, D_MODEL), *ffn2, final_norm=True).reshape(n_b, n_t, D_MODEL)
    ys = _ffn(xs, *ffn2, final_norm=True).reshape(s_b, s_t, D_MODEL)

    states = (jnp.swapaxes(p_conv, 0, 1), p_lru, jnp.swapaxes(p_pool, 0, 1), mk, mv,
              jnp.swapaxes(s_conv, 0, 1), s_lru, jnp.swapaxes(s_pool, 0, 1))
    return yp, ys, states


def kernel(x_prompt, x_sample, mem_prompt, state_conv, state_lru, state_pool, cache_mem_k, cache_mem_v, ffn1_norm, ffn1_w_gate, ffn1_w_up, ffn1_w_down, mix_norm, w_in, conv_w, conv_b, lru_wa, lru_ba, lru_wx, lru_bx, lru_lambda, pool_w, pool_scale, w_out, xattn_norm, mem_norm, xattn_wq, xattn_wk, xattn_wv, xattn_wo, ffn2_norm, ffn2_w_gate, ffn2_w_up, ffn2_w_down, final_norm):
    depth = ffn1_norm.shape[0]
    assert depth == 1, "the final RMSNorm is fused into the last layer's second FFN"
    names = ("ffn1_norm", "ffn1_w_gate", "ffn1_w_up", "ffn1_w_down", "mix_norm", "w_in", "conv_w", "conv_b",
             "lru_wa", "lru_ba", "lru_wx", "lru_bx", "lru_lambda", "pool_w", "pool_scale", "w_out",
             "xattn_norm", "mem_norm", "xattn_wq", "xattn_wk", "xattn_wv", "xattn_wo",
             "ffn2_norm", "ffn2_w_gate", "ffn2_w_up", "ffn2_w_down")
    stacked = (ffn1_norm, ffn1_w_gate, ffn1_w_up, ffn1_w_down, mix_norm, w_in, conv_w, conv_b,
               lru_wa, lru_ba, lru_wx, lru_bx, lru_lambda, pool_w, pool_scale, w_out,
               xattn_norm, mem_norm, xattn_wq, xattn_wk, xattn_wv, xattn_wo,
               ffn2_norm, ffn2_w_gate, ffn2_w_up, ffn2_w_down)
    p = {n: a[0] for n, a in zip(names, stacked)}
    yp, ys, st = _layer(x_prompt, x_sample, mem_prompt, state_conv[0], state_lru[0], state_pool[0],
                        cache_mem_k[0], cache_mem_v[0], p, final_norm)
    p_conv, p_lru, p_pool, p_mk, p_mv, s_conv, s_lru, s_pool = (s[None] for s in st)
    return (yp, ys, p_conv, p_lru, p_pool, p_mk, p_mv, s_conv, s_lru, s_pool)
```

```python
import functools
import itertools

import jax
import jax.numpy as jnp
from jax import lax
from jax.experimental import pallas as pl
from jax.experimental.pallas import tpu as pltpu

F32 = jnp.float32
BF16 = jnp.bfloat16

D_MODEL = 1024
D_FF = 2816
LRU_WIDTH = 512
LRU_HEADS = 8
LRU_C = 8.0
CONV_W = 4
POOL_WIDTH = 512
POOL_WINDOWS = (2, 4, 8, 16)
POOL_GROUP_DIM = POOL_WIDTH // len(POOL_WINDOWS)
POOL_HIST = max(POOL_WINDOWS) - 1
CONV_HIST = CONV_W - 1
MEM_LEN = 256
XA_HEADS = 4
XA_HEAD_DIM = D_MODEL // XA_HEADS
EPS = 1e-6
PAST_LEN = 16384

MXU_TILE = 256
LANES = 128
assert XA_HEAD_DIM == 2 * LANES
FF_CHUNK = MXU_TILE
N_FF_CHUNKS = D_FF // FF_CHUNK
ROW_TILE = 512
SUB_TILES = 2
WEIGHT_LOOKAHEAD = 4
COL_STAGE_SLOTS = 4
ROW_STAGE_SLOTS = 2
SAMPLE_PAIR = 2
KV_LEAD = 2
VMEM_LIMIT_BYTES = 56 * 1024 * 1024


def _rmsnorm(x, g):
    var = jnp.mean(x * x, axis=-1, keepdims=True)
    return x * lax.rsqrt(var + EPS) * g


def _gelu_tanh(x):
    c0 = (2.0 / jnp.pi) ** 0.5
    half_x = 0.5 * x
    return half_x + half_x * jnp.tanh(x * (c0 + (c0 * 0.044715) * (x * x)))


def _dot(a, b):
    return jnp.dot(a, b, preferred_element_type=F32)


def _const_spec(shape):
    zeros = (0,) * len(shape)
    return pl.BlockSpec(shape, lambda *_: zeros, pipeline_mode=pl.Buffered(1))


def _params(semantics):
    return pltpu.CompilerParams(dimension_semantics=semantics, vmem_limit_bytes=VMEM_LIMIT_BYTES)


def _ffn_weight_fetcher(wg_hbm, wu_hbm, wd_hbm, wg_scr, wu_scr, wd_scr, col_stage, row_stage, sems):
    stages = (col_stage, row_stage)

    def chunk(c):
        cols = pl.ds(c * FF_CHUNK, FF_CHUNK)
        return {"gate": (wg_hbm.at[:, cols], 0, wg_scr.at[c]), "up": (wu_hbm.at[:, cols], 0, wu_scr.at[c]),
                "down": (wd_hbm.at[cols, :], 1, wd_scr.at[c])}

    items = [chunk(0)["gate"], chunk(0)["up"]]
    for c in range(1, N_FF_CHUNKS):
        items += [chunk(c)["gate"], chunk(c)["up"], chunk(c - 1)["down"]]
    items.append(chunk(N_FF_CHUNKS - 1)["down"])
    uses = [0, 0]
    plan = []
    for src, kind, dst in items:
        slot = uses[kind] % stages[kind].shape[0]
        uses[kind] += 1
        staged = stages[kind].at[slot]
        plan.append((pltpu.make_async_copy(src, staged, sems.at[kind, slot]), staged, dst, (kind, slot)))
    slot_free_after = {}
    state = {"next": 0}

    def start(m):
        key = plan[m][3]
        assert slot_free_after.get(key, -1) < state["next"], "staging slot reused before it was converted"
        slot_free_after[key] = m
        plan[m][0].start()

    for m in range(min(WEIGHT_LOOKAHEAD, len(plan))):
        start(m)

    def fetch():
        n = state["next"]
        copy, staged, dst, _ = plan[n]
        copy.wait()
        dst[...] = staged[...].astype(BF16)
        state["next"] = n + 1
        if n + WEIGHT_LOOKAHEAD < len(plan):
            start(n + WEIGHT_LOOKAHEAD)

    return fetch, len(plan)


def _ffn_kernel(x_ref, g_ref, wg_hbm, wu_hbm, wd_hbm, fn_ref, o_ref,
                wg_scr, wu_scr, wd_scr, col_stage, row_stage, sems, h_scr, a_scr, acc_scr,
                *, n_steps, final_norm):
    subs = range(SUB_TILES)
    rows = [slice(s * ROW_TILE, (s + 1) * ROW_TILE) for s in subs]

    def gate_up(s, c):
        h = h_scr[s]
        g = _dot(h, wg_scr[c])
        u = _dot(h, wu_scr[c])
        a_scr[s, c % 2] = (g * jax.nn.sigmoid(g) * u).astype(BF16)

    def down(s, c):
        d = _dot(a_scr[s, c % 2], wd_scr[c])
        if c == 0:
            acc_scr[s] = d
        else:
            acc_scr[s] += d

    def body(load_weights):
        if load_weights:
            fetch, n_items = _ffn_weight_fetcher(wg_hbm, wu_hbm, wd_hbm, wg_scr, wu_scr, wd_scr,
                                                 col_stage, row_stage, sems)
        else:
            fetch, n_items = (lambda: None), 0
        fetch()
        fetch()
        for s in subs:
            h_scr[s] = _rmsnorm(x_ref[rows[s], :], g_ref[...]).astype(BF16)
            gate_up(s, 0)
        for c in range(1, N_FF_CHUNKS):
            fetch()
            fetch()
            fetch()
            for s in subs:
                gate_up(s, c)
                down(s, c - 1)
        fetch()
        assert n_items in (0, 3 * N_FF_CHUNKS)
        for s in subs:
            down(s, N_FF_CHUNKS - 1)
            y = x_ref[rows[s], :] + 0.5 * acc_scr[s]
            if final_norm:
                y = _rmsnorm(y, fn_ref[...])
            o_ref[rows[s], :] = y

    if n_steps == 1:
        body(load_weights=True)
    else:
        @pl.when(pl.program_id(0) == 0)
        def _():
            fetch, n_items = _ffn_weight_fetcher(wg_hbm, wu_hbm, wd_hbm, wg_scr, wu_scr, wd_scr,
                                                 col_stage, row_stage, sems)
            for _ in range(n_items):
                fetch()

        body(load_weights=False)


def _ffn(x, g, wg, wu, wd, fn, *, final_norm):
    row_spec = pl.BlockSpec((SUB_TILES * ROW_TILE, D_MODEL), lambda i: (i, 0))
    hbm = pl.BlockSpec(memory_space=pl.ANY)
    n_steps = x.shape[0] // (SUB_TILES * ROW_TILE)
    return pl.pallas_call(
        functools.partial(_ffn_kernel, n_steps=n_steps, final_norm=final_norm),
        grid=(n_steps,),
        in_specs=[row_spec, _const_spec((1, D_MODEL)), hbm, hbm, hbm, _const_spec((1, D_MODEL))],
        out_specs=row_spec,
        out_shape=jax.ShapeDtypeStruct(x.shape, F32),
        scratch_shapes=[pltpu.VMEM((N_FF_CHUNKS, D_MODEL, FF_CHUNK), BF16),
                        pltpu.VMEM((N_FF_CHUNKS, D_MODEL, FF_CHUNK), BF16),
                        pltpu.VMEM((N_FF_CHUNKS, FF_CHUNK, D_MODEL), BF16),
                        pltpu.VMEM((COL_STAGE_SLOTS, D_MODEL, FF_CHUNK), F32),
                        pltpu.VMEM((ROW_STAGE_SLOTS, FF_CHUNK, D_MODEL), F32),
                        pltpu.SemaphoreType.DMA((2, COL_STAGE_SLOTS)),
                        pltpu.VMEM((SUB_TILES, ROW_TILE, D_MODEL), BF16),
                        pltpu.VMEM((SUB_TILES, 2, ROW_TILE, FF_CHUNK), BF16),
                        pltpu.VMEM((SUB_TILES, ROW_TILE, D_MODEL), F32)],
        compiler_params=_params(("arbitrary",)),
        name="ffn_final" if final_norm else "ffn",
    )(x, g, wg, wu, wd, fn)


def _mix_kernel(x_ref, gm_ref, win_ref, cw_ref, cb_ref, wa_ref, ba_ref, wx_ref, bx_ref, lam_ref,
                pw_ref, ps_ref, wout_ref, sconv_ref, slru_ref, spool_ref,
                o_ref, oconv_ref, olru_ref, opool_ref,
                lru_ext, pool_ext, h_scr, x_scr, hn_scr, gate_scr, a_scr, b_scr, y_scr, *, tt, bb, n_sub, pos0):
    rows = tt * bb
    i = pl.program_id(1)

    @pl.when(i == 0)
    def _():
        lru_ext[0:CONV_HIST * bb, :] = sconv_ref[...].reshape(CONV_HIST * bb, LRU_WIDTH)
        pool_ext[0:POOL_HIST * bb, :] = spool_ref[...].reshape(POOL_HIST * bb, POOL_WIDTH)
        h_scr[...] = slru_ref[...]

    half = MXU_TILE

    def prepare_pieces(s):
        n_pieces = max(1, min(4, tt // 8))
        tq = tt // n_pieces

        def piece(q):
            x = jnp.swapaxes(x_ref[:, s * tt + q * tq:s * tt + (q + 1) * tq, :], 0, 1)
            x = x.reshape(tq * bb, D_MODEL)
            x_scr[s, q * tq * bb:(q + 1) * tq * bb, :] = x
            hn_scr[s, q * tq * bb:(q + 1) * tq * bb, :] = _rmsnorm(x, gm_ref[...]).astype(BF16)
        return [functools.partial(piece, q) for q in range(n_pieces)]

    def project_in_pieces(s):
        lru_rows = slice((CONV_HIST + s * tt) * bb, (CONV_HIST + (s + 1) * tt) * bb)
        pool_rows = slice((POOL_HIST + s * tt) * bb, (POOL_HIST + (s + 1) * tt) * bb)

        def piece(n):
            p = _dot(hn_scr[s], win_ref[:, n * half:(n + 1) * half])
            cols = slice((n % 2) * half, (n % 2 + 1) * half)
            if n < 2:
                lru_ext[lru_rows, cols] = p
            elif n < 4:
                gate_scr[s, :, cols] = p
            else:
                pool_ext[pool_rows, cols] = p
        return [functools.partial(piece, n) for n in range(3 * LRU_WIDTH // half)]

    def lru_piece(s, j):
        t0 = s * tt
        lanes = slice(j * half, (j + 1) * half)
        cw = cw_ref[:, lanes]
        u = cb_ref[:, lanes] + sum(lru_ext[(t0 + k) * bb:(t0 + k) * bb + rows, lanes] * cw[k:k + 1, :]
                                   for k in range(CONV_W))
        ub = u.astype(BF16)
        r = jax.nn.sigmoid(_dot(ub, wa_ref[j]) + ba_ref[:, lanes])
        ig = jax.nn.sigmoid(_dot(ub, wx_ref[j]) + bx_ref[:, lanes])
        neg_lam = -lam_ref[:, lanes]
        softplus = jnp.maximum(neg_lam, 0.0) + jnp.log1p(jnp.exp(-jnp.abs(neg_lam)))
        log_a = (-LRU_C * softplus) * r
        a = jnp.exp(log_a)
        a_scr[s, :, lanes] = a
        m = jnp.maximum(1.0 - a * a, 0.0)
        b_scr[s, :, lanes] = jnp.where(m > 0.0, m * lax.rsqrt(m), 0.0) * ig * u
        hcur = h_scr[:, lanes]
        for t in range(tt):
            sl = slice(t * bb, (t + 1) * bb)
            hcur = a_scr[s, sl, lanes] * hcur + b_scr[s, sl, lanes]
            b_scr[s, sl, lanes] = hcur
        h_scr[:, lanes] = hcur
        y_scr[s, :, lanes] = (_gelu_tanh(gate_scr[s, :, lanes]) * b_scr[s, :, lanes]).astype(BF16)

    def pool_piece(s, j):
        t0 = s * tt
        t_idx = lax.broadcasted_iota(jnp.int32, (rows, POOL_GROUP_DIM), 0) // bb
        pos = pos0 + (i * n_sub + s) * tt + t_idx
        groups_per_tile = half // POOL_GROUP_DIM
        deltas = []
        for g in range(j * groups_per_tile, (j + 1) * groups_per_tile):
            win = POOL_WINDOWS[g]
            lanes = slice(g * POOL_GROUP_DIM, (g + 1) * POOL_GROUP_DIM)
            w_sum = pool_ext[t0 * bb:(t0 + POOL_HIST + tt) * bb, lanes]
            span = 1
            while span < win:
                w_sum = w_sum[span * bb:, :] + w_sum[:-span * bb, :]
                span *= 2
            w_sum = w_sum[w_sum.shape[0] - rows:, :]
            cnt = jnp.minimum(pos + 1, win).astype(F32)
            u_pool = pool_ext[(t0 + POOL_HIST) * bb:(t0 + POOL_HIST + tt) * bb, lanes]
            deltas.append(w_sum / cnt - u_pool)
        delta = jnp.concatenate(deltas, axis=1).astype(BF16)
        lanes = slice(j * half, (j + 1) * half)
        y_scr[s, :, LRU_WIDTH + j * half:LRU_WIDTH + (j + 1) * half] = (
            ps_ref[:, lanes] * _dot(delta, pw_ref[j])).astype(BF16)

    def mix_pieces(s):
        return [functools.partial(piece, s, j) for j in range(LRU_WIDTH // half) for piece in (lru_piece, pool_piece)]

    def project_out_pieces(s):
        def piece(n):
            cols = slice(n * half, (n + 1) * half)
            out = x_scr[s, :, cols] + _dot(y_scr[s], wout_ref[:, cols])
            o_ref[:, s * tt:(s + 1) * tt, cols] = jnp.swapaxes(out.reshape(tt, bb, half), 0, 1)
        return [functools.partial(piece, n) for n in range(D_MODEL // half)]

    def emit(*streams):
        for pieces in itertools.zip_longest(*streams):
            for piece in pieces:
                if piece is not None:
                    piece()

    emit(prepare_pieces(0))
    for k in range(1, n_sub + 3):
        mxu, valu = [], []
        if k - 1 < n_sub:
            mxu += project_in_pieces(k - 1)
        if 0 <= k - 3 < n_sub:
            mxu += project_out_pieces(k - 3)
        if k < n_sub:
            valu += prepare_pieces(k)
        if 0 <= k - 2 < n_sub:
            valu += mix_pieces(k - 2)
        emit(mxu, valu)

    new_conv = lru_ext[n_sub * tt * bb:(n_sub * tt + CONV_HIST) * bb, :]
    oconv_ref[...] = new_conv.reshape(CONV_HIST, bb, LRU_WIDTH)
    lru_ext[0:CONV_HIST * bb, :] = new_conv
    new_pool = pool_ext[n_sub * tt * bb:(n_sub * tt + POOL_HIST) * bb, :]
    opool_ref[...] = new_pool.reshape(POOL_HIST, bb, POOL_WIDTH)
    pool_ext[0:POOL_HIST * bb, :] = new_pool
    olru_ref[...] = h_scr[...]


def _mix(x, w, sconv, slru, spool, *, tt, bb, n_sub, pos0):
    n_b, n_t = x.shape[0], x.shape[1]
    rows = tt * bb
    x_spec = pl.BlockSpec((bb, n_sub * tt, D_MODEL), lambda j, i: (j, i, 0))
    conv_spec = pl.BlockSpec((CONV_HIST, bb, LRU_WIDTH), lambda j, i: (0, j, 0))
    lru_spec = pl.BlockSpec((bb, LRU_WIDTH), lambda j, i: (j, 0))
    pool_spec = pl.BlockSpec((POOL_HIST, bb, POOL_WIDTH), lambda j, i: (0, j, 0))
    return pl.pallas_call(
        functools.partial(_mix_kernel, tt=tt, bb=bb, n_sub=n_sub, pos0=pos0),
        grid=(n_b // bb, n_t // (n_sub * tt)),
        in_specs=[x_spec, _const_spec((1, D_MODEL)), _const_spec((D_MODEL, 3 * LRU_WIDTH)),
                  _const_spec((CONV_W, LRU_WIDTH)), _const_spec((1, LRU_WIDTH)),
                  _const_spec((2, MXU_TILE, MXU_TILE)), _const_spec((1, LRU_WIDTH)),
                  _const_spec((2, MXU_TILE, MXU_TILE)), _const_spec((1, LRU_WIDTH)),
                  _const_spec((1, LRU_WIDTH)),
                  _const_spec((2, MXU_TILE, MXU_TILE)), _const_spec((1, POOL_WIDTH)),
                  _const_spec((D_MODEL, D_MODEL)),
                  conv_spec, lru_spec, pool_spec],
        out_specs=[x_spec, conv_spec, lru_spec, pool_spec],
        out_shape=[jax.ShapeDtypeStruct(x.shape, F32),
                   jax.ShapeDtypeStruct((CONV_HIST, n_b, LRU_WIDTH), F32),
                   jax.ShapeDtypeStruct((n_b, LRU_WIDTH), F32),
                   jax.ShapeDtypeStruct((POOL_HIST, n_b, POOL_WIDTH), F32)],
        scratch_shapes=[pltpu.VMEM(((CONV_HIST + n_sub * tt) * bb, LRU_WIDTH), F32),
                        pltpu.VMEM(((POOL_HIST + n_sub * tt) * bb, POOL_WIDTH), F32),
                        pltpu.VMEM((bb, LRU_WIDTH), F32),
                        pltpu.VMEM((n_sub, rows, D_MODEL), F32),
                        pltpu.VMEM((n_sub, rows, D_MODEL), BF16),
                        pltpu.VMEM((n_sub, rows, LRU_WIDTH), F32),
                        pltpu.VMEM((n_sub, rows, LRU_WIDTH), F32),
                        pltpu.VMEM((n_sub, rows, LRU_WIDTH), F32),
                        pltpu.VMEM((n_sub, rows, LRU_WIDTH + POOL_WIDTH), BF16)],
        compiler_params=_params(("parallel", "arbitrary")),
        name="mix",
    )(x, w["mix_norm"], w["w_in"], w["conv_w"], w["conv_b"], w["lru_wa"], w["lru_ba"],
      w["lru_wx"], w["lru_bx"], w["lru_lambda"], w["pool_w"], w["pool_scale"], w["w_out"],
      sconv, slru, spool)


def _memkv_kernel(m_ref, g_ref, wk_ref, wv_ref, k_ref, v_ref, kb_ref, vb_ref):
    n_b = m_ref.shape[0]
    m = _rmsnorm(m_ref[...].reshape(n_b * MEM_LEN, D_MODEL), g_ref[...]).astype(BF16)
    for w_ref, o_ref, ob_ref in ((wk_ref, k_ref, kb_ref), (wv_ref, v_ref, vb_ref)):
        kv = _dot(m, w_ref[...])
        for b in range(n_b):
            kv_b = kv[b * MEM_LEN:(b + 1) * MEM_LEN, :]
            ob_ref[b] = kv_b.astype(BF16)
            for hd in range(XA_HEADS):
                o_ref[b, :, hd, :] = kv_b[:, hd * XA_HEAD_DIM:(hd + 1) * XA_HEAD_DIM]


def _memkv(mem, g, wk, wv):
    n_b = mem.shape[0]
    per_step = ROW_TILE // MEM_LEN
    row_spec = pl.BlockSpec((per_step, MEM_LEN, D_MODEL), lambda i: (i, 0, 0))
    head_spec = pl.BlockSpec((per_step, MEM_LEN, XA_HEADS, XA_HEAD_DIM), lambda i: (i, 0, 0, 0))
    return pl.pallas_call(
        _memkv_kernel,
        grid=(n_b // per_step,),
        in_specs=[row_spec, _const_spec((1, D_MODEL)), _const_spec((D_MODEL, D_MODEL)),
                  _const_spec((D_MODEL, D_MODEL))],
        out_specs=[head_spec, head_spec, row_spec, row_spec],
        out_shape=[jax.ShapeDtypeStruct((n_b, MEM_LEN, XA_HEADS, XA_HEAD_DIM), F32)] * 2
        + [jax.ShapeDtypeStruct((n_b, MEM_LEN, D_MODEL), BF16)] * 2,
        compiler_params=_params(("parallel",)),
        name="memkv",
    )(mem, g, wk, wv)


def _xattn_kernel(x_ref, g_ref, wq_ref, k_ref, v_ref, wo_ref, qs_ref, kc_hbm, vc_hbm, o_ref, atts_ref,
                  q_scr, s_scr, att_scr, ring, sems, ss_scr, *, tq, per_step):
    subs = range(SUB_TILES)
    rows = [slice(s * ROW_TILE, (s + 1) * ROW_TILE) for s in subs]
    heads = [slice(hd * XA_HEAD_DIM, (hd + 1) * XA_HEAD_DIM) for hd in range(XA_HEADS)]
    scale = XA_HEAD_DIM ** -0.5

    def project_q(s):
        q_scr[s] = _dot(_rmsnorm(x_ref[rows[s], :], g_ref[...]).astype(BF16), wq_ref[...]).astype(BF16)

    def scores(s):
        for hd, cols in enumerate(heads):
            s_scr[s, hd] = lax.dot_general(q_scr[s, :, cols], k_ref[:, cols], (((1,), (1,)), ((), ())),
                                           preferred_element_type=F32) * scale

    def attend(s):
        sc = s_scr[s]
        e = jnp.exp(sc - jnp.max(sc, axis=-1, keepdims=True))
        s_scr[s] = e / jnp.sum(e, axis=-1, keepdims=True)
        for hd, cols in enumerate(heads):
            att_scr[s, :, cols] = _dot(s_scr[s, hd].astype(BF16), v_ref[:, cols]).astype(BF16)

    def project_out(s):
        o_ref[rows[s], :] = x_ref[rows[s], :] + _dot(att_scr[s], wo_ref[...])

    step = pl.program_id(0) * pl.num_programs(1) + pl.program_id(1)
    n_steps = pl.num_programs(0) * pl.num_programs(1)
    n_pairs = per_step // SAMPLE_PAIR
    assert KV_LEAD < n_pairs

    def pair_copies(p, of_step):
        first_seq = of_step * per_step + p * SAMPLE_PAIR
        sources = ((kc_hbm, 0), (kc_hbm, 1), (vc_hbm, 0), (vc_hbm, 1))
        return [pltpu.make_async_copy(
            cache.at[pl.ds(first_seq, SAMPLE_PAIR), :, :, pl.ds(half * LANES, LANES)],
            ring.at[p, j], sems.at[p, j]) for j, (cache, half) in enumerate(sources)]

    @pl.when(step == 0)
    def _():
        for p in range(KV_LEAD):
            for copy in pair_copies(p, 0):
                copy.start()

    def pair_views(p):
        views = [ring.at[p, j].reshape(SAMPLE_PAIR * MEM_LEN * XA_HEADS, LANES) for j in range(4)]

        def head(kv, b, hd):
            r = pl.ds(b * MEM_LEN * XA_HEADS + hd, MEM_LEN, stride=XA_HEADS)
            return jnp.concatenate([views[2 * kv][r, :], views[2 * kv + 1][r, :]], axis=1).astype(BF16)
        return head

    items = [(b, hd) for b in range(SAMPLE_PAIR) for hd in range(XA_HEADS)]

    def sample_scores(p):
        slot = p % 2
        ahead = p + KV_LEAD
        if ahead < n_pairs:
            for copy in pair_copies(ahead, step):
                copy.start()
        else:
            @pl.when(step + 1 < n_steps)
            def _():
                for copy in pair_copies(ahead - n_pairs, step + 1):
                    copy.start()
        for copy in pair_copies(p, step):
            copy.wait()
        head = pair_views(p)
        for n, (b, hd) in enumerate(items):
            r0 = (p * SAMPLE_PAIR + b) * tq
            ss_scr[slot, n * tq:(n + 1) * tq, :] = lax.dot_general(
                qs_ref[r0:r0 + tq, heads[hd]].astype(BF16), head(0, b, hd), (((1,), (1,)), ((), ())),
                preferred_element_type=F32) * scale

    def sample_attend(p):
        slot = p % 2
        head = pair_views(p)
        sc = ss_scr[slot]
        e = jnp.exp(sc - jnp.max(sc, axis=-1, keepdims=True))
        ss_scr[slot] = e / jnp.sum(e, axis=-1, keepdims=True)
        for n, (b, hd) in enumerate(items):
            r0 = (p * SAMPLE_PAIR + b) * tq
            atts_ref[r0:r0 + tq, heads[hd]] = _dot(ss_scr[slot, n * tq:(n + 1) * tq, :].astype(BF16),
                                                  head(1, b, hd))

    stages = (project_q, scores, attend, project_out)
    n_rounds = SUB_TILES + len(stages) - 1
    assert n_pairs + 1 <= n_rounds
    for k in range(n_rounds):
        for depth in reversed(range(len(stages))):
            if 0 <= k - depth < SUB_TILES:
                stages[depth](k - depth)
        if 1 <= k <= n_pairs:
            sample_attend(k - 1)
        if k < n_pairs:
            sample_scores(k)


def _xattn(x, g, wq, k, v, wo, qs, cache_k, cache_v):
    n_b, n_t = x.shape[0], x.shape[1]
    tile = SUB_TILES * ROW_TILE
    n_i = n_t // tile
    n_seq = cache_k.shape[0]
    tq = qs.shape[0] // n_seq
    per_step = n_seq // (n_b * n_i)
    assert per_step * n_b * n_i == n_seq
    x_spec = pl.BlockSpec((None, tile, D_MODEL), lambda b, i: (b, i, 0))
    kv_spec = pl.BlockSpec((None, MEM_LEN, D_MODEL), lambda b, i: (b, 0, 0))
    qs_spec = pl.BlockSpec((per_step * tq, D_MODEL), lambda b, i: (b * n_i + i, 0))
    hbm = pl.BlockSpec(memory_space=pl.ANY)
    return pl.pallas_call(
        functools.partial(_xattn_kernel, tq=tq, per_step=per_step),
        grid=(n_b, n_i),
        in_specs=[x_spec, _const_spec((1, D_MODEL)), _const_spec((D_MODEL, D_MODEL)),
                  kv_spec, kv_spec, _const_spec((D_MODEL, D_MODEL)), qs_spec, hbm, hbm],
        out_specs=[x_spec, qs_spec],
        out_shape=[jax.ShapeDtypeStruct(x.shape, F32), jax.ShapeDtypeStruct(qs.shape, F32)],
        scratch_shapes=[pltpu.VMEM((SUB_TILES, ROW_TILE, D_MODEL), BF16),
                        pltpu.VMEM((SUB_TILES, XA_HEADS, ROW_TILE, MEM_LEN), F32),
                        pltpu.VMEM((SUB_TILES, ROW_TILE, D_MODEL), BF16),
                        pltpu.VMEM((per_step // SAMPLE_PAIR, 4, SAMPLE_PAIR, MEM_LEN, XA_HEADS, LANES), F32),
                        pltpu.SemaphoreType.DMA((per_step // SAMPLE_PAIR, 4)),
                        pltpu.VMEM((2, SAMPLE_PAIR * XA_HEADS * tq, MEM_LEN), F32)],
        compiler_params=_params(("arbitrary", "arbitrary")),
        name="xattn",
    )(x, g, wq, k, v, wo, qs, cache_k, cache_v)


def _sample_q_kernel(x_ref, g_ref, wq_ref, q_ref):
    q_ref[...] = _dot(_rmsnorm(x_ref[...], g_ref[...]).astype(BF16), wq_ref[...])


def _sample_out_kernel(x_ref, a_ref, wo_ref, o_ref):
    o_ref[...] = x_ref[...] + _dot(a_ref[...].astype(BF16), wo_ref[...])


def _sample_q(x, g, wq):
    row_spec = pl.BlockSpec((ROW_TILE, D_MODEL), lambda i: (i, 0))
    return pl.pallas_call(
        _sample_q_kernel, grid=(x.shape[0] // ROW_TILE,),
        in_specs=[row_spec, _const_spec((1, D_MODEL)), _const_spec((D_MODEL, D_MODEL))],
        out_specs=row_spec, out_shape=jax.ShapeDtypeStruct(x.shape, F32),
        compiler_params=_params(("parallel",)), name="sample_q",
    )(x, g, wq)


def _sample_out(x, att, wo):
    row_spec = pl.BlockSpec((ROW_TILE, D_MODEL), lambda i: (i, 0))
    return pl.pallas_call(
        _sample_out_kernel, grid=(x.shape[0] // ROW_TILE,),
        in_specs=[row_spec, row_spec, _const_spec((D_MODEL, D_MODEL))],
        out_specs=row_spec, out_shape=jax.ShapeDtypeStruct(x.shape, F32),
        compiler_params=_params(("parallel",)), name="sample_out",
    )(x, att, wo)


def _block_diag_tiles(w):
    groups, dg, _ = w.shape
    per_tile = MXU_TILE // dg
    w = w.reshape(groups // per_tile, per_tile, dg, dg)
    eye = jnp.eye(per_tile, dtype=w.dtype)
    tiles = jnp.einsum("npij,pq->npiqj", w, eye)
    return tiles.reshape(groups // per_tile, MXU_TILE, MXU_TILE).astype(BF16)


def _row(v):
    return v.reshape(1, -1).astype(F32)


def _layer(x_prompt, x_sample, mem_prompt, sconv, slru, spool, cache_k, cache_v, p, final_norm):
    n_b, n_t, _ = x_prompt.shape
    s_b, s_t, _ = x_sample.shape
    ffn1 = (_row(p["ffn1_norm"]), p["ffn1_w_gate"], p["ffn1_w_up"], p["ffn1_w_down"], _row(final_norm))
    ffn2 = (_row(p["ffn2_norm"]), p["ffn2_w_gate"], p["ffn2_w_up"], p["ffn2_w_down"], _row(final_norm))
    mixw = {
        "mix_norm": _row(p["mix_norm"]), "w_in": p["w_in"].astype(BF16),
        "conv_w": p["conv_w"].astype(F32), "conv_b": _row(p["conv_b"]),
        "lru_wa": _block_diag_tiles(p["lru_wa"]), "lru_ba": _row(p["lru_ba"]),
        "lru_wx": _block_diag_tiles(p["lru_wx"]), "lru_bx": _row(p["lru_bx"]),
        "lru_lambda": _row(p["lru_lambda"]),
        "pool_w": _block_diag_tiles(p["pool_w"]), "pool_scale": _row(p["pool_scale"]),
        "w_out": p["w_out"].astype(BF16),
    }
    gx, wq, wo = _row(p["xattn_norm"]), p["xattn_wq"].astype(BF16), p["xattn_wo"].astype(BF16)

    mk, mv, mk_bf, mv_bf = _memkv(mem_prompt, _row(p["mem_norm"]),
                                  p["xattn_wk"].astype(BF16), p["xattn_wv"].astype(BF16))

    xp = _ffn(x_prompt.reshape(n_b * n_t, D_MODEL), *ffn1, final_norm=False)
    xs = _ffn(x_sample.reshape(s_b * s_t, D_MODEL), *ffn1, final_norm=False)

    xp, p_conv, p_lru, p_pool = _mix(
        xp.reshape(n_b, n_t, D_MODEL), mixw, jnp.zeros((CONV_HIST, n_b, LRU_WIDTH), F32),
        jnp.zeros((n_b, LRU_WIDTH), F32), jnp.zeros((POOL_HIST, n_b, POOL_WIDTH), F32),
        tt=ROW_TILE // n_b, bb=n_b, n_sub=SUB_TILES, pos0=0)
    xs, s_conv, s_lru, s_pool = _mix(
        xs.reshape(s_b, s_t, D_MODEL), mixw, jnp.swapaxes(sconv, 0, 1), slru, jnp.swapaxes(spool, 0, 1),
        tt=s_t, bb=ROW_TILE // s_t, n_sub=1, pos0=PAST_LEN)
    xs = xs.reshape(s_b * s_t, D_MODEL)

    xp, att_s = _xattn(xp, gx, wq, mk_bf, mv_bf, wo, _sample_q(xs, gx, wq), cache_k, cache_v)
    xs = _sample_out(xs, att_s, wo)

    yp = _ffn(xp.reshape(n_b * n_t, D_MODEL), *ffn2, final_norm=True).reshape(n_b, n_t, D_MODEL)
    ys = _ffn(xs, *ffn2, final_norm=True).reshape(s_b, s_t, D_MODEL)

    states = (jnp.swapaxes(p_conv, 0, 1), p_lru, jnp.swapaxes(p_pool, 0, 1), mk, mv,
              jnp.swapaxes(s_conv, 0, 1), s_lru, jnp.swapaxes(s_pool, 0, 1))
    return yp, ys, states


def kernel(x_prompt, x_sample, mem_prompt, state_conv, state_lru, state_pool, cache_mem_k, cache_mem_v, ffn1_norm, ffn1_w_gate, ffn1_w_up, ffn1_w_down, mix_norm, w_in, conv_w, conv_b, lru_wa, lru_ba, lru_wx, lru_bx, lru_lambda, pool_w, pool_scale, w_out, xattn_norm, mem_norm, xattn_wq, xattn_wk, xattn_wv, xattn_wo, ffn2_norm, ffn2_w_gate, ffn2_w_up, ffn2_w_down, final_norm):
    depth = ffn1_norm.shape[0]
    assert depth == 1, "the final RMSNorm is fused into the last layer's second FFN"
    names = ("ffn1_norm", "ffn1_w_gate", "ffn1_w_up", "ffn1_w_down", "mix_norm", "w_in", "conv_w", "conv_b",
             "lru_wa", "lru_ba", "lru_wx", "lru_bx", "lru_lambda", "pool_w", "pool_scale", "w_out",
             "xattn_norm", "mem_norm", "xattn_wq", "xattn_wk", "xattn_wv", "xattn_wo",
             "ffn2_norm", "ffn2_w_gate", "ffn2_w_up", "ffn2_w_down")
    stacked = (ffn1_norm, ffn1_w_gate, ffn1_w_up, ffn1_w_down, mix_norm, w_in, conv_w, conv_b,
               lru_wa, lru_ba, lru_wx, lru_bx, lru_lambda, pool_w, pool_scale, w_out,
               xattn_norm, mem_norm, xattn_wq, xattn_wk, xattn_wv, xattn_wo,
               ffn2_norm, ffn2_w_gate, ffn2_w_up, ffn2_w_down)
    p = {n: a[0] for n, a in zip(names, stacked)}
    yp, ys, st = _layer(x_prompt, x_sample, mem_prompt, state_conv[0], state_lru[0], state_pool[0],
                        cache_mem_k[0], cache_mem_v[0], p, final_norm)
    p_conv, p_lru, p_pool, p_mk, p_mv, s_conv, s_lru, s_pool = (s[None] for s in st)
    return (yp, ys, p_conv, p_lru, p_pool, p_mk, p_mv, s_conv, s_lru, s_pool)
```

```python
import functools
import itertools
import math

import jax
import jax.numpy as jnp
from jax import lax
from jax.experimental import pallas as pl
from jax.experimental.pallas import tpu as pltpu

F32 = jnp.float32
BF16 = jnp.bfloat16

D_MODEL = 1024
D_FF = 2816
LRU_WIDTH = 512
LRU_C = 8.0
CONV_W = 4
POOL_WIDTH = 512
POOL_WINDOWS = (2, 4, 8, 16)
POOL_GROUP_DIM = POOL_WIDTH // len(POOL_WINDOWS)
POOL_HIST = max(POOL_WINDOWS) - 1
CONV_HIST = CONV_W - 1
MEM_LEN = 256
XA_HEADS = 4
XA_HEAD_DIM = D_MODEL // XA_HEADS
EPS = 1e-6
PAST_LEN = 16384

MXU_TILE = 256
LANES = 128
assert XA_HEAD_DIM == 2 * LANES
FF_CHUNK = MXU_TILE
N_FF_CHUNKS = D_FF // FF_CHUNK
ROW_TILE = 512
SUB_TILES = 2
WEIGHT_LOOKAHEAD = 4
COL_STAGE_SLOTS = 4
ROW_STAGE_SLOTS = 2
SAMPLE_PAIR = 2
KV_LEAD = 2
VMEM_BYTES = 64 * 1024 * 1024
TEMPORARIES_BYTES = SUB_TILES * 2 * ROW_TILE * D_MODEL * 4


def _rmsnorm(x, g):
    var = jnp.mean(x * x, axis=-1, keepdims=True)
    return x * lax.rsqrt(var + EPS) * g


def _gelu_tanh(x):
    c0 = (2.0 / jnp.pi) ** 0.5
    half_x = 0.5 * x
    return half_x + half_x * jnp.tanh(x * (c0 + (c0 * 0.044715) * (x * x)))


def _dot(a, b):
    return jnp.dot(a, b, preferred_element_type=F32)


def _const_spec(shape):
    zeros = (0,) * len(shape)
    return pl.BlockSpec(shape, lambda *_: zeros, pipeline_mode=pl.Buffered(1))


def _pallas(kernel, operands, *, grid, in_specs, out_specs, out_shape, semantics, name, scratch_shapes=()):
    def block_bytes(spec, dtype):
        if spec.block_shape is None:
            return 0
        buffers = spec.pipeline_mode.buffer_count if spec.pipeline_mode is not None else 2
        return buffers * math.prod(d or 1 for d in spec.block_shape) * jnp.dtype(dtype).itemsize

    outs = out_shape if isinstance(out_shape, (list, tuple)) else [out_shape]
    out_spec_list = out_specs if isinstance(out_specs, (list, tuple)) else [out_specs]
    need = sum(block_bytes(s, a.dtype) for s, a in zip(in_specs, operands))
    need += sum(block_bytes(s, o.dtype) for s, o in zip(out_spec_list, outs))
    need += sum(math.prod(s.shape) * jnp.dtype(s.dtype).itemsize for s in scratch_shapes
                if s.memory_space == pltpu.MemorySpace.VMEM)
    limit = min(need + TEMPORARIES_BYTES, VMEM_BYTES)
    return pl.pallas_call(
        kernel, grid=grid, in_specs=in_specs, out_specs=out_specs, out_shape=out_shape,
        scratch_shapes=list(scratch_shapes),
        compiler_params=pltpu.CompilerParams(dimension_semantics=semantics, vmem_limit_bytes=limit),
        name=name,
    )(*operands)


def _ffn_weight_fetcher(wg_hbm, wu_hbm, wd_hbm, wg_scr, wu_scr, wd_scr, col_stage, row_stage, sems):
    stages = (col_stage, row_stage)

    def chunk(c):
        cols = pl.ds(c * FF_CHUNK, FF_CHUNK)
        return {"gate": (wg_hbm.at[:, cols], 0, wg_scr.at[c]), "up": (wu_hbm.at[:, cols], 0, wu_scr.at[c]),
                "down": (wd_hbm.at[cols, :], 1, wd_scr.at[c])}

    items = [chunk(0)["gate"], chunk(0)["up"]]
    for c in range(1, N_FF_CHUNKS):
        items += [chunk(c)["gate"], chunk(c)["up"], chunk(c - 1)["down"]]
    items.append(chunk(N_FF_CHUNKS - 1)["down"])
    uses = [0, 0]
    plan = []
    for src, kind, dst in items:
        slot = uses[kind] % stages[kind].shape[0]
        uses[kind] += 1
        staged = stages[kind].at[slot]
        plan.append((pltpu.make_async_copy(src, staged, sems.at[kind, slot]), staged, dst, (kind, slot)))
    slot_free_after = {}
    state = {"next": 0}

    def start(m):
        key = plan[m][3]
        assert slot_free_after.get(key, -1) < state["next"], "staging slot reused before it was converted"
        slot_free_after[key] = m
        plan[m][0].start()

    for m in range(min(WEIGHT_LOOKAHEAD, len(plan))):
        start(m)

    def fetch():
        n = state["next"]
        copy, staged, dst, _ = plan[n]
        copy.wait()
        dst[...] = staged[...].astype(BF16)
        state["next"] = n + 1
        if n + WEIGHT_LOOKAHEAD < len(plan):
            start(n + WEIGHT_LOOKAHEAD)

    return fetch, len(plan)


def _ffn_kernel(x_ref, g_ref, wg_hbm, wu_hbm, wd_hbm, fn_ref, o_ref,
                wg_scr, wu_scr, wd_scr, col_stage, row_stage, sems, h_scr, a_scr, acc_scr,
                *, n_steps, final_norm):
    subs = range(SUB_TILES)
    rows = [slice(s * ROW_TILE, (s + 1) * ROW_TILE) for s in subs]

    def gate_up(s, c):
        h = h_scr[s]
        g = _dot(h, wg_scr[c])
        u = _dot(h, wu_scr[c])
        a_scr[s, c % 2] = (g * jax.nn.sigmoid(g) * u).astype(BF16)

    def down(s, c):
        d = _dot(a_scr[s, c % 2], wd_scr[c])
        if c == 0:
            acc_scr[s] = d
        else:
            acc_scr[s] += d

    def body(load_weights):
        if load_weights:
            fetch, n_items = _ffn_weight_fetcher(wg_hbm, wu_hbm, wd_hbm, wg_scr, wu_scr, wd_scr,
                                                 col_stage, row_stage, sems)
        else:
            fetch, n_items = (lambda: None), 0
        fetch()
        fetch()
        for s in subs:
            h_scr[s] = _rmsnorm(x_ref[rows[s], :], g_ref[...]).astype(BF16)
            gate_up(s, 0)
        for c in range(1, N_FF_CHUNKS):
            fetch()
            fetch()
            fetch()
            for s in subs:
                gate_up(s, c)
                down(s, c - 1)
        fetch()
        assert n_items in (0, 3 * N_FF_CHUNKS)
        for s in subs:
            down(s, N_FF_CHUNKS - 1)
            y = x_ref[rows[s], :] + 0.5 * acc_scr[s]
            if final_norm:
                y = _rmsnorm(y, fn_ref[...])
            o_ref[rows[s], :] = y

    if n_steps == 1:
        body(load_weights=True)
    else:
        @pl.when(pl.program_id(0) == 0)
        def _():
            fetch, n_items = _ffn_weight_fetcher(wg_hbm, wu_hbm, wd_hbm, wg_scr, wu_scr, wd_scr,
                                                 col_stage, row_stage, sems)
            for _ in range(n_items):
                fetch()

        body(load_weights=False)


def _ffn(x, g, wg, wu, wd, fn, *, final_norm):
    row_spec = pl.BlockSpec((SUB_TILES * ROW_TILE, D_MODEL), lambda i: (i, 0))
    hbm = pl.BlockSpec(memory_space=pl.ANY)
    n_steps = x.shape[0] // (SUB_TILES * ROW_TILE)
    return _pallas(
        functools.partial(_ffn_kernel, n_steps=n_steps, final_norm=final_norm), (x, g, wg, wu, wd, fn),
        grid=(n_steps,),
        in_specs=[row_spec, _const_spec((1, D_MODEL)), hbm, hbm, hbm, _const_spec((1, D_MODEL))],
        out_specs=row_spec,
        out_shape=jax.ShapeDtypeStruct(x.shape, F32),
        scratch_shapes=[pltpu.VMEM((N_FF_CHUNKS, D_MODEL, FF_CHUNK), BF16),
                        pltpu.VMEM((N_FF_CHUNKS, D_MODEL, FF_CHUNK), BF16),
                        pltpu.VMEM((N_FF_CHUNKS, FF_CHUNK, D_MODEL), BF16),
                        pltpu.VMEM((COL_STAGE_SLOTS, D_MODEL, FF_CHUNK), F32),
                        pltpu.VMEM((ROW_STAGE_SLOTS, FF_CHUNK, D_MODEL), F32),
                        pltpu.SemaphoreType.DMA((2, COL_STAGE_SLOTS)),
                        pltpu.VMEM((SUB_TILES, ROW_TILE, D_MODEL), BF16),
                        pltpu.VMEM((SUB_TILES, 2, ROW_TILE, FF_CHUNK), BF16),
                        pltpu.VMEM((SUB_TILES, ROW_TILE, D_MODEL), F32)],
        semantics=("arbitrary",),
        name="ffn_final" if final_norm else "ffn")


def _mix_kernel(x_ref, gm_ref, win_ref, cw_ref, cb_ref, wa_ref, ba_ref, wx_ref, bx_ref, lam_ref,
                pw_ref, ps_ref, wout_ref, sconv_ref, slru_ref, spool_ref,
                o_ref, oconv_ref, olru_ref, opool_ref,
                lru_ext, pool_ext, h_scr, x_scr, hn_scr, gate_scr, a_scr, b_scr, y_scr, *, tt, bb, n_sub, pos0):
    rows = tt * bb
    i = pl.program_id(1)

    @pl.when(i == 0)
    def _():
        lru_ext[0:CONV_HIST * bb, :] = sconv_ref[...].reshape(CONV_HIST * bb, LRU_WIDTH)
        pool_ext[0:POOL_HIST * bb, :] = spool_ref[...].reshape(POOL_HIST * bb, POOL_WIDTH)
        h_scr[...] = slru_ref[...]

    half = MXU_TILE

    def prepare_pieces(s):
        n_pieces = max(1, min(4, tt // 8))
        tq = tt // n_pieces

        def piece(q):
            x = jnp.swapaxes(x_ref[:, s * tt + q * tq:s * tt + (q + 1) * tq, :], 0, 1)
            x = x.reshape(tq * bb, D_MODEL)
            x_scr[s, q * tq * bb:(q + 1) * tq * bb, :] = x
            hn_scr[s, q * tq * bb:(q + 1) * tq * bb, :] = _rmsnorm(x, gm_ref[...]).astype(BF16)
        return [functools.partial(piece, q) for q in range(n_pieces)]

    def project_in_pieces(s):
        lru_rows = slice((CONV_HIST + s * tt) * bb, (CONV_HIST + (s + 1) * tt) * bb)
        pool_rows = slice((POOL_HIST + s * tt) * bb, (POOL_HIST + (s + 1) * tt) * bb)

        def piece(n):
            p = _dot(hn_scr[s], win_ref[:, n * half:(n + 1) * half])
            cols = slice((n % 2) * half, (n % 2 + 1) * half)
            if n < 2:
                lru_ext[lru_rows, cols] = p
            elif n < 4:
                gate_scr[s, :, cols] = p
            else:
                pool_ext[pool_rows, cols] = p
        return [functools.partial(piece, n) for n in range(3 * LRU_WIDTH // half)]

    def lru_piece(s, j):
        t0 = s * tt
        lanes = slice(j * half, (j + 1) * half)
        cw = cw_ref[:, lanes]
        u = cb_ref[:, lanes] + sum(lru_ext[(t0 + k) * bb:(t0 + k) * bb + rows, lanes] * cw[k:k + 1, :]
                                   for k in range(CONV_W))
        ub = u.astype(BF16)
        r = jax.nn.sigmoid(_dot(ub, wa_ref[j]) + ba_ref[:, lanes])
        ig = jax.nn.sigmoid(_dot(ub, wx_ref[j]) + bx_ref[:, lanes])
        neg_lam = -lam_ref[:, lanes]
        softplus = jnp.maximum(neg_lam, 0.0) + jnp.log1p(jnp.exp(-jnp.abs(neg_lam)))
        log_a = (-LRU_C * softplus) * r
        a = jnp.exp(log_a)
        a_scr[s, :, lanes] = a
        m = jnp.maximum(1.0 - a * a, 0.0)
        b_scr[s, :, lanes] = jnp.where(m > 0.0, m * lax.rsqrt(m), 0.0) * ig * u
        hcur = h_scr[:, lanes]
        for t in range(tt):
            sl = slice(t * bb, (t + 1) * bb)
            hcur = a_scr[s, sl, lanes] * hcur + b_scr[s, sl, lanes]
            b_scr[s, sl, lanes] = hcur
        h_scr[:, lanes] = hcur
        y_scr[s, :, lanes] = (_gelu_tanh(gate_scr[s, :, lanes]) * b_scr[s, :, lanes]).astype(BF16)

    def pool_piece(s, j):
        t0 = s * tt
        t_idx = lax.broadcasted_iota(jnp.int32, (rows, POOL_GROUP_DIM), 0) // bb
        pos = pos0 + (i * n_sub + s) * tt + t_idx
        groups_per_tile = half // POOL_GROUP_DIM
        deltas = []
        for g in range(j * groups_per_tile, (j + 1) * groups_per_tile):
            win = POOL_WINDOWS[g]
            lanes = slice(g * POOL_GROUP_DIM, (g + 1) * POOL_GROUP_DIM)
            w_sum = pool_ext[t0 * bb:(t0 + POOL_HIST + tt) * bb, lanes]
            span = 1
            while span < win:
                w_sum = w_sum[span * bb:, :] + w_sum[:-span * bb, :]
                span *= 2
            w_sum = w_sum[w_sum.shape[0] - rows:, :]
            cnt = jnp.minimum(pos + 1, win).astype(F32)
            u_pool = pool_ext[(t0 + POOL_HIST) * bb:(t0 + POOL_HIST + tt) * bb, lanes]
            deltas.append(w_sum / cnt - u_pool)
        delta = jnp.concatenate(deltas, axis=1).astype(BF16)
        lanes = slice(j * half, (j + 1) * half)
        y_scr[s, :, LRU_WIDTH + j * half:LRU_WIDTH + (j + 1) * half] = (
            ps_ref[:, lanes] * _dot(delta, pw_ref[j])).astype(BF16)

    def mix_pieces(s):
        return [functools.partial(piece, s, j) for j in range(LRU_WIDTH // half) for piece in (lru_piece, pool_piece)]

    def project_out_pieces(s):
        def piece(n):
            cols = slice(n * half, (n + 1) * half)
            out = x_scr[s, :, cols] + _dot(y_scr[s], wout_ref[:, cols])
            o_ref[:, s * tt:(s + 1) * tt, cols] = jnp.swapaxes(out.reshape(tt, bb, half), 0, 1)
        return [functools.partial(piece, n) for n in range(D_MODEL // half)]

    def emit(*streams):
        for pieces in itertools.zip_longest(*streams):
            for piece in pieces:
                if piece is not None:
                    piece()

    emit(prepare_pieces(0))
    for k in range(1, n_sub + 3):
        mxu, valu = [], []
        if k - 1 < n_sub:
            mxu += project_in_pieces(k - 1)
        if 0 <= k - 3 < n_sub:
            mxu += project_out_pieces(k - 3)
        if k < n_sub:
            valu += prepare_pieces(k)
        if 0 <= k - 2 < n_sub:
            valu += mix_pieces(k - 2)
        emit(mxu, valu)

    new_conv = lru_ext[n_sub * tt * bb:(n_sub * tt + CONV_HIST) * bb, :]
    oconv_ref[...] = new_conv.reshape(CONV_HIST, bb, LRU_WIDTH)
    lru_ext[0:CONV_HIST * bb, :] = new_conv
    new_pool = pool_ext[n_sub * tt * bb:(n_sub * tt + POOL_HIST) * bb, :]
    opool_ref[...] = new_pool.reshape(POOL_HIST, bb, POOL_WIDTH)
    pool_ext[0:POOL_HIST * bb, :] = new_pool
    olru_ref[...] = h_scr[...]


def _mix(x, w, sconv, slru, spool, *, tt, bb, n_sub, pos0):
    n_b, n_t = x.shape[0], x.shape[1]
    rows = tt * bb
    x_spec = pl.BlockSpec((bb, n_sub * tt, D_MODEL), lambda j, i: (j, i, 0))
    conv_spec = pl.BlockSpec((CONV_HIST, bb, LRU_WIDTH), lambda j, i: (0, j, 0))
    lru_spec = pl.BlockSpec((bb, LRU_WIDTH), lambda j, i: (j, 0))
    pool_spec = pl.BlockSpec((POOL_HIST, bb, POOL_WIDTH), lambda j, i: (0, j, 0))
    operands = (x, w["mix_norm"], w["w_in"], w["conv_w"], w["conv_b"], w["lru_wa"], w["lru_ba"],
                w["lru_wx"], w["lru_bx"], w["lru_lambda"], w["pool_w"], w["pool_scale"], w["w_out"],
                sconv, slru, spool)
    return _pallas(
        functools.partial(_mix_kernel, tt=tt, bb=bb, n_sub=n_sub, pos0=pos0), operands,
        grid=(n_b // bb, n_t // (n_sub * tt)),
        in_specs=[x_spec, _const_spec((1, D_MODEL)), _const_spec((D_MODEL, 3 * LRU_WIDTH)),
                  _const_spec((CONV_W, LRU_WIDTH)), _const_spec((1, LRU_WIDTH)),
                  _const_spec((2, MXU_TILE, MXU_TILE)), _const_spec((1, LRU_WIDTH)),
                  _const_spec((2, MXU_TILE, MXU_TILE)), _const_spec((1, LRU_WIDTH)),
                  _const_spec((1, LRU_WIDTH)),
                  _const_spec((2, MXU_TILE, MXU_TILE)), _const_spec((1, POOL_WIDTH)),
                  _const_spec((D_MODEL, D_MODEL)),
                  conv_spec, lru_spec, pool_spec],
        out_specs=[x_spec, conv_spec, lru_spec, pool_spec],
        out_shape=[jax.ShapeDtypeStruct(x.shape, F32),
                   jax.ShapeDtypeStruct((CONV_HIST, n_b, LRU_WIDTH), F32),
                   jax.ShapeDtypeStruct((n_b, LRU_WIDTH), F32),
                   jax.ShapeDtypeStruct((POOL_HIST, n_b, POOL_WIDTH), F32)],
        scratch_shapes=[pltpu.VMEM(((CONV_HIST + n_sub * tt) * bb, LRU_WIDTH), F32),
                        pltpu.VMEM(((POOL_HIST + n_sub * tt) * bb, POOL_WIDTH), F32),
                        pltpu.VMEM((bb, LRU_WIDTH), F32),
                        pltpu.VMEM((n_sub, rows, D_MODEL), F32),
                        pltpu.VMEM((n_sub, rows, D_MODEL), BF16),
                        pltpu.VMEM((n_sub, rows, LRU_WIDTH), F32),
                        pltpu.VMEM((n_sub, rows, LRU_WIDTH), F32),
                        pltpu.VMEM((n_sub, rows, LRU_WIDTH), F32),
                        pltpu.VMEM((n_sub, rows, LRU_WIDTH + POOL_WIDTH), BF16)],
        semantics=("parallel", "arbitrary"),
        name="mix")


def _memkv_kernel(m_ref, g_ref, wk_ref, wv_ref, k_ref, v_ref, kb_ref, vb_ref):
    n_b = m_ref.shape[0]
    m = _rmsnorm(m_ref[...].reshape(n_b * MEM_LEN, D_MODEL), g_ref[...]).astype(BF16)
    for w_ref, o_ref, ob_ref in ((wk_ref, k_ref, kb_ref), (wv_ref, v_ref, vb_ref)):
        kv = _dot(m, w_ref[...])
        for b in range(n_b):
            kv_b = kv[b * MEM_LEN:(b + 1) * MEM_LEN, :]
            ob_ref[b] = kv_b.astype(BF16)
            for hd in range(XA_HEADS):
                o_ref[b, :, hd, :] = kv_b[:, hd * XA_HEAD_DIM:(hd + 1) * XA_HEAD_DIM]


def _memkv(mem, g, wk, wv):
    n_b = mem.shape[0]
    per_step = ROW_TILE // MEM_LEN
    row_spec = pl.BlockSpec((per_step, MEM_LEN, D_MODEL), lambda i: (i, 0, 0))
    head_spec = pl.BlockSpec((per_step, MEM_LEN, XA_HEADS, XA_HEAD_DIM), lambda i: (i, 0, 0, 0))
    return _pallas(
        _memkv_kernel, (mem, g, wk, wv),
        grid=(n_b // per_step,),
        in_specs=[row_spec, _const_spec((1, D_MODEL)), _const_spec((D_MODEL, D_MODEL)),
                  _const_spec((D_MODEL, D_MODEL))],
        out_specs=[head_spec, head_spec, row_spec, row_spec],
        out_shape=[jax.ShapeDtypeStruct((n_b, MEM_LEN, XA_HEADS, XA_HEAD_DIM), F32)] * 2
        + [jax.ShapeDtypeStruct((n_b, MEM_LEN, D_MODEL), BF16)] * 2,
        semantics=("parallel",),
        name="memkv")


def _xattn_kernel(x_ref, g_ref, wq_ref, k_ref, v_ref, wo_ref, qs_ref, kc_hbm, vc_hbm, o_ref, atts_ref,
                  q_scr, s_scr, att_scr, ring, sems, ss_scr, *, tq, per_step):
    subs = range(SUB_TILES)
    rows = [slice(s * ROW_TILE, (s + 1) * ROW_TILE) for s in subs]
    heads = [slice(hd * XA_HEAD_DIM, (hd + 1) * XA_HEAD_DIM) for hd in range(XA_HEADS)]
    scale = XA_HEAD_DIM ** -0.5

    def project_q(s):
        q_scr[s] = _dot(_rmsnorm(x_ref[rows[s], :], g_ref[...]).astype(BF16), wq_ref[...]).astype(BF16)

    def scores(s):
        for hd, cols in enumerate(heads):
            s_scr[s, hd] = lax.dot_general(q_scr[s, :, cols], k_ref[:, cols], (((1,), (1,)), ((), ())),
                                           preferred_element_type=F32) * scale

    def attend(s):
        sc = s_scr[s]
        e = jnp.exp(sc - jnp.max(sc, axis=-1, keepdims=True))
        s_scr[s] = e / jnp.sum(e, axis=-1, keepdims=True)
        for hd, cols in enumerate(heads):
            att_scr[s, :, cols] = _dot(s_scr[s, hd].astype(BF16), v_ref[:, cols]).astype(BF16)

    def project_out(s):
        o_ref[rows[s], :] = x_ref[rows[s], :] + _dot(att_scr[s], wo_ref[...])

    step = pl.program_id(0) * pl.num_programs(1) + pl.program_id(1)
    n_steps = pl.num_programs(0) * pl.num_programs(1)
    n_pairs = per_step // SAMPLE_PAIR
    assert KV_LEAD < n_pairs

    def pair_copies(p, of_step):
        first_seq = of_step * per_step + p * SAMPLE_PAIR
        sources = ((kc_hbm, 0), (kc_hbm, 1), (vc_hbm, 0), (vc_hbm, 1))
        return [pltpu.make_async_copy(
            cache.at[pl.ds(first_seq, SAMPLE_PAIR), :, :, pl.ds(half * LANES, LANES)],
            ring.at[p, j], sems.at[p, j]) for j, (cache, half) in enumerate(sources)]

    @pl.when(step == 0)
    def _():
        for p in range(KV_LEAD):
            for copy in pair_copies(p, 0):
                copy.start()

    def pair_views(p):
        views = [ring.at[p, j].reshape(SAMPLE_PAIR * MEM_LEN * XA_HEADS, LANES) for j in range(4)]

        def head(kv, b, hd):
            r = pl.ds(b * MEM_LEN * XA_HEADS + hd, MEM_LEN, stride=XA_HEADS)
            return jnp.concatenate([views[2 * kv][r, :], views[2 * kv + 1][r, :]], axis=1).astype(BF16)
        return head

    items = [(b, hd) for b in range(SAMPLE_PAIR) for hd in range(XA_HEADS)]

    def sample_scores(p):
        slot = p % 2
        ahead = p + KV_LEAD
        if ahead < n_pairs:
            for copy in pair_copies(ahead, step):
                copy.start()
        else:
            @pl.when(step + 1 < n_steps)
            def _():
                for copy in pair_copies(ahead - n_pairs, step + 1):
                    copy.start()
        for copy in pair_copies(p, step):
            copy.wait()
        head = pair_views(p)
        for n, (b, hd) in enumerate(items):
            r0 = (p * SAMPLE_PAIR + b) * tq
            ss_scr[slot, n * tq:(n + 1) * tq, :] = lax.dot_general(
                qs_ref[r0:r0 + tq, heads[hd]].astype(BF16), head(0, b, hd), (((1,), (1,)), ((), ())),
                preferred_element_type=F32) * scale

    def sample_attend(p):
        slot = p % 2
        head = pair_views(p)
        sc = ss_scr[slot]
        e = jnp.exp(sc - jnp.max(sc, axis=-1, keepdims=True))
        ss_scr[slot] = e / jnp.sum(e, axis=-1, keepdims=True)
        for n, (b, hd) in enumerate(items):
            r0 = (p * SAMPLE_PAIR + b) * tq
            atts_ref[r0:r0 + tq, heads[hd]] = _dot(ss_scr[slot, n * tq:(n + 1) * tq, :].astype(BF16),
                                                  head(1, b, hd))

    stages = (project_q, scores, attend, project_out)
    n_rounds = SUB_TILES + len(stages) - 1
    assert n_pairs + 1 <= n_rounds
    for k in range(n_rounds):
        for depth in reversed(range(len(stages))):
            if 0 <= k - depth < SUB_TILES:
                stages[depth](k - depth)
        if 1 <= k <= n_pairs:
            sample_attend(k - 1)
        if k < n_pairs:
            sample_scores(k)


def _xattn(x, g, wq, k, v, wo, qs, cache_k, cache_v):
    n_b, n_t = x.shape[0], x.shape[1]
    tile = SUB_TILES * ROW_TILE
    n_i = n_t // tile
    n_seq = cache_k.shape[0]
    tq = qs.shape[0] // n_seq
    per_step = n_seq // (n_b * n_i)
    assert per_step * n_b * n_i == n_seq
    x_spec = pl.BlockSpec((None, tile, D_MODEL), lambda b, i: (b, i, 0))
    kv_spec = pl.BlockSpec((None, MEM_LEN, D_MODEL), lambda b, i: (b, 0, 0))
    qs_spec = pl.BlockSpec((per_step * tq, D_MODEL), lambda b, i: (b * n_i + i, 0))
    hbm = pl.BlockSpec(memory_space=pl.ANY)
    return _pallas(
        functools.partial(_xattn_kernel, tq=tq, per_step=per_step), (x, g, wq, k, v, wo, qs, cache_k, cache_v),
        grid=(n_b, n_i),
        in_specs=[x_spec, _const_spec((1, D_MODEL)), _const_spec((D_MODEL, D_MODEL)),
                  kv_spec, kv_spec, _const_spec((D_MODEL, D_MODEL)), qs_spec, hbm, hbm],
        out_specs=[x_spec, qs_spec],
        out_shape=[jax.ShapeDtypeStruct(x.shape, F32), jax.ShapeDtypeStruct(qs.shape, F32)],
        scratch_shapes=[pltpu.VMEM((SUB_TILES, ROW_TILE, D_MODEL), BF16),
                        pltpu.VMEM((SUB_TILES, XA_HEADS, ROW_TILE, MEM_LEN), F32),
                        pltpu.VMEM((SUB_TILES, ROW_TILE, D_MODEL), BF16),
                        pltpu.VMEM((per_step // SAMPLE_PAIR, 4, SAMPLE_PAIR, MEM_LEN, XA_HEADS, LANES), F32),
                        pltpu.SemaphoreType.DMA((per_step // SAMPLE_PAIR, 4)),
                        pltpu.VMEM((2, SAMPLE_PAIR * XA_HEADS * tq, MEM_LEN), F32)],
        semantics=("arbitrary", "arbitrary"),
        name="xattn")


def _sample_q_kernel(x_ref, g_ref, wq_ref, q_ref):
    q_ref[...] = _dot(_rmsnorm(x_ref[...], g_ref[...]).astype(BF16), wq_ref[...])


def _sample_out_kernel(x_ref, a_ref, wo_ref, o_ref):
    o_ref[...] = x_ref[...] + _dot(a_ref[...].astype(BF16), wo_ref[...])


def _sample_q(x, g, wq):
    row_spec = pl.BlockSpec((ROW_TILE, D_MODEL), lambda i: (i, 0))
    return _pallas(
        _sample_q_kernel, (x, g, wq), grid=(x.shape[0] // ROW_TILE,),
        in_specs=[row_spec, _const_spec((1, D_MODEL)), _const_spec((D_MODEL, D_MODEL))],
        out_specs=row_spec, out_shape=jax.ShapeDtypeStruct(x.shape, F32),
        semantics=("parallel",), name="sample_q")


def _sample_out(x, att, wo):
    row_spec = pl.BlockSpec((ROW_TILE, D_MODEL), lambda i: (i, 0))
    return _pallas(
        _sample_out_kernel, (x, att, wo), grid=(x.shape[0] // ROW_TILE,),
        in_specs=[row_spec, row_spec, _const_spec((D_MODEL, D_MODEL))],
        out_specs=row_spec, out_shape=jax.ShapeDtypeStruct(x.shape, F32),
        semantics=("parallel",), name="sample_out")


def _block_diag_tiles(w):
    groups, dg, _ = w.shape
    per_tile = MXU_TILE // dg
    w = w.reshape(groups // per_tile, per_tile, dg, dg)
    eye = jnp.eye(per_tile, dtype=w.dtype)
    tiles = jnp.einsum("npij,pq->npiqj", w, eye)
    return tiles.reshape(groups // per_tile, MXU_TILE, MXU_TILE).astype(BF16)


def _row(v):
    return v.reshape(1, -1).astype(F32)


def _layer(x_prompt, x_sample, mem_prompt, sconv, slru, spool, cache_k, cache_v, p, final_norm):
    n_b, n_t, _ = x_prompt.shape
    s_b, s_t, _ = x_sample.shape
    ffn1 = (_row(p["ffn1_norm"]), p["ffn1_w_gate"], p["ffn1_w_up"], p["ffn1_w_down"], _row(final_norm))
    ffn2 = (_row(p["ffn2_norm"]), p["ffn2_w_gate"], p["ffn2_w_up"], p["ffn2_w_down"], _row(final_norm))
    mixw = {
        "mix_norm": _row(p["mix_norm"]), "w_in": p["w_in"].astype(BF16),
        "conv_w": p["conv_w"].astype(F32), "conv_b": _row(p["conv_b"]),
        "lru_wa": _block_diag_tiles(p["lru_wa"]), "lru_ba": _row(p["lru_ba"]),
        "lru_wx": _block_diag_tiles(p["lru_wx"]), "lru_bx": _row(p["lru_bx"]),
        "lru_lambda": _row(p["lru_lambda"]),
        "pool_w": _block_diag_tiles(p["pool_w"]), "pool_scale": _row(p["pool_scale"]),
        "w_out": p["w_out"].astype(BF16),
    }
    gx, wq, wo = _row(p["xattn_norm"]), p["xattn_wq"].astype(BF16), p["xattn_wo"].astype(BF16)

    mk, mv, mk_bf, mv_bf = _memkv(mem_prompt, _row(p["mem_norm"]),
                                  p["xattn_wk"].astype(BF16), p["xattn_wv"].astype(BF16))

    xp = _ffn(x_prompt.reshape(n_b * n_t, D_MODEL), *ffn1, final_norm=False)
    xs = _ffn(x_sample.reshape(s_b * s_t, D_MODEL), *ffn1, final_norm=False)

    xp, p_conv, p_lru, p_pool = _mix(
        xp.reshape(n_b, n_t, D_MODEL), mixw, jnp.zeros((CONV_HIST, n_b, LRU_WIDTH), F32),
        jnp.zeros((n_b, LRU_WIDTH), F32), jnp.zeros((POOL_HIST, n_b, POOL_WIDTH), F32),
        tt=ROW_TILE // n_b, bb=n_b, n_sub=SUB_TILES, pos0=0)
    xs, s_conv, s_lru, s_pool = _mix(
        xs.reshape(s_b, s_t, D_MODEL), mixw, jnp.swapaxes(sconv, 0, 1), slru, jnp.swapaxes(spool, 0, 1),
        tt=s_t, bb=ROW_TILE // s_t, n_sub=1, pos0=PAST_LEN)
    xs = xs.reshape(s_b * s_t, D_MODEL)

    xp, att_s = _xattn(xp, gx, wq, mk_bf, mv_bf, wo, _sample_q(xs, gx, wq), cache_k, cache_v)
    xs = _sample_out(xs, att_s, wo)

    yp = _ffn(xp.reshape(n_b * n_t, D_MODEL), *ffn2, final_norm=True).reshape(n_b, n_t, D_MODEL)
    ys = _ffn(xs, *ffn2, final_norm=True).reshape(s_b, s_t, D_MODEL)

    states = (jnp.swapaxes(p_conv, 0, 1), p_lru, jnp.swapaxes(p_pool, 0, 1), mk, mv,
              jnp.swapaxes(s_conv, 0, 1), s_lru, jnp.swapaxes(s_pool, 0, 1))
    return yp, ys, states


def kernel(x_prompt, x_sample, mem_prompt, state_conv, state_lru, state_pool, cache_mem_k, cache_mem_v, ffn1_norm, ffn1_w_gate, ffn1_w_up, ffn1_w_down, mix_norm, w_in, conv_w, conv_b, lru_wa, lru_ba, lru_wx, lru_bx, lru_lambda, pool_w, pool_scale, w_out, xattn_norm, mem_norm, xattn_wq, xattn_wk, xattn_wv, xattn_wo, ffn2_norm, ffn2_w_gate, ffn2_w_up, ffn2_w_down, final_norm):
    depth = ffn1_norm.shape[0]
    assert depth == 1, "the final RMSNorm is fused into the last layer's second FFN"
    names = ("ffn1_norm", "ffn1_w_gate", "ffn1_w_up", "ffn1_w_down", "mix_norm", "w_in", "conv_w", "conv_b",
             "lru_wa", "lru_ba", "lru_wx", "lru_bx", "lru_lambda", "pool_w", "pool_scale", "w_out",
             "xattn_norm", "mem_norm", "xattn_wq", "xattn_wk", "xattn_wv", "xattn_wo",
             "ffn2_norm", "ffn2_w_gate", "ffn2_w_up", "ffn2_w_down")
    stacked = (ffn1_norm, ffn1_w_gate, ffn1_w_up, ffn1_w_down, mix_norm, w_in, conv_w, conv_b,
               lru_wa, lru_ba, lru_wx, lru_bx, lru_lambda, pool_w, pool_scale, w_out,
               xattn_norm, mem_norm, xattn_wq, xattn_wk, xattn_wv, xattn_wo,
               ffn2_norm, ffn2_w_gate, ffn2_w_up, ffn2_w_down)
    p = {n: a[0] for n, a in zip(names, stacked)}
    yp, ys, st = _layer(x_prompt, x_sample, mem_prompt, state_conv[0], state_lru[0], state_pool[0],
                        cache_mem_k[0], cache_mem_v[0], p, final_norm)
    p_conv, p_lru, p_pool, p_mk, p_mv, s_conv, s_lru, s_pool = (s[None] for s in st)
    return (yp, ys, p_conv, p_lru, p_pool, p_mk, p_mv, s_conv, s_lru, s_pool)
```

```python
import functools
import itertools
import math

import jax
import jax.numpy as jnp
from jax import lax
from jax.experimental import pallas as pl
from jax.experimental.pallas import tpu as pltpu

F32 = jnp.float32
BF16 = jnp.bfloat16

D_MODEL = 1024
D_FF = 2816
LRU_WIDTH = 512
LRU_C = 8.0
CONV_W = 4
POOL_WIDTH = 512
POOL_WINDOWS = (2, 4, 8, 16)
POOL_GROUP_DIM = POOL_WIDTH // len(POOL_WINDOWS)
POOL_HIST = max(POOL_WINDOWS) - 1
CONV_HIST = CONV_W - 1
MEM_LEN = 256
XA_HEADS = 4
XA_HEAD_DIM = D_MODEL // XA_HEADS
EPS = 1e-6
PAST_LEN = 16384

MXU_TILE = 256
LANES = 128
assert XA_HEAD_DIM == 2 * LANES
FF_CHUNK = MXU_TILE
N_FF_CHUNKS = D_FF // FF_CHUNK
ROW_TILE = 512
SUB_TILES = 2
WEIGHT_LOOKAHEAD = 4
COL_STAGE_SLOTS = 4
ROW_STAGE_SLOTS = 2
SAMPLE_PAIR = 2
KV_LEAD = 2
VMEM_BYTES = 64 * 1024 * 1024
TEMPORARIES_BYTES = SUB_TILES * ROW_TILE * D_MODEL * 4


def _rmsnorm(x, g):
    var = jnp.mean(x * x, axis=-1, keepdims=True)
    return x * lax.rsqrt(var + EPS) * g


def _gelu_tanh(x):
    c0 = (2.0 / jnp.pi) ** 0.5
    half_x = 0.5 * x
    return half_x + half_x * jnp.tanh(x * (c0 + (c0 * 0.044715) * (x * x)))


def _dot(a, b):
    return jnp.dot(a, b, preferred_element_type=F32)


def _const_spec(shape):
    zeros = (0,) * len(shape)
    return pl.BlockSpec(shape, lambda *_: zeros, pipeline_mode=pl.Buffered(1))


def _pallas(kernel, operands, *, grid, in_specs, out_specs, out_shape, semantics, name, scratch_shapes=()):
    def block_bytes(spec, dtype):
        if spec.block_shape is None:
            return 0
        buffers = spec.pipeline_mode.buffer_count if spec.pipeline_mode is not None else 2
        return buffers * math.prod(d or 1 for d in spec.block_shape) * jnp.dtype(dtype).itemsize

    outs = out_shape if isinstance(out_shape, (list, tuple)) else [out_shape]
    out_spec_list = out_specs if isinstance(out_specs, (list, tuple)) else [out_specs]
    need = sum(block_bytes(s, a.dtype) for s, a in zip(in_specs, operands))
    need += sum(block_bytes(s, o.dtype) for s, o in zip(out_spec_list, outs))
    need += sum(math.prod(s.shape) * jnp.dtype(s.dtype).itemsize for s in scratch_shapes
                if s.memory_space == pltpu.MemorySpace.VMEM)
    limit = min(need + TEMPORARIES_BYTES, VMEM_BYTES)
    return pl.pallas_call(
        kernel, grid=grid, in_specs=in_specs, out_specs=out_specs, out_shape=out_shape,
        scratch_shapes=list(scratch_shapes),
        compiler_params=pltpu.CompilerParams(dimension_semantics=semantics, vmem_limit_bytes=limit),
        name=name,
    )(*operands)


def _ffn_weight_fetcher(wg_hbm, wu_hbm, wd_hbm, wg_scr, wu_scr, wd_scr, col_stage, row_stage, sems):
    stages = (col_stage, row_stage)

    def chunk(c):
        cols = pl.ds(c * FF_CHUNK, FF_CHUNK)
        return {"gate": (wg_hbm.at[:, cols], 0, wg_scr.at[c]), "up": (wu_hbm.at[:, cols], 0, wu_scr.at[c]),
                "down": (wd_hbm.at[cols, :], 1, wd_scr.at[c])}

    items = [chunk(0)["gate"], chunk(0)["up"]]
    for c in range(1, N_FF_CHUNKS):
        items += [chunk(c)["gate"], chunk(c)["up"], chunk(c - 1)["down"]]
    items.append(chunk(N_FF_CHUNKS - 1)["down"])
    uses = [0, 0]
    plan = []
    for src, kind, dst in items:
        slot = uses[kind] % stages[kind].shape[0]
        uses[kind] += 1
        staged = stages[kind].at[slot]
        plan.append((pltpu.make_async_copy(src, staged, sems.at[kind, slot]), staged, dst, (kind, slot)))
    slot_free_after = {}
    state = {"next": 0}

    def start(m):
        key = plan[m][3]
        assert slot_free_after.get(key, -1) < state["next"], "staging slot reused before it was converted"
        slot_free_after[key] = m
        plan[m][0].start()

    for m in range(min(WEIGHT_LOOKAHEAD, len(plan))):
        start(m)

    def fetch():
        n = state["next"]
        copy, staged, dst, _ = plan[n]
        copy.wait()
        dst[...] = staged[...].astype(BF16)
        state["next"] = n + 1
        if n + WEIGHT_LOOKAHEAD < len(plan):
            start(n + WEIGHT_LOOKAHEAD)

    return fetch, len(plan)


def _ffn_kernel(x_ref, g_ref, wg_hbm, wu_hbm, wd_hbm, fn_ref, o_ref,
                wg_scr, wu_scr, wd_scr, col_stage, row_stage, sems, h_scr, a_scr, acc_scr,
                *, n_steps, final_norm):
    subs = range(SUB_TILES)
    rows = [slice(s * ROW_TILE, (s + 1) * ROW_TILE) for s in subs]

    def gate_up(s, c):
        h = h_scr[s]
        g = _dot(h, wg_scr[c])
        u = _dot(h, wu_scr[c])
        a_scr[s, c % 2] = (g * jax.nn.sigmoid(g) * u).astype(BF16)

    def down(s, c):
        d = _dot(a_scr[s, c % 2], wd_scr[c])
        if c == 0:
            acc_scr[s] = d
        else:
            acc_scr[s] += d

    def body(load_weights):
        if load_weights:
            fetch, n_items = _ffn_weight_fetcher(wg_hbm, wu_hbm, wd_hbm, wg_scr, wu_scr, wd_scr,
                                                 col_stage, row_stage, sems)
        else:
            fetch, n_items = (lambda: None), 0
        fetch()
        fetch()
        for s in subs:
            h_scr[s] = _rmsnorm(x_ref[rows[s], :], g_ref[...]).astype(BF16)
            gate_up(s, 0)
        for c in range(1, N_FF_CHUNKS):
            fetch()
            fetch()
            fetch()
            for s in subs:
                gate_up(s, c)
                down(s, c - 1)
        fetch()
        assert n_items in (0, 3 * N_FF_CHUNKS)
        for s in subs:
            down(s, N_FF_CHUNKS - 1)
            y = x_ref[rows[s], :] + 0.5 * acc_scr[s]
            if final_norm:
                y = _rmsnorm(y, fn_ref[...])
            o_ref[rows[s], :] = y

    if n_steps == 1:
        body(load_weights=True)
    else:
        @pl.when(pl.program_id(0) == 0)
        def _():
            fetch, n_items = _ffn_weight_fetcher(wg_hbm, wu_hbm, wd_hbm, wg_scr, wu_scr, wd_scr,
                                                 col_stage, row_stage, sems)
            for _ in range(n_items):
                fetch()

        body(load_weights=False)


def _ffn(x, g, wg, wu, wd, fn, *, final_norm):
    row_spec = pl.BlockSpec((SUB_TILES * ROW_TILE, D_MODEL), lambda i: (i, 0))
    hbm = pl.BlockSpec(memory_space=pl.ANY)
    n_steps = x.shape[0] // (SUB_TILES * ROW_TILE)
    return _pallas(
        functools.partial(_ffn_kernel, n_steps=n_steps, final_norm=final_norm), (x, g, wg, wu, wd, fn),
        grid=(n_steps,),
        in_specs=[row_spec, _const_spec((1, D_MODEL)), hbm, hbm, hbm, _const_spec((1, D_MODEL))],
        out_specs=row_spec,
        out_shape=jax.ShapeDtypeStruct(x.shape, F32),
        scratch_shapes=[pltpu.VMEM((N_FF_CHUNKS, D_MODEL, FF_CHUNK), BF16),
                        pltpu.VMEM((N_FF_CHUNKS, D_MODEL, FF_CHUNK), BF16),
                        pltpu.VMEM((N_FF_CHUNKS, FF_CHUNK, D_MODEL), BF16),
                        pltpu.VMEM((COL_STAGE_SLOTS, D_MODEL, FF_CHUNK), F32),
                        pltpu.VMEM((ROW_STAGE_SLOTS, FF_CHUNK, D_MODEL), F32),
                        pltpu.SemaphoreType.DMA((2, COL_STAGE_SLOTS)),
                        pltpu.VMEM((SUB_TILES, ROW_TILE, D_MODEL), BF16),
                        pltpu.VMEM((SUB_TILES, 2, ROW_TILE, FF_CHUNK), BF16),
                        pltpu.VMEM((SUB_TILES, ROW_TILE, D_MODEL), F32)],
        semantics=("arbitrary",),
        name="ffn_final" if final_norm else "ffn")


def _mix_kernel(x_ref, gm_ref, win_ref, cw_ref, cb_ref, wa_ref, ba_ref, wx_ref, bx_ref, lam_ref,
                pw_ref, ps_ref, wout_ref, sconv_ref, slru_ref, spool_ref,
                o_ref, oconv_ref, olru_ref, opool_ref,
                lru_ext, pool_ext, h_scr, x_scr, hn_scr, gate_scr, a_scr, b_scr, y_scr, *, tt, bb, n_sub, pos0):
    rows = tt * bb
    i = pl.program_id(1)

    @pl.when(i == 0)
    def _():
        lru_ext[0:CONV_HIST * bb, :] = sconv_ref[...].reshape(CONV_HIST * bb, LRU_WIDTH)
        pool_ext[0:POOL_HIST * bb, :] = spool_ref[...].reshape(POOL_HIST * bb, POOL_WIDTH)
        h_scr[...] = slru_ref[...]

    half = MXU_TILE

    def prepare_pieces(s):
        n_pieces = max(1, min(4, tt // 8))
        tq = tt // n_pieces

        def piece(q):
            x = jnp.swapaxes(x_ref[:, s * tt + q * tq:s * tt + (q + 1) * tq, :], 0, 1)
            x = x.reshape(tq * bb, D_MODEL)
            x_scr[s, q * tq * bb:(q + 1) * tq * bb, :] = x
            hn_scr[s, q * tq * bb:(q + 1) * tq * bb, :] = _rmsnorm(x, gm_ref[...]).astype(BF16)
        return [functools.partial(piece, q) for q in range(n_pieces)]

    def project_in_pieces(s):
        lru_rows = slice((CONV_HIST + s * tt) * bb, (CONV_HIST + (s + 1) * tt) * bb)
        pool_rows = slice((POOL_HIST + s * tt) * bb, (POOL_HIST + (s + 1) * tt) * bb)

        def piece(n):
            p = _dot(hn_scr[s], win_ref[:, n * half:(n + 1) * half])
            cols = slice((n % 2) * half, (n % 2 + 1) * half)
            if n < 2:
                lru_ext[lru_rows, cols] = p
            elif n < 4:
                gate_scr[s, :, cols] = p
            else:
                pool_ext[pool_rows, cols] = p
        return [functools.partial(piece, n) for n in range(3 * LRU_WIDTH // half)]

    def lru_piece(s, j):
        t0 = s * tt
        lanes = slice(j * half, (j + 1) * half)
        cw = cw_ref[:, lanes]
        u = cb_ref[:, lanes] + sum(lru_ext[(t0 + k) * bb:(t0 + k) * bb + rows, lanes] * cw[k:k + 1, :]
                                   for k in range(CONV_W))
        ub = u.astype(BF16)
        r = jax.nn.sigmoid(_dot(ub, wa_ref[j]) + ba_ref[:, lanes])
        ig = jax.nn.sigmoid(_dot(ub, wx_ref[j]) + bx_ref[:, lanes])
        neg_lam = -lam_ref[:, lanes]
        softplus = jnp.maximum(neg_lam, 0.0) + jnp.log1p(jnp.exp(-jnp.abs(neg_lam)))
        log_a = (-LRU_C * softplus) * r
        a = jnp.exp(log_a)
        a_scr[s, :, lanes] = a
        m = jnp.maximum(1.0 - a * a, 0.0)
        b_scr[s, :, lanes] = jnp.where(m > 0.0, m * lax.rsqrt(m), 0.0) * ig * u
        hcur = h_scr[:, lanes]
        for t in range(tt):
            sl = slice(t * bb, (t + 1) * bb)
            hcur = a_scr[s, sl, lanes] * hcur + b_scr[s, sl, lanes]
            b_scr[s, sl, lanes] = hcur
        h_scr[:, lanes] = hcur
        y_scr[s, :, lanes] = (_gelu_tanh(gate_scr[s, :, lanes]) * b_scr[s, :, lanes]).astype(BF16)

    def pool_piece(s, j):
        t0 = s * tt
        t_idx = lax.broadcasted_iota(jnp.int32, (rows, POOL_GROUP_DIM), 0) // bb
        pos = pos0 + (i * n_sub + s) * tt + t_idx
        groups_per_tile = half // POOL_GROUP_DIM
        deltas = []
        for g in range(j * groups_per_tile, (j + 1) * groups_per_tile):
            win = POOL_WINDOWS[g]
            lanes = slice(g * POOL_GROUP_DIM, (g + 1) * POOL_GROUP_DIM)
            w_sum = pool_ext[t0 * bb:(t0 + POOL_HIST + tt) * bb, lanes]
            span = 1
            while span < win:
                w_sum = w_sum[span * bb:, :] + w_sum[:-span * bb, :]
                span *= 2
            w_sum = w_sum[w_sum.shape[0] - rows:, :]
            cnt = jnp.minimum(pos + 1, win).astype(F32)
            u_pool = pool_ext[(t0 + POOL_HIST) * bb:(t0 + POOL_HIST + tt) * bb, lanes]
            deltas.append(w_sum / cnt - u_pool)
        delta = jnp.concatenate(deltas, axis=1).astype(BF16)
        lanes = slice(j * half, (j + 1) * half)
        y_scr[s, :, LRU_WIDTH + j * half:LRU_WIDTH + (j + 1) * half] = (
            ps_ref[:, lanes] * _dot(delta, pw_ref[j])).astype(BF16)

    def mix_pieces(s):
        return [functools.partial(piece, s, j) for piece in (lru_piece, pool_piece) for j in range(LRU_WIDTH // half)]

    def project_out_pieces(s):
        def piece(n):
            cols = slice(n * half, (n + 1) * half)
            out = x_scr[s, :, cols] + _dot(y_scr[s], wout_ref[:, cols])
            o_ref[:, s * tt:(s + 1) * tt, cols] = jnp.swapaxes(out.reshape(tt, bb, half), 0, 1)
        return [functools.partial(piece, n) for n in range(D_MODEL // half)]

    def emit(*streams):
        for pieces in itertools.zip_longest(*streams):
            for piece in pieces:
                if piece is not None:
                    piece()

    emit(prepare_pieces(0))
    for k in range(1, n_sub + 3):
        mxu, valu = [], []
        if k - 1 < n_sub:
            mxu += project_in_pieces(k - 1)
        if 0 <= k - 3 < n_sub:
            mxu += project_out_pieces(k - 3)
        if k < n_sub:
            valu += prepare_pieces(k)
        if 0 <= k - 2 < n_sub:
            valu += mix_pieces(k - 2)
        emit(mxu, valu)

    new_conv = lru_ext[n_sub * tt * bb:(n_sub * tt + CONV_HIST) * bb, :]
    oconv_ref[...] = new_conv.reshape(CONV_HIST, bb, LRU_WIDTH)
    lru_ext[0:CONV_HIST * bb, :] = new_conv
    new_pool = pool_ext[n_sub * tt * bb:(n_sub * tt + POOL_HIST) * bb, :]
    opool_ref[...] = new_pool.reshape(POOL_HIST, bb, POOL_WIDTH)
    pool_ext[0:POOL_HIST * bb, :] = new_pool
    olru_ref[...] = h_scr[...]


def _mix(x, w, sconv, slru, spool, *, tt, bb, n_sub, pos0):
    n_b, n_t = x.shape[0], x.shape[1]
    rows = tt * bb
    x_spec = pl.BlockSpec((bb, n_sub * tt, D_MODEL), lambda j, i: (j, i, 0))
    conv_spec = pl.BlockSpec((CONV_HIST, bb, LRU_WIDTH), lambda j, i: (0, j, 0))
    lru_spec = pl.BlockSpec((bb, LRU_WIDTH), lambda j, i: (j, 0))
    pool_spec = pl.BlockSpec((POOL_HIST, bb, POOL_WIDTH), lambda j, i: (0, j, 0))
    operands = (x, w["mix_norm"], w["w_in"], w["conv_w"], w["conv_b"], w["lru_wa"], w["lru_ba"],
                w["lru_wx"], w["lru_bx"], w["lru_lambda"], w["pool_w"], w["pool_scale"], w["w_out"],
                sconv, slru, spool)
    return _pallas(
        functools.partial(_mix_kernel, tt=tt, bb=bb, n_sub=n_sub, pos0=pos0), operands,
        grid=(n_b // bb, n_t // (n_sub * tt)),
        in_specs=[x_spec, _const_spec((1, D_MODEL)), _const_spec((D_MODEL, 3 * LRU_WIDTH)),
                  _const_spec((CONV_W, LRU_WIDTH)), _const_spec((1, LRU_WIDTH)),
                  _const_spec((2, MXU_TILE, MXU_TILE)), _const_spec((1, LRU_WIDTH)),
                  _const_spec((2, MXU_TILE, MXU_TILE)), _const_spec((1, LRU_WIDTH)),
                  _const_spec((1, LRU_WIDTH)),
                  _const_spec((2, MXU_TILE, MXU_TILE)), _const_spec((1, POOL_WIDTH)),
                  _const_spec((D_MODEL, D_MODEL)),
                  conv_spec, lru_spec, pool_spec],
        out_specs=[x_spec, conv_spec, lru_spec, pool_spec],
        out_shape=[jax.ShapeDtypeStruct(x.shape, F32),
                   jax.ShapeDtypeStruct((CONV_HIST, n_b, LRU_WIDTH), F32),
                   jax.ShapeDtypeStruct((n_b, LRU_WIDTH), F32),
                   jax.ShapeDtypeStruct((POOL_HIST, n_b, POOL_WIDTH), F32)],
        scratch_shapes=[pltpu.VMEM(((CONV_HIST + n_sub * tt) * bb, LRU_WIDTH), F32),
                        pltpu.VMEM(((POOL_HIST + n_sub * tt) * bb, POOL_WIDTH), F32),
                        pltpu.VMEM((bb, LRU_WIDTH), F32),
                        pltpu.VMEM((n_sub, rows, D_MODEL), F32),
                        pltpu.VMEM((n_sub, rows, D_MODEL), BF16),
                        pltpu.VMEM((n_sub, rows, LRU_WIDTH), F32),
                        pltpu.VMEM((n_sub, rows, LRU_WIDTH), F32),
                        pltpu.VMEM((n_sub, rows, LRU_WIDTH), F32),
                        pltpu.VMEM((n_sub, rows, LRU_WIDTH + POOL_WIDTH), BF16)],
        semantics=("parallel", "arbitrary"),
        name="mix")


def _memkv_kernel(m_ref, g_ref, wk_ref, wv_ref, k_ref, v_ref, kb_ref, vb_ref):
    n_b = m_ref.shape[0]
    m = _rmsnorm(m_ref[...].reshape(n_b * MEM_LEN, D_MODEL), g_ref[...]).astype(BF16)
    for w_ref, o_ref, ob_ref in ((wk_ref, k_ref, kb_ref), (wv_ref, v_ref, vb_ref)):
        kv = _dot(m, w_ref[...].astype(BF16))
        for b in range(n_b):
            kv_b = kv[b * MEM_LEN:(b + 1) * MEM_LEN, :]
            ob_ref[b] = kv_b.astype(BF16)
            for hd in range(XA_HEADS):
                o_ref[b, :, hd, :] = kv_b[:, hd * XA_HEAD_DIM:(hd + 1) * XA_HEAD_DIM]


def _memkv(mem, g, wk, wv):
    n_b = mem.shape[0]
    per_step = ROW_TILE // MEM_LEN
    row_spec = pl.BlockSpec((per_step, MEM_LEN, D_MODEL), lambda i: (i, 0, 0))
    head_spec = pl.BlockSpec((per_step, MEM_LEN, XA_HEADS, XA_HEAD_DIM), lambda i: (i, 0, 0, 0))
    return _pallas(
        _memkv_kernel, (mem, g, wk, wv),
        grid=(n_b // per_step,),
        in_specs=[row_spec, _const_spec((1, D_MODEL)), _const_spec((D_MODEL, D_MODEL)),
                  _const_spec((D_MODEL, D_MODEL))],
        out_specs=[head_spec, head_spec, row_spec, row_spec],
        out_shape=[jax.ShapeDtypeStruct((n_b, MEM_LEN, XA_HEADS, XA_HEAD_DIM), F32)] * 2
        + [jax.ShapeDtypeStruct((n_b, MEM_LEN, D_MODEL), BF16)] * 2,
        semantics=("parallel",),
        name="memkv")


def _xattn_kernel(x_ref, g_ref, wq_ref, k_ref, v_ref, wo_ref, qs_ref, kc_hbm, vc_hbm, o_ref, atts_ref,
                  q_scr, s_scr, att_scr, ring, sems, ss_scr, *, tq, per_step):
    subs = range(SUB_TILES)
    rows = [slice(s * ROW_TILE, (s + 1) * ROW_TILE) for s in subs]
    heads = [slice(hd * XA_HEAD_DIM, (hd + 1) * XA_HEAD_DIM) for hd in range(XA_HEADS)]
    scale = XA_HEAD_DIM ** -0.5

    def project_q(s):
        h = _rmsnorm(x_ref[rows[s], :], g_ref[...]).astype(BF16)
        q_scr[s] = _dot(h, wq_ref[...].astype(BF16)).astype(BF16)

    def scores(s):
        for hd, cols in enumerate(heads):
            s_scr[s, hd] = lax.dot_general(q_scr[s, :, cols], k_ref[:, cols], (((1,), (1,)), ((), ())),
                                           preferred_element_type=F32) * scale

    def attend(s):
        sc = s_scr[s]
        e = jnp.exp(sc - jnp.max(sc, axis=-1, keepdims=True))
        s_scr[s] = e / jnp.sum(e, axis=-1, keepdims=True)
        for hd, cols in enumerate(heads):
            att_scr[s, :, cols] = _dot(s_scr[s, hd].astype(BF16), v_ref[:, cols]).astype(BF16)

    def project_out(s):
        o_ref[rows[s], :] = x_ref[rows[s], :] + _dot(att_scr[s], wo_ref[...].astype(BF16))

    step = pl.program_id(0) * pl.num_programs(1) + pl.program_id(1)
    n_steps = pl.num_programs(0) * pl.num_programs(1)
    n_pairs = per_step // SAMPLE_PAIR
    assert KV_LEAD < n_pairs

    def pair_copies(p, of_step):
        first_seq = of_step * per_step + p * SAMPLE_PAIR
        sources = ((kc_hbm, 0), (kc_hbm, 1), (vc_hbm, 0), (vc_hbm, 1))
        return [pltpu.make_async_copy(
            cache.at[pl.ds(first_seq, SAMPLE_PAIR), :, :, pl.ds(half * LANES, LANES)],
            ring.at[p, j], sems.at[p, j]) for j, (cache, half) in enumerate(sources)]

    @pl.when(step == 0)
    def _():
        for p in range(KV_LEAD):
            for copy in pair_copies(p, 0):
                copy.start()

    def pair_views(p):
        views = [ring.at[p, j].reshape(SAMPLE_PAIR * MEM_LEN * XA_HEADS, LANES) for j in range(4)]

        def head(kv, b, hd):
            r = pl.ds(b * MEM_LEN * XA_HEADS + hd, MEM_LEN, stride=XA_HEADS)
            return jnp.concatenate([views[2 * kv][r, :], views[2 * kv + 1][r, :]], axis=1).astype(BF16)
        return head

    items = [(b, hd) for b in range(SAMPLE_PAIR) for hd in range(XA_HEADS)]

    def sample_scores(p):
        slot = p % 2
        ahead = p + KV_LEAD
        if ahead < n_pairs:
            for copy in pair_copies(ahead, step):
                copy.start()
        else:
            @pl.when(step + 1 < n_steps)
            def _():
                for copy in pair_copies(ahead - n_pairs, step + 1):
                    copy.start()
        for copy in pair_copies(p, step):
            copy.wait()
        head = pair_views(p)
        for n, (b, hd) in enumerate(items):
            r0 = (p * SAMPLE_PAIR + b) * tq
            ss_scr[slot, n * tq:(n + 1) * tq, :] = lax.dot_general(
                qs_ref[r0:r0 + tq, heads[hd]].astype(BF16), head(0, b, hd), (((1,), (1,)), ((), ())),
                preferred_element_type=F32) * scale

    def sample_attend(p):
        slot = p % 2
        head = pair_views(p)
        sc = ss_scr[slot]
        e = jnp.exp(sc - jnp.max(sc, axis=-1, keepdims=True))
        ss_scr[slot] = e / jnp.sum(e, axis=-1, keepdims=True)
        for n, (b, hd) in enumerate(items):
            r0 = (p * SAMPLE_PAIR + b) * tq
            atts_ref[r0:r0 + tq, heads[hd]] = _dot(ss_scr[slot, n * tq:(n + 1) * tq, :].astype(BF16),
                                                  head(1, b, hd))

    stages = (project_q, scores, attend, project_out)
    n_rounds = SUB_TILES + len(stages) - 1
    assert n_pairs + 1 <= n_rounds
    for k in range(n_rounds):
        for depth in reversed(range(len(stages))):
            if 0 <= k - depth < SUB_TILES:
                stages[depth](k - depth)
        if 1 <= k <= n_pairs:
            sample_attend(k - 1)
        if k < n_pairs:
            sample_scores(k)


def _xattn(x, g, wq, k, v, wo, qs, cache_k, cache_v):
    n_b, n_t = x.shape[0], x.shape[1]
    tile = SUB_TILES * ROW_TILE
    n_i = n_t // tile
    n_seq = cache_k.shape[0]
    tq = qs.shape[0] // n_seq
    per_step = n_seq // (n_b * n_i)
    assert per_step * n_b * n_i == n_seq
    x_spec = pl.BlockSpec((None, tile, D_MODEL), lambda b, i: (b, i, 0))
    kv_spec = pl.BlockSpec((None, MEM_LEN, D_MODEL), lambda b, i: (b, 0, 0))
    qs_spec = pl.BlockSpec((per_step * tq, D_MODEL), lambda b, i: (b * n_i + i, 0))
    hbm = pl.BlockSpec(memory_space=pl.ANY)
    return _pallas(
        functools.partial(_xattn_kernel, tq=tq, per_step=per_step), (x, g, wq, k, v, wo, qs, cache_k, cache_v),
        grid=(n_b, n_i),
        in_specs=[x_spec, _const_spec((1, D_MODEL)), _const_spec((D_MODEL, D_MODEL)),
                  kv_spec, kv_spec, _const_spec((D_MODEL, D_MODEL)), qs_spec, hbm, hbm],
        out_specs=[x_spec, qs_spec],
        out_shape=[jax.ShapeDtypeStruct(x.shape, F32), jax.ShapeDtypeStruct(qs.shape, F32)],
        scratch_shapes=[pltpu.VMEM((SUB_TILES, ROW_TILE, D_MODEL), BF16),
                        pltpu.VMEM((SUB_TILES, XA_HEADS, ROW_TILE, MEM_LEN), F32),
                        pltpu.VMEM((SUB_TILES, ROW_TILE, D_MODEL), BF16),
                        pltpu.VMEM((per_step // SAMPLE_PAIR, 4, SAMPLE_PAIR, MEM_LEN, XA_HEADS, LANES), F32),
                        pltpu.SemaphoreType.DMA((per_step // SAMPLE_PAIR, 4)),
                        pltpu.VMEM((2, SAMPLE_PAIR * XA_HEADS * tq, MEM_LEN), F32)],
        semantics=("arbitrary", "arbitrary"),
        name="xattn")


def _sample_q_kernel(x_ref, g_ref, wq_ref, q_ref):
    q_ref[...] = _dot(_rmsnorm(x_ref[...], g_ref[...]).astype(BF16), wq_ref[...].astype(BF16))


def _sample_out_kernel(x_ref, a_ref, wo_ref, o_ref):
    o_ref[...] = x_ref[...] + _dot(a_ref[...].astype(BF16), wo_ref[...].astype(BF16))


def _sample_q(x, g, wq):
    row_spec = pl.BlockSpec((ROW_TILE, D_MODEL), lambda i: (i, 0))
    return _pallas(
        _sample_q_kernel, (x, g, wq), grid=(x.shape[0] // ROW_TILE,),
        in_specs=[row_spec, _const_spec((1, D_MODEL)), _const_spec((D_MODEL, D_MODEL))],
        out_specs=row_spec, out_shape=jax.ShapeDtypeStruct(x.shape, F32),
        semantics=("parallel",), name="sample_q")


def _sample_out(x, att, wo):
    row_spec = pl.BlockSpec((ROW_TILE, D_MODEL), lambda i: (i, 0))
    return _pallas(
        _sample_out_kernel, (x, att, wo), grid=(x.shape[0] // ROW_TILE,),
        in_specs=[row_spec, row_spec, _const_spec((D_MODEL, D_MODEL))],
        out_specs=row_spec, out_shape=jax.ShapeDtypeStruct(x.shape, F32),
        semantics=("parallel",), name="sample_out")


def _block_diag_tiles(w):
    groups, dg, _ = w.shape
    per_tile = MXU_TILE // dg
    w = w.reshape(groups // per_tile, per_tile, dg, dg)
    eye = jnp.eye(per_tile, dtype=w.dtype)
    tiles = jnp.einsum("npij,pq->npiqj", w, eye)
    return tiles.reshape(groups // per_tile, MXU_TILE, MXU_TILE).astype(BF16)


def _row(v):
    return v.reshape(1, -1).astype(F32)


def _layer(x_prompt, x_sample, mem_prompt, sconv, slru, spool, cache_k, cache_v, p, final_norm):
    n_b, n_t, _ = x_prompt.shape
    s_b, s_t, _ = x_sample.shape
    ffn1 = (_row(p["ffn1_norm"]), p["ffn1_w_gate"], p["ffn1_w_up"], p["ffn1_w_down"], _row(final_norm))
    ffn2 = (_row(p["ffn2_norm"]), p["ffn2_w_gate"], p["ffn2_w_up"], p["ffn2_w_down"], _row(final_norm))
    mixw = {
        "mix_norm": _row(p["mix_norm"]), "w_in": p["w_in"].astype(BF16),
        "conv_w": p["conv_w"].astype(F32), "conv_b": _row(p["conv_b"]),
        "lru_wa": _block_diag_tiles(p["lru_wa"]), "lru_ba": _row(p["lru_ba"]),
        "lru_wx": _block_diag_tiles(p["lru_wx"]), "lru_bx": _row(p["lru_bx"]),
        "lru_lambda": _row(p["lru_lambda"]),
        "pool_w": _block_diag_tiles(p["pool_w"]), "pool_scale": _row(p["pool_scale"]),
        "w_out": p["w_out"].astype(BF16),
    }
    gx, wq, wo = _row(p["xattn_norm"]), p["xattn_wq"], p["xattn_wo"]

    mk, mv, mk_bf, mv_bf = _memkv(mem_prompt, _row(p["mem_norm"]),
                                  p["xattn_wk"], p["xattn_wv"])

    xp = _ffn(x_prompt.reshape(n_b * n_t, D_MODEL), *ffn1, final_norm=False)
    xs = _ffn(x_sample.reshape(s_b * s_t, D_MODEL), *ffn1, final_norm=False)

    xp, p_conv, p_lru, p_pool = _mix(
        xp.reshape(n_b, n_t, D_MODEL), mixw, jnp.zeros((CONV_HIST, n_b, LRU_WIDTH), F32),
        jnp.zeros((n_b, LRU_WIDTH), F32), jnp.zeros((POOL_HIST, n_b, POOL_WIDTH), F32),
        tt=ROW_TILE // n_b, bb=n_b, n_sub=SUB_TILES, pos0=0)
    xs, s_conv, s_lru, s_pool = _mix(
        xs.reshape(s_b, s_t, D_MODEL), mixw, jnp.swapaxes(sconv, 0, 1), slru, jnp.swapaxes(spool, 0, 1),
        tt=s_t, bb=ROW_TILE // s_t, n_sub=1, pos0=PAST_LEN)
    xs = xs.reshape(s_b * s_t, D_MODEL)

    xp, att_s = _xattn(xp, gx, wq, mk_bf, mv_bf, wo, _sample_q(xs, gx, wq), cache_k, cache_v)
    xs = _sample_out(xs, att_s, wo)

    yp = _ffn(xp.reshape(n_b * n_t, D_MODEL), *ffn2, final_norm=True).reshape(n_b, n_t, D_MODEL)
    ys = _ffn(xs, *ffn2, final_norm=True).reshape(s_b, s_t, D_MODEL)

    states = (jnp.swapaxes(p_conv, 0, 1), p_lru, jnp.swapaxes(p_pool, 0, 1), mk, mv,
              jnp.swapaxes(s_conv, 0, 1), s_lru, jnp.swapaxes(s_pool, 0, 1))
    return yp, ys, states


def kernel(x_prompt, x_sample, mem_prompt, state_conv, state_lru, state_pool, cache_mem_k, cache_mem_v, ffn1_norm, ffn1_w_gate, ffn1_w_up, ffn1_w_down, mix_norm, w_in, conv_w, conv_b, lru_wa, lru_ba, lru_wx, lru_bx, lru_lambda, pool_w, pool_scale, w_out, xattn_norm, mem_norm, xattn_wq, xattn_wk, xattn_wv, xattn_wo, ffn2_norm, ffn2_w_gate, ffn2_w_up, ffn2_w_down, final_norm):
    depth = ffn1_norm.shape[0]
    assert depth == 1, "the final RMSNorm is fused into the last layer's second FFN"
    names = ("ffn1_norm", "ffn1_w_gate", "ffn1_w_up", "ffn1_w_down", "mix_norm", "w_in", "conv_w", "conv_b",
             "lru_wa", "lru_ba", "lru_wx", "lru_bx", "lru_lambda", "pool_w", "pool_scale", "w_out",
             "xattn_norm", "mem_norm", "xattn_wq", "xattn_wk", "xattn_wv", "xattn_wo",
             "ffn2_norm", "ffn2_w_gate", "ffn2_w_up", "ffn2_w_down")
    stacked = (ffn1_norm, ffn1_w_gate, ffn1_w_up, ffn1_w_down, mix_norm, w_in, conv_w, conv_b,
               lru_wa, lru_ba, lru_wx, lru_bx, lru_lambda, pool_w, pool_scale, w_out,
               xattn_norm, mem_norm, xattn_wq, xattn_wk, xattn_wv, xattn_wo,
               ffn2_norm, ffn2_w_gate, ffn2_w_up, ffn2_w_down)
    p = {n: a[0] for n, a in zip(names, stacked)}
    yp, ys, st = _layer(x_prompt, x_sample, mem_prompt, state_conv[0], state_lru[0], state_pool[0],
                        cache_mem_k[0], cache_mem_v[0], p, final_norm)
    p_conv, p_lru, p_pool, p_mk, p_mv, s_conv, s_lru, s_pool = (s[None] for s in st)
    return (yp, ys, p_conv, p_lru, p_pool, p_mk, p_mv, s_conv, s_lru, s_pool)
```

```python
import functools
import itertools
import math

import jax
import jax.numpy as jnp
from jax import lax
from jax.experimental import pallas as pl
from jax.experimental.pallas import tpu as pltpu

F32 = jnp.float32
BF16 = jnp.bfloat16

D_MODEL = 1024
D_FF = 2816
LRU_WIDTH = 512
LRU_C = 8.0
CONV_W = 4
POOL_WIDTH = 512
POOL_WINDOWS = (2, 4, 8, 16)
POOL_GROUP_DIM = POOL_WIDTH // len(POOL_WINDOWS)
POOL_HIST = max(POOL_WINDOWS) - 1
CONV_HIST = CONV_W - 1
MEM_LEN = 256
XA_HEADS = 4
XA_HEAD_DIM = D_MODEL // XA_HEADS
EPS = 1e-6
PAST_LEN = 16384

MXU_TILE = 256
LANES = 128
assert XA_HEAD_DIM == 2 * LANES
FF_CHUNK = MXU_TILE
N_FF_CHUNKS = D_FF // FF_CHUNK
ROW_TILE = 512
SUB_TILES = 2
WEIGHT_LOOKAHEAD = 4
COL_STAGE_SLOTS = 4
ROW_STAGE_SLOTS = 2
SAMPLE_PAIR = 2
KV_LEAD = 2
VMEM_BYTES = 64 * 1024 * 1024
TEMPORARIES_BYTES = SUB_TILES * ROW_TILE * D_MODEL * 4


def _rmsnorm(x, g):
    var = jnp.mean(x * x, axis=-1, keepdims=True)
    return x * lax.rsqrt(var + EPS) * g


def _gelu_tanh(x):
    c0 = (2.0 / jnp.pi) ** 0.5
    half_x = 0.5 * x
    return half_x + half_x * jnp.tanh(x * (c0 + (c0 * 0.044715) * (x * x)))


def _dot(a, b):
    return jnp.dot(a, b, preferred_element_type=F32)


def _const_spec(shape):
    zeros = (0,) * len(shape)
    return pl.BlockSpec(shape, lambda *_: zeros, pipeline_mode=pl.Buffered(1))


def _pallas(kernel, operands, *, grid, in_specs, out_specs, out_shape, semantics, name, scratch_shapes=()):
    def block_bytes(spec, dtype):
        if spec.block_shape is None:
            return 0
        buffers = spec.pipeline_mode.buffer_count if spec.pipeline_mode is not None else 2
        return buffers * math.prod(d or 1 for d in spec.block_shape) * jnp.dtype(dtype).itemsize

    outs = out_shape if isinstance(out_shape, (list, tuple)) else [out_shape]
    out_spec_list = out_specs if isinstance(out_specs, (list, tuple)) else [out_specs]
    need = sum(block_bytes(s, a.dtype) for s, a in zip(in_specs, operands))
    need += sum(block_bytes(s, o.dtype) for s, o in zip(out_spec_list, outs))
    need += sum(math.prod(s.shape) * jnp.dtype(s.dtype).itemsize for s in scratch_shapes
                if s.memory_space == pltpu.MemorySpace.VMEM)
    limit = min(need + TEMPORARIES_BYTES, VMEM_BYTES)
    return pl.pallas_call(
        kernel, grid=grid, in_specs=in_specs, out_specs=out_specs, out_shape=out_shape,
        scratch_shapes=list(scratch_shapes),
        compiler_params=pltpu.CompilerParams(dimension_semantics=semantics, vmem_limit_bytes=limit),
        name=name,
    )(*operands)


def _ffn_weight_fetcher(wg_hbm, wu_hbm, wd_hbm, wg_scr, wu_scr, wd_scr, col_stage, row_stage, sems):
    stages = (col_stage, row_stage)

    def chunk(c):
        cols = pl.ds(c * FF_CHUNK, FF_CHUNK)
        return {"gate": (wg_hbm.at[:, cols], 0, wg_scr.at[c]), "up": (wu_hbm.at[:, cols], 0, wu_scr.at[c]),
                "down": (wd_hbm.at[cols, :], 1, wd_scr.at[c])}

    items = [chunk(0)["gate"], chunk(0)["up"]]
    for c in range(1, N_FF_CHUNKS):
        items += [chunk(c)["gate"], chunk(c)["up"], chunk(c - 1)["down"]]
    items.append(chunk(N_FF_CHUNKS - 1)["down"])
    uses = [0, 0]
    plan = []
    for src, kind, dst in items:
        slot = uses[kind] % stages[kind].shape[0]
        uses[kind] += 1
        staged = stages[kind].at[slot]
        plan.append((pltpu.make_async_copy(src, staged, sems.at[kind, slot]), staged, dst, (kind, slot)))
    slot_free_after = {}
    state = {"next": 0}

    def start(m):
        key = plan[m][3]
        assert slot_free_after.get(key, -1) < state["next"], "staging slot reused before it was converted"
        slot_free_after[key] = m
        plan[m][0].start()

    for m in range(min(WEIGHT_LOOKAHEAD, len(plan))):
        start(m)

    def fetch():
        n = state["next"]
        copy, staged, dst, _ = plan[n]
        copy.wait()
        dst[...] = staged[...].astype(BF16)
        state["next"] = n + 1
        if n + WEIGHT_LOOKAHEAD < len(plan):
            start(n + WEIGHT_LOOKAHEAD)

    return fetch, len(plan)


def _ffn_kernel(x_ref, g_ref, wg_hbm, wu_hbm, wd_hbm, fn_ref, o_ref,
                wg_scr, wu_scr, wd_scr, col_stage, row_stage, sems, h_scr, a_scr, acc_scr,
                *, n_steps, final_norm):
    subs = range(SUB_TILES)
    rows = [slice(s * ROW_TILE, (s + 1) * ROW_TILE) for s in subs]

    def gate_up(s, c):
        h = h_scr[s]
        g = _dot(h, wg_scr[c])
        u = _dot(h, wu_scr[c])
        a_scr[s, c % 2] = (g * jax.nn.sigmoid(g) * u).astype(BF16)

    def down(s, c):
        d = _dot(a_scr[s, c % 2], wd_scr[c])
        if c == 0:
            acc_scr[s] = d
        else:
            acc_scr[s] += d

    def body(load_weights):
        if load_weights:
            fetch, n_items = _ffn_weight_fetcher(wg_hbm, wu_hbm, wd_hbm, wg_scr, wu_scr, wd_scr,
                                                 col_stage, row_stage, sems)
        else:
            fetch, n_items = (lambda: None), 0
        fetch()
        fetch()
        for s in subs:
            h_scr[s] = _rmsnorm(x_ref[rows[s], :], g_ref[...]).astype(BF16)
            gate_up(s, 0)
        for c in range(1, N_FF_CHUNKS):
            fetch()
            fetch()
            fetch()
            for s in subs:
                gate_up(s, c)
                down(s, c - 1)
        fetch()
        assert n_items in (0, 3 * N_FF_CHUNKS)
        for s in subs:
            down(s, N_FF_CHUNKS - 1)
            y = x_ref[rows[s], :] + 0.5 * acc_scr[s]
            if final_norm:
                y = _rmsnorm(y, fn_ref[...])
            o_ref[rows[s], :] = y

    if n_steps == 1:
        body(load_weights=True)
    else:
        @pl.when(pl.program_id(0) == 0)
        def _():
            fetch, n_items = _ffn_weight_fetcher(wg_hbm, wu_hbm, wd_hbm, wg_scr, wu_scr, wd_scr,
                                                 col_stage, row_stage, sems)
            for _ in range(n_items):
                fetch()

        body(load_weights=False)


def _ffn(x, g, wg, wu, wd, fn, *, final_norm):
    row_spec = pl.BlockSpec((SUB_TILES * ROW_TILE, D_MODEL), lambda i: (i, 0))
    hbm = pl.BlockSpec(memory_space=pl.ANY)
    n_steps = x.shape[0] // (SUB_TILES * ROW_TILE)
    return _pallas(
        functools.partial(_ffn_kernel, n_steps=n_steps, final_norm=final_norm), (x, g, wg, wu, wd, fn),
        grid=(n_steps,),
        in_specs=[row_spec, _const_spec((1, D_MODEL)), hbm, hbm, hbm, _const_spec((1, D_MODEL))],
        out_specs=row_spec,
        out_shape=jax.ShapeDtypeStruct(x.shape, F32),
        scratch_shapes=[pltpu.VMEM((N_FF_CHUNKS, D_MODEL, FF_CHUNK), BF16),
                        pltpu.VMEM((N_FF_CHUNKS, D_MODEL, FF_CHUNK), BF16),
                        pltpu.VMEM((N_FF_CHUNKS, FF_CHUNK, D_MODEL), BF16),
                        pltpu.VMEM((COL_STAGE_SLOTS, D_MODEL, FF_CHUNK), F32),
                        pltpu.VMEM((ROW_STAGE_SLOTS, FF_CHUNK, D_MODEL), F32),
                        pltpu.SemaphoreType.DMA((2, COL_STAGE_SLOTS)),
                        pltpu.VMEM((SUB_TILES, ROW_TILE, D_MODEL), BF16),
                        pltpu.VMEM((SUB_TILES, 2, ROW_TILE, FF_CHUNK), BF16),
                        pltpu.VMEM((SUB_TILES, ROW_TILE, D_MODEL), F32)],
        semantics=("arbitrary",),
        name="ffn_final" if final_norm else "ffn")


def _mix_kernel(x_ref, gm_ref, win_ref, cw_ref, cb_ref, wa_ref, ba_ref, wx_ref, bx_ref, lam_ref,
                pw_ref, ps_ref, wout_ref, sconv_ref, slru_ref, spool_ref,
                o_ref, oconv_ref, olru_ref, opool_ref,
                lru_ext, pool_ext, h_scr, x_scr, hn_scr, gate_scr, a_scr, b_scr, y_scr, *, tt, bb, n_sub, pos0):
    rows = tt * bb
    i = pl.program_id(1)

    @pl.when(i == 0)
    def _():
        lru_ext[0:CONV_HIST * bb, :] = sconv_ref[...].reshape(CONV_HIST * bb, LRU_WIDTH)
        pool_ext[0:POOL_HIST * bb, :] = spool_ref[...].reshape(POOL_HIST * bb, POOL_WIDTH)
        h_scr[...] = slru_ref[...]

    half = MXU_TILE

    def prepare_pieces(s):
        n_pieces = max(1, min(4, tt // 8))
        tq = tt // n_pieces

        def piece(q):
            x = jnp.swapaxes(x_ref[:, s * tt + q * tq:s * tt + (q + 1) * tq, :], 0, 1)
            x = x.reshape(tq * bb, D_MODEL)
            x_scr[s, q * tq * bb:(q + 1) * tq * bb, :] = x
            hn_scr[s, q * tq * bb:(q + 1) * tq * bb, :] = _rmsnorm(x, gm_ref[...]).astype(BF16)
        return [functools.partial(piece, q) for q in range(n_pieces)]

    def project_in_pieces(s):
        lru_rows = slice((CONV_HIST + s * tt) * bb, (CONV_HIST + (s + 1) * tt) * bb)
        pool_rows = slice((POOL_HIST + s * tt) * bb, (POOL_HIST + (s + 1) * tt) * bb)

        def piece(n):
            p = _dot(hn_scr[s], win_ref[:, n * half:(n + 1) * half])
            cols = slice((n % 2) * half, (n % 2 + 1) * half)
            if n < 2:
                lru_ext[lru_rows, cols] = p
            elif n < 4:
                gate_scr[s, :, cols] = p
            else:
                pool_ext[pool_rows, cols] = p
        return [functools.partial(piece, n) for n in range(3 * LRU_WIDTH // half)]

    def lru_piece(s, j):
        t0 = s * tt
        lanes = slice(j * half, (j + 1) * half)
        cw = cw_ref[:, lanes]
        u = cb_ref[:, lanes] + sum(lru_ext[(t0 + k) * bb:(t0 + k) * bb + rows, lanes] * cw[k:k + 1, :]
                                   for k in range(CONV_W))
        ub = u.astype(BF16)
        r = jax.nn.sigmoid(_dot(ub, wa_ref[j]) + ba_ref[:, lanes])
        ig = jax.nn.sigmoid(_dot(ub, wx_ref[j]) + bx_ref[:, lanes])
        neg_lam = -lam_ref[:, lanes]
        softplus = jnp.maximum(neg_lam, 0.0) + jnp.log1p(jnp.exp(-jnp.abs(neg_lam)))
        log_a = (-LRU_C * softplus) * r
        a = jnp.exp(log_a)
        a_scr[s, :, lanes] = a
        m = jnp.maximum(1.0 - a * a, 0.0)
        b_scr[s, :, lanes] = jnp.where(m > 0.0, m * lax.rsqrt(m), 0.0) * ig * u
        hcur = h_scr[:, lanes]
        for t in range(tt):
            sl = slice(t * bb, (t + 1) * bb)
            hcur = a_scr[s, sl, lanes] * hcur + b_scr[s, sl, lanes]
            b_scr[s, sl, lanes] = hcur
        h_scr[:, lanes] = hcur
        y_scr[s, :, lanes] = (_gelu_tanh(gate_scr[s, :, lanes]) * b_scr[s, :, lanes]).astype(BF16)

    def pool_piece(s, j):
        t0 = s * tt
        t_idx = lax.broadcasted_iota(jnp.int32, (rows, POOL_GROUP_DIM), 0) // bb
        pos = pos0 + (i * n_sub + s) * tt + t_idx
        groups_per_tile = half // POOL_GROUP_DIM
        deltas = []
        for g in range(j * groups_per_tile, (j + 1) * groups_per_tile):
            win = POOL_WINDOWS[g]
            lanes = slice(g * POOL_GROUP_DIM, (g + 1) * POOL_GROUP_DIM)
            w_sum = pool_ext[t0 * bb:(t0 + POOL_HIST + tt) * bb, lanes]
            span = 1
            while span < win:
                w_sum = w_sum[span * bb:, :] + w_sum[:-span * bb, :]
                span *= 2
            w_sum = w_sum[w_sum.shape[0] - rows:, :]
            cnt = jnp.minimum(pos + 1, win).astype(F32)
            u_pool = pool_ext[(t0 + POOL_HIST) * bb:(t0 + POOL_HIST + tt) * bb, lanes]
            deltas.append(w_sum / cnt - u_pool)
        delta = jnp.concatenate(deltas, axis=1).astype(BF16)
        lanes = slice(j * half, (j + 1) * half)
        y_scr[s, :, LRU_WIDTH + j * half:LRU_WIDTH + (j + 1) * half] = (
            ps_ref[:, lanes] * _dot(delta, pw_ref[j])).astype(BF16)

    def mix_pieces(s):
        return [functools.partial(piece, s, j) for j in range(LRU_WIDTH // half) for piece in (lru_piece, pool_piece)]

    def project_out_pieces(s):
        def piece(n):
            cols = slice(n * half, (n + 1) * half)
            out = x_scr[s, :, cols] + _dot(y_scr[s], wout_ref[:, cols])
            o_ref[:, s * tt:(s + 1) * tt, cols] = jnp.swapaxes(out.reshape(tt, bb, half), 0, 1)
        return [functools.partial(piece, n) for n in range(D_MODEL // half)]

    def emit(*streams):
        for pieces in itertools.zip_longest(*streams):
            for piece in pieces:
                if piece is not None:
                    piece()

    emit(prepare_pieces(0))
    for k in range(1, n_sub + 3):
        mxu, valu = [], []
        if k - 1 < n_sub:
            mxu += project_in_pieces(k - 1)
        if 0 <= k - 3 < n_sub:
            mxu += project_out_pieces(k - 3)
        if k < n_sub:
            valu += prepare_pieces(k)
        if 0 <= k - 2 < n_sub:
            valu += mix_pieces(k - 2)
        emit(mxu, valu)

    new_conv = lru_ext[n_sub * tt * bb:(n_sub * tt + CONV_HIST) * bb, :]
    oconv_ref[...] = new_conv.reshape(CONV_HIST, bb, LRU_WIDTH)
    lru_ext[0:CONV_HIST * bb, :] = new_conv
    new_pool = pool_ext[n_sub * tt * bb:(n_sub * tt + POOL_HIST) * bb, :]
    opool_ref[...] = new_pool.reshape(POOL_HIST, bb, POOL_WIDTH)
    pool_ext[0:POOL_HIST * bb, :] = new_pool
    olru_ref[...] = h_scr[...]


def _mix(x, w, sconv, slru, spool, *, tt, bb, n_sub, pos0):
    n_b, n_t = x.shape[0], x.shape[1]
    rows = tt * bb
    x_spec = pl.BlockSpec((bb, n_sub * tt, D_MODEL), lambda j, i: (j, i, 0))
    conv_spec = pl.BlockSpec((CONV_HIST, bb, LRU_WIDTH), lambda j, i: (0, j, 0))
    lru_spec = pl.BlockSpec((bb, LRU_WIDTH), lambda j, i: (j, 0))
    pool_spec = pl.BlockSpec((POOL_HIST, bb, POOL_WIDTH), lambda j, i: (0, j, 0))
    operands = (x, w["mix_norm"], w["w_in"], w["conv_w"], w["conv_b"], w["lru_wa"], w["lru_ba"],
                w["lru_wx"], w["lru_bx"], w["lru_lambda"], w["pool_w"], w["pool_scale"], w["w_out"],
                sconv, slru, spool)
    return _pallas(
        functools.partial(_mix_kernel, tt=tt, bb=bb, n_sub=n_sub, pos0=pos0), operands,
        grid=(n_b // bb, n_t // (n_sub * tt)),
        in_specs=[x_spec, _const_spec((1, D_MODEL)), _const_spec((D_MODEL, 3 * LRU_WIDTH)),
                  _const_spec((CONV_W, LRU_WIDTH)), _const_spec((1, LRU_WIDTH)),
                  _const_spec((2, MXU_TILE, MXU_TILE)), _const_spec((1, LRU_WIDTH)),
                  _const_spec((2, MXU_TILE, MXU_TILE)), _const_spec((1, LRU_WIDTH)),
                  _const_spec((1, LRU_WIDTH)),
                  _const_spec((2, MXU_TILE, MXU_TILE)), _const_spec((1, POOL_WIDTH)),
                  _const_spec((D_MODEL, D_MODEL)),
                  conv_spec, lru_spec, pool_spec],
        out_specs=[x_spec, conv_spec, lru_spec, pool_spec],
        out_shape=[jax.ShapeDtypeStruct(x.shape, F32),
                   jax.ShapeDtypeStruct((CONV_HIST, n_b, LRU_WIDTH), F32),
                   jax.ShapeDtypeStruct((n_b, LRU_WIDTH), F32),
                   jax.ShapeDtypeStruct((POOL_HIST, n_b, POOL_WIDTH), F32)],
        scratch_shapes=[pltpu.VMEM(((CONV_HIST + n_sub * tt) * bb, LRU_WIDTH), F32),
                        pltpu.VMEM(((POOL_HIST + n_sub * tt) * bb, POOL_WIDTH), F32),
                        pltpu.VMEM((bb, LRU_WIDTH), F32),
                        pltpu.VMEM((n_sub, rows, D_MODEL), F32),
                        pltpu.VMEM((n_sub, rows, D_MODEL), BF16),
                        pltpu.VMEM((n_sub, rows, LRU_WIDTH), F32),
                        pltpu.VMEM((n_sub, rows, LRU_WIDTH), F32),
                        pltpu.VMEM((n_sub, rows, LRU_WIDTH), F32),
                        pltpu.VMEM((n_sub, rows, LRU_WIDTH + POOL_WIDTH), BF16)],
        semantics=("parallel", "arbitrary"),
        name="mix")


def _memkv_kernel(m_ref, g_ref, wk_ref, wv_ref, k_ref, v_ref, kb_ref, vb_ref):
    n_b = m_ref.shape[0]
    m = _rmsnorm(m_ref[...].reshape(n_b * MEM_LEN, D_MODEL), g_ref[...]).astype(BF16)
    for w_ref, o_ref, ob_ref in ((wk_ref, k_ref, kb_ref), (wv_ref, v_ref, vb_ref)):
        kv = _dot(m, w_ref[...].astype(BF16))
        for b in range(n_b):
            kv_b = kv[b * MEM_LEN:(b + 1) * MEM_LEN, :]
            ob_ref[b] = kv_b.astype(BF16)
            for hd in range(XA_HEADS):
                o_ref[b, :, hd, :] = kv_b[:, hd * XA_HEAD_DIM:(hd + 1) * XA_HEAD_DIM]


def _memkv(mem, g, wk, wv):
    n_b = mem.shape[0]
    per_step = ROW_TILE // MEM_LEN
    row_spec = pl.BlockSpec((per_step, MEM_LEN, D_MODEL), lambda i: (i, 0, 0))
    head_spec = pl.BlockSpec((per_step, MEM_LEN, XA_HEADS, XA_HEAD_DIM), lambda i: (i, 0, 0, 0))
    return _pallas(
        _memkv_kernel, (mem, g, wk, wv),
        grid=(n_b // per_step,),
        in_specs=[row_spec, _const_spec((1, D_MODEL)), _const_spec((D_MODEL, D_MODEL)),
                  _const_spec((D_MODEL, D_MODEL))],
        out_specs=[head_spec, head_spec, row_spec, row_spec],
        out_shape=[jax.ShapeDtypeStruct((n_b, MEM_LEN, XA_HEADS, XA_HEAD_DIM), F32)] * 2
        + [jax.ShapeDtypeStruct((n_b, MEM_LEN, D_MODEL), BF16)] * 2,
        semantics=("parallel",),
        name="memkv")


def _xattn_kernel(x_ref, g_ref, wq_ref, k_ref, v_ref, wo_ref, qs_ref, kc_hbm, vc_hbm, o_ref, atts_ref,
                  q_scr, s_scr, att_scr, ring, sems, ss_scr, *, tq, per_step):
    subs = range(SUB_TILES)
    rows = [slice(s * ROW_TILE, (s + 1) * ROW_TILE) for s in subs]
    heads = [slice(hd * XA_HEAD_DIM, (hd + 1) * XA_HEAD_DIM) for hd in range(XA_HEADS)]
    scale = XA_HEAD_DIM ** -0.5

    def project_q(s):
        h = _rmsnorm(x_ref[rows[s], :], g_ref[...]).astype(BF16)
        q_scr[s] = _dot(h, wq_ref[...].astype(BF16)).astype(BF16)

    def scores(s):
        for hd, cols in enumerate(heads):
            s_scr[s, hd] = lax.dot_general(q_scr[s, :, cols], k_ref[:, cols], (((1,), (1,)), ((), ())),
                                           preferred_element_type=F32) * scale

    def attend(s):
        sc = s_scr[s]
        e = jnp.exp(sc - jnp.max(sc, axis=-1, keepdims=True))
        s_scr[s] = e / jnp.sum(e, axis=-1, keepdims=True)
        for hd, cols in enumerate(heads):
            att_scr[s, :, cols] = _dot(s_scr[s, hd].astype(BF16), v_ref[:, cols]).astype(BF16)

    def project_out(s):
        o_ref[rows[s], :] = x_ref[rows[s], :] + _dot(att_scr[s], wo_ref[...].astype(BF16))

    step = pl.program_id(0) * pl.num_programs(1) + pl.program_id(1)
    n_steps = pl.num_programs(0) * pl.num_programs(1)
    n_pairs = per_step // SAMPLE_PAIR
    assert KV_LEAD < n_pairs

    def pair_copies(p, of_step):
        first_seq = of_step * per_step + p * SAMPLE_PAIR
        sources = ((kc_hbm, 0), (kc_hbm, 1), (vc_hbm, 0), (vc_hbm, 1))
        return [pltpu.make_async_copy(
            cache.at[pl.ds(first_seq, SAMPLE_PAIR), :, :, pl.ds(half * LANES, LANES)],
            ring.at[p, j], sems.at[p, j]) for j, (cache, half) in enumerate(sources)]

    @pl.when(step == 0)
    def _():
        for p in range(KV_LEAD):
            for copy in pair_copies(p, 0):
                copy.start()

    def pair_views(p):
        views = [ring.at[p, j].reshape(SAMPLE_PAIR * MEM_LEN * XA_HEADS, LANES) for j in range(4)]

        def head(kv, b, hd):
            r = pl.ds(b * MEM_LEN * XA_HEADS + hd, MEM_LEN, stride=XA_HEADS)
            return jnp.concatenate([views[2 * kv][r, :], views[2 * kv + 1][r, :]], axis=1).astype(BF16)
        return head

    items = [(b, hd) for b in range(SAMPLE_PAIR) for hd in range(XA_HEADS)]

    def sample_scores(p):
        slot = p % 2
        ahead = p + KV_LEAD
        if ahead < n_pairs:
            for copy in pair_copies(ahead, step):
                copy.start()
        else:
            @pl.when(step + 1 < n_steps)
            def _():
                for copy in pair_copies(ahead - n_pairs, step + 1):
                    copy.start()
        for copy in pair_copies(p, step):
            copy.wait()
        head = pair_views(p)
        for n, (b, hd) in enumerate(items):
            r0 = (p * SAMPLE_PAIR + b) * tq
            ss_scr[slot, n * tq:(n + 1) * tq, :] = lax.dot_general(
                qs_ref[r0:r0 + tq, heads[hd]].astype(BF16), head(0, b, hd), (((1,), (1,)), ((), ())),
                preferred_element_type=F32) * scale

    def sample_attend(p):
        slot = p % 2
        head = pair_views(p)
        sc = ss_scr[slot]
        e = jnp.exp(sc - jnp.max(sc, axis=-1, keepdims=True))
        ss_scr[slot] = e / jnp.sum(e, axis=-1, keepdims=True)
        for n, (b, hd) in enumerate(items):
            r0 = (p * SAMPLE_PAIR + b) * tq
            atts_ref[r0:r0 + tq, heads[hd]] = _dot(ss_scr[slot, n * tq:(n + 1) * tq, :].astype(BF16),
                                                  head(1, b, hd))

    stages = (project_q, scores, attend, project_out)
    n_rounds = SUB_TILES + len(stages) - 1
    assert n_pairs + 1 <= n_rounds
    for k in range(n_rounds):
        for depth in reversed(range(len(stages))):
            if 0 <= k - depth < SUB_TILES:
                stages[depth](k - depth)
        if 1 <= k <= n_pairs:
            sample_attend(k - 1)
        if k < n_pairs:
            sample_scores(k)


def _xattn(x, g, wq, k, v, wo, qs, cache_k, cache_v):
    n_b, n_t = x.shape[0], x.shape[1]
    tile = SUB_TILES * ROW_TILE
    n_i = n_t // tile
    n_seq = cache_k.shape[0]
    tq = qs.shape[0] // n_seq
    per_step = n_seq // (n_b * n_i)
    assert per_step * n_b * n_i == n_seq
    x_spec = pl.BlockSpec((None, tile, D_MODEL), lambda b, i: (b, i, 0))
    kv_spec = pl.BlockSpec((None, MEM_LEN, D_MODEL), lambda b, i: (b, 0, 0))
    qs_spec = pl.BlockSpec((per_step * tq, D_MODEL), lambda b, i: (b * n_i + i, 0))
    hbm = pl.BlockSpec(memory_space=pl.ANY)
    return _pallas(
        functools.partial(_xattn_kernel, tq=tq, per_step=per_step), (x, g, wq, k, v, wo, qs, cache_k, cache_v),
        grid=(n_b, n_i),
        in_specs=[x_spec, _const_spec((1, D_MODEL)), _const_spec((D_MODEL, D_MODEL)),
                  kv_spec, kv_spec, _const_spec((D_MODEL, D_MODEL)), qs_spec, hbm, hbm],
        out_specs=[x_spec, qs_spec],
        out_shape=[jax.ShapeDtypeStruct(x.shape, F32), jax.ShapeDtypeStruct(qs.shape, F32)],
        scratch_shapes=[pltpu.VMEM((SUB_TILES, ROW_TILE, D_MODEL), BF16),
                        pltpu.VMEM((SUB_TILES, XA_HEADS, ROW_TILE, MEM_LEN), F32),
                        pltpu.VMEM((SUB_TILES, ROW_TILE, D_MODEL), BF16),
                        pltpu.VMEM((per_step // SAMPLE_PAIR, 4, SAMPLE_PAIR, MEM_LEN, XA_HEADS, LANES), F32),
                        pltpu.SemaphoreType.DMA((per_step // SAMPLE_PAIR, 4)),
                        pltpu.VMEM((2, SAMPLE_PAIR * XA_HEADS * tq, MEM_LEN), F32)],
        semantics=("arbitrary", "arbitrary"),
        name="xattn")


def _sample_q_kernel(x_ref, g_ref, wq_ref, q_ref):
    q_ref[...] = _dot(_rmsnorm(x_ref[...], g_ref[...]).astype(BF16), wq_ref[...].astype(BF16))


def _sample_out_kernel(x_ref, a_ref, wo_ref, o_ref):
    o_ref[...] = x_ref[...] + _dot(a_ref[...].astype(BF16), wo_ref[...].astype(BF16))


def _sample_q(x, g, wq):
    row_spec = pl.BlockSpec((ROW_TILE, D_MODEL), lambda i: (i, 0))
    return _pallas(
        _sample_q_kernel, (x, g, wq), grid=(x.shape[0] // ROW_TILE,),
        in_specs=[row_spec, _const_spec((1, D_MODEL)), _const_spec((D_MODEL, D_MODEL))],
        out_specs=row_spec, out_shape=jax.ShapeDtypeStruct(x.shape, F32),
        semantics=("parallel",), name="sample_q")


def _sample_out(x, att, wo):
    row_spec = pl.BlockSpec((ROW_TILE, D_MODEL), lambda i: (i, 0))
    return _pallas(
        _sample_out_kernel, (x, att, wo), grid=(x.shape[0] // ROW_TILE,),
        in_specs=[row_spec, row_spec, _const_spec((D_MODEL, D_MODEL))],
        out_specs=row_spec, out_shape=jax.ShapeDtypeStruct(x.shape, F32),
        semantics=("parallel",), name="sample_out")


def _block_diag_tiles(w):
    groups, dg, _ = w.shape
    per_tile = MXU_TILE // dg
    w = w.reshape(groups // per_tile, per_tile, dg, dg)
    eye = jnp.eye(per_tile, dtype=w.dtype)
    tiles = jnp.einsum("npij,pq->npiqj", w, eye)
    return tiles.reshape(groups // per_tile, MXU_TILE, MXU_TILE).astype(BF16)


def _row(v):
    return v.reshape(1, -1).astype(F32)


def _layer(x_prompt, x_sample, mem_prompt, sconv, slru, spool, cache_k, cache_v, p, final_norm):
    n_b, n_t, _ = x_prompt.shape
    s_b, s_t, _ = x_sample.shape
    ffn1 = (_row(p["ffn1_norm"]), p["ffn1_w_gate"], p["ffn1_w_up"], p["ffn1_w_down"], _row(final_norm))
    ffn2 = (_row(p["ffn2_norm"]), p["ffn2_w_gate"], p["ffn2_w_up"], p["ffn2_w_down"], _row(final_norm))
    mixw = {
        "mix_norm": _row(p["mix_norm"]), "w_in": p["w_in"].astype(BF16),
        "conv_w": p["conv_w"].astype(F32), "conv_b": _row(p["conv_b"]),
        "lru_wa": _block_diag_tiles(p["lru_wa"]), "lru_ba": _row(p["lru_ba"]),
        "lru_wx": _block_diag_tiles(p["lru_wx"]), "lru_bx": _row(p["lru_bx"]),
        "lru_lambda": _row(p["lru_lambda"]),
        "pool_w": _block_diag_tiles(p["pool_w"]), "pool_scale": _row(p["pool_scale"]),
        "w_out": p["w_out"].astype(BF16),
    }
    gx, wq, wo = _row(p["xattn_norm"]), p["xattn_wq"], p["xattn_wo"]

    mk, mv, mk_bf, mv_bf = _memkv(mem_prompt, _row(p["mem_norm"]),
                                  p["xattn_wk"], p["xattn_wv"])

    xp = _ffn(x_prompt.reshape(n_b * n_t, D_MODEL), *ffn1, final_norm=False)
    xs = _ffn(x_sample.reshape(s_b * s_t, D_MODEL), *ffn1, final_norm=False)

    xp, p_conv, p_lru, p_pool = _mix(
        xp.reshape(n_b, n_t, D_MODEL), mixw, jnp.zeros((CONV_HIST, n_b, LRU_WIDTH), F32),
        jnp.zeros((n_b, LRU_WIDTH), F32), jnp.zeros((POOL_HIST, n_b, POOL_WIDTH), F32),
        tt=ROW_TILE // n_b, bb=n_b, n_sub=SUB_TILES, pos0=0)
    xs, s_conv, s_lru, s_pool = _mix(
        xs.reshape(s_b, s_t, D_MODEL), mixw, jnp.swapaxes(sconv, 0, 1), slru, jnp.swapaxes(spool, 0, 1),
        tt=s_t, bb=ROW_TILE // s_t, n_sub=1, pos0=PAST_LEN)
    xs = xs.reshape(s_b * s_t, D_MODEL)

    xp, att_s = _xattn(xp, gx, wq, mk_bf, mv_bf, wo, _sample_q(xs, gx, wq), cache_k, cache_v)
    xs = _sample_out(xs, att_s, wo)

    yp = _ffn(xp.reshape(n_b * n_t, D_MODEL), *ffn2, final_norm=True).reshape(n_b, n_t, D_MODEL)
    ys = _ffn(xs, *ffn2, final_norm=True).reshape(s_b, s_t, D_MODEL)

    states = (jnp.swapaxes(p_conv, 0, 1), p_lru, jnp.swapaxes(p_pool, 0, 1), mk, mv,
              jnp.swapaxes(s_conv, 0, 1), s_lru, jnp.swapaxes(s_pool, 0, 1))
    return yp, ys, states


def kernel(x_prompt, x_sample, mem_prompt, state_conv, state_lru, state_pool, cache_mem_k, cache_mem_v, ffn1_norm, ffn1_w_gate, ffn1_w_up, ffn1_w_down, mix_norm, w_in, conv_w, conv_b, lru_wa, lru_ba, lru_wx, lru_bx, lru_lambda, pool_w, pool_scale, w_out, xattn_norm, mem_norm, xattn_wq, xattn_wk, xattn_wv, xattn_wo, ffn2_norm, ffn2_w_gate, ffn2_w_up, ffn2_w_down, final_norm):
    depth = ffn1_norm.shape[0]
    assert depth == 1, "the final RMSNorm is fused into the last layer's second FFN"
    names = ("ffn1_norm", "ffn1_w_gate", "ffn1_w_up", "ffn1_w_down", "mix_norm", "w_in", "conv_w", "conv_b",
             "lru_wa", "lru_ba", "lru_wx", "lru_bx", "lru_lambda", "pool_w", "pool_scale", "w_out",
             "xattn_norm", "mem_norm", "xattn_wq", "xattn_wk", "xattn_wv", "xattn_wo",
             "ffn2_norm", "ffn2_w_gate", "ffn2_w_up", "ffn2_w_down")
    stacked = (ffn1_norm, ffn1_w_gate, ffn1_w_up, ffn1_w_down, mix_norm, w_in, conv_w, conv_b,
               lru_wa, lru_ba, lru_wx, lru_bx, lru_lambda, pool_w, pool_scale, w_out,
               xattn_norm, mem_norm, xattn_wq, xattn_wk, xattn_wv, xattn_wo,
               ffn2_norm, ffn2_w_gate, ffn2_w_up, ffn2_w_down)
    p = {n: a[0] for n, a in zip(names, stacked)}
    yp, ys, st = _layer(x_prompt, x_sample, mem_prompt, state_conv[0], state_lru[0], state_pool[0],
                        cache_mem_k[0], cache_mem_v[0], p, final_norm)
    p_conv, p_lru, p_pool, p_mk, p_mv, s_conv, s_lru, s_pool = (s[None] for s in st)
    return (yp, ys, p_conv, p_lru, p_pool, p_mk, p_mv, s_conv, s_lru, s_pool)
```
